```python
import jax
import jax.numpy as jnp
from jax import lax
import numpy as np

D_MODEL = 4096
BATCH = 4
SEQ = 4096
DEPTH = 2

N_META = 16
CHUNK = 128
BRANCH_WIDTH = D_MODEL // 4
N_BRANCH = 3
RET_DIM = 128
RET_HEADS = BRANCH_WIDTH // RET_DIM
ROPE_BASE = 10000.0
RET_EPS = 1e-5
LRU_BLOCK = 128
LRU_BLOCKS = BRANCH_WIDTH // LRU_BLOCK
CONV_W = 4
LRU_C = 8.0
RWKV_DIM = 64
RWKV_HEADS = BRANCH_WIDTH // RWKV_DIM
DECAY_LORA = 64
AAA_LORA = 64
GATE_LORA = 160
RWKV_EPS = 64e-5
RWKV_COLS = 3 * BRANCH_WIDTH + DECAY_LORA + AAA_LORA + GATE_LORA
IN_SIZES = (BRANCH_WIDTH,) * 4 + (BRANCH_WIDTH,) * 2 + (RWKV_COLS,) + (D_MODEL,) * N_BRANCH
IN_COLS = sum(IN_SIZES)
N_GROUPS = 4
EXPERTS_PER_GROUP = 8
N_EXPERTS = N_GROUPS * EXPERTS_PER_GROUP
TOP_K = 2
D_EXPERT = 512
RMS_EPS = 1e-6

kernel_name = 'hybrid_ret_rglru_rwkv7_hmoe'


def rms_norm(x, g):
    xf = x.astype(jnp.float32)
    y = xf * lax.rsqrt(jnp.mean(xf * xf, axis=-1, keepdims=True) + RMS_EPS)
    return (y * g).astype(x.dtype)


def head_layer_norm(y, eps):
    yf = y.astype(jnp.float32)
    mu = jnp.mean(yf, axis=-1, keepdims=True)
    var = jnp.mean(jnp.square(yf - mu), axis=-1, keepdims=True)
    return (yf - mu) * lax.rsqrt(var + eps)


def rotary(x, pos):
    half = x.shape[-1] // 2
    inv_freq = ROPE_BASE ** (-jnp.arange(half, dtype=jnp.float32) / half)
    ang = pos.astype(jnp.float32)[:, None] * inv_freq[None, :]
    cos, sin = jnp.cos(ang)[:, None, :], jnp.sin(ang)[:, None, :]
    x1, x2 = x[..., :half], x[..., half:]
    return jnp.concatenate([x1 * cos - x2 * sin, x1 * sin + x2 * cos], axis=-1)


def retention(q, k, v, g, gn_g):
    B, T, _ = q.shape
    H, d = RET_HEADS, RET_DIM
    pos = jnp.arange(T)
    q = rotary(q.reshape(B, T, H, d), pos)
    k = rotary(k.reshape(B, T, H, d), pos) * d ** -0.5
    v = v.reshape(B, T, H, d)
    pad = CHUNK - N_META

    def chunked(z):
        z = jnp.pad(z, ((0, 0), (pad, 0), (0, 0), (0, 0)))
        return z.reshape(B, z.shape[1] // CHUNK, CHUNK, H, d)

    qc, kc, vc = chunked(q), chunked(k), chunked(v)
    n_chunks = qc.shape[1]
    log_gamma = jnp.log1p(-(2.0 ** (-5.0 - jnp.arange(H, dtype=jnp.float32))))
    idx = jnp.arange(CHUNK, dtype=jnp.float32)
    diff = idx[:, None] - idx[None, :]
    decay_intra = jnp.where(diff >= 0, jnp.exp(log_gamma[:, None, None] * jnp.maximum(diff, 0.0)), 0.0)
    scores = jnp.einsum('bnchd,bnmhd->bnhcm', qc, kc) * decay_intra
    o = jnp.einsum('bnhcm,bnmhd->bnchd', scores, vc)
    zeta = jnp.exp(log_gamma[:, None] * (CHUNK - 1.0 - idx))
    kv = jnp.einsum('bnmhd,hm,bnmhe->bnhde', kc, zeta, vc)
    ci = jnp.arange(n_chunks, dtype=jnp.float32)
    cdiff = ci[:, None] - ci[None, :] - 1.0
    decay_inter = jnp.where(cdiff >= 0, jnp.exp(log_gamma[:, None, None] * CHUNK * jnp.maximum(cdiff, 0.0)), 0.0)
    state = jnp.einsum('hnj,bjhde->bnhde', decay_inter, kv)
    xi = jnp.exp(log_gamma[:, None] * (idx + 1.0))
    o = o + jnp.einsum('bnchd,hc,bnhde->bnche', qc, xi, state)
    o = o.reshape(B, n_chunks * CHUNK, H, d)[:, pad:]
    o = head_layer_norm(o, RET_EPS).reshape(B, T, H * d) * gn_g
    return (jax.nn.silu(g) * o).astype(q.dtype)


def rg_lru(xb, yb, conv_w, conv_b, wa, ba, wx, bx, lam):
    B, T, C = xb.shape
    xc = lax.conv_general_dilated(xb, conv_w[:, None, :], (1,), [(CONV_W - 1, 0)],
                                  dimension_numbers=('NWC', 'WIO', 'NWC'),
                                  feature_group_count=C) + conv_b
    xh = xc.reshape(B, T, LRU_BLOCKS, LRU_BLOCK)
    r = jax.nn.sigmoid(jnp.einsum('btni,nij->btnj', xh, wa).reshape(B, T, C) + ba)
    i = jax.nn.sigmoid(jnp.einsum('btni,nij->btnj', xh, wx).reshape(B, T, C) + bx)
    log_a = -LRU_C * r.astype(jnp.float32) * jax.nn.softplus(-lam.astype(jnp.float32))
    a = jnp.exp(log_a)
    b = jnp.sqrt(-jnp.expm1(2.0 * log_a)) * (i * xc).astype(jnp.float32)

    def combine(left, right):
        a1, b1 = left
        a2, b2 = right
        return a1 * a2, a2 * b1 + b2

    _, h = lax.associative_scan(combine, (a, b), axis=1)
    return h.astype(xb.dtype) * jax.nn.gelu(yb)


def rwkv7_time_mix(p, mu, w0, w2, a0, a2, g2, k_k, k_a, r_k, ln_g, ln_b):
    B, T, _ = p.shape
    H, N, W = RWKV_HEADS, RWKV_DIM, BRANCH_WIDTH
    p_prev = jnp.pad(p, ((0, 0), (1, 0), (0, 0)))[:, :-1]
    pm = p + (p_prev - p) * mu
    r, k, v = pm[..., :W], pm[..., W:2 * W], pm[..., 2 * W:3 * W]
    o1 = 3 * W
    xw = pm[..., o1:o1 + DECAY_LORA]
    xa = pm[..., o1 + DECAY_LORA:o1 + DECAY_LORA + AAA_LORA]
    xg = pm[..., o1 + DECAY_LORA + AAA_LORA:]
    w = -jax.nn.softplus(-(w0 + jnp.tanh(xw) @ w2)) - 0.5
    decay = jnp.exp(-jnp.exp(w.astype(jnp.float32)))
    a = jax.nn.sigmoid(a0 + xa @ a2)
    g = jax.nn.sigmoid(xg) @ g2
    kk = (k * k_k).reshape(B, T, H, N).astype(jnp.float32)
    kk = kk * lax.rsqrt(jnp.maximum(jnp.sum(kk * kk, axis=-1, keepdims=True), 1e-24))
    k = k * (1.0 + (a - 1.0) * k_a)
    heads = lambda z: z.reshape(B, T, H, N).astype(jnp.float32)
    r_h, k_h, v_h, a_h, w_h = heads(r), heads(k), heads(v), heads(a), heads(decay)
    tmaj = lambda z: z.transpose(1, 0, 2, 3)

    def step(S, inp):
        r_t, w_t, k_t, v_t, kk_t, a_t = inp
        sa = jnp.einsum('bhvk,bhk->bhv', S, -kk_t)
        S = (S * w_t[:, :, None, :] + sa[..., None] * (kk_t * a_t)[:, :, None, :]
             + v_t[..., None] * k_t[:, :, None, :])
        return S, jnp.einsum('bhvk,bhk->bhv', S, r_t)

    S0 = jnp.zeros((B, H, N, N), jnp.float32)
    _, y = lax.scan(step, S0, (tmaj(r_h), tmaj(w_h), tmaj(k_h), tmaj(v_h), tmaj(kk), tmaj(a_h)))
    y = tmaj(y)
    y = head_layer_norm(y, RWKV_EPS).reshape(B, T, W) * ln_g + ln_b
    bonus = jnp.sum(r_h * k_h * r_k, axis=-1, keepdims=True) * v_h
    y = y + bonus.reshape(B, T, W)
    return (y * g).astype(p.dtype)


def hybrid_mixer(a, w_in, ret_gn_g, lru_conv_w, lru_conv_b, lru_wa, lru_ba, lru_wx, lru_bx, lru_lambda,
                 rwkv_mu, rwkv_w0, rwkv_w2, rwkv_a0, rwkv_a2, rwkv_g2, rwkv_k_k, rwkv_k_a, rwkv_r_k,
                 rwkv_ln_g, rwkv_ln_b, w_branch, w_out):
    p = a @ w_in
    cuts = [int(c) for c in np.cumsum(IN_SIZES)[:-1]]
    rq, rk, rv, rg, lx, ly, pr, g_ret, g_lru, g_rwkv = jnp.split(p, cuts, axis=-1)
    o_ret = retention(rq, rk, rv, rg, ret_gn_g)
    o_lru = rg_lru(lx, ly, lru_conv_w, lru_conv_b, lru_wa, lru_ba, lru_wx, lru_bx, lru_lambda)
    o_rwkv = rwkv7_time_mix(pr, rwkv_mu, rwkv_w0, rwkv_w2, rwkv_a0, rwkv_a2, rwkv_g2,
                            rwkv_k_k, rwkv_k_a, rwkv_r_k, rwkv_ln_g, rwkv_ln_b)
    merged = (jax.nn.sigmoid(g_ret) * (o_ret @ w_branch[0])
              + jax.nn.sigmoid(g_lru) * (o_lru @ w_branch[1])
              + jax.nn.sigmoid(g_rwkv) * (o_rwkv @ w_branch[2]))
    return merged @ w_out


def hier_moe(m, wg_r, bg_r, we_r, be_r, w_gate, w_up, w_down):
    B, T, D = m.shape
    xt = m.reshape(B * T, D)
    n_tok = xt.shape[0]
    glog = (xt @ wg_r + bg_r).astype(jnp.float32)
    gprob = jax.nn.softmax(glog, axis=-1)
    gsel = jnp.argmax(glog, axis=-1)
    pg = jnp.take_along_axis(gprob, gsel[:, None], axis=-1)
    elog = (xt @ we_r + be_r).astype(jnp.float32).reshape(n_tok, N_GROUPS, EXPERTS_PER_GROUP)
    elog = jnp.take_along_axis(elog, gsel[:, None, None], axis=1)[:, 0]
    top_v, top_i = lax.top_k(elog, TOP_K)
    weight = jax.nn.softmax(top_v, axis=-1) * pg
    eid = gsel[:, None] * EXPERTS_PER_GROUP + top_i
    combine = jnp.sum(jax.nn.one_hot(eid, N_EXPERTS, dtype=jnp.float32) * weight[..., None], axis=1)
    out = jnp.zeros((n_tok, D), jnp.float32)
    for e in range(N_EXPERTS):
        y = (jax.nn.silu(xt @ w_gate[e]) * (xt @ w_up[e])) @ w_down[e]
        out = out + combine[:, e:e + 1] * y
    return out.reshape(B, T, D)


def setup_inputs(seed: int = 0) -> dict:
    key = jax.random.key(seed)
    ks = jax.random.split(key, 34)
    L, D, W = DEPTH, D_MODEL, BRANCH_WIDTH
    f32 = jnp.float32
    nrm = lambda i, shape, scale: jax.random.normal(ks[i], shape, f32) * scale
    unif = lambda i, shape, lo, hi: jax.random.uniform(ks[i], shape, f32, lo, hi)
    u = unif(12, (L, W), 0.9, 0.999)
    return {
        'x': nrm(0, (BATCH, SEQ, D), 1.0),
        'meta_tokens': nrm(1, (N_META, D), 1.0),
        'final_norm_g': 1.0 + nrm(2, (D,), 0.02),
        'norm1_g': 1.0 + nrm(3, (L, D), 0.02),
        'w_in': nrm(4, (L, D, IN_COLS), D ** -0.5),
        'ret_gn_g': 1.0 + nrm(5, (L, W), 0.02),
        'lru_conv_w': nrm(6, (L, CONV_W, W), 0.5),
        'lru_conv_b': nrm(7, (L, W), 0.02),
        'lru_wa': nrm(8, (L, LRU_BLOCKS, LRU_BLOCK, LRU_BLOCK), LRU_BLOCK ** -0.5),
        'lru_ba': nrm(9, (L, W), 0.02),
        'lru_wx': nrm(10, (L, LRU_BLOCKS, LRU_BLOCK, LRU_BLOCK), LRU_BLOCK ** -0.5),
        'lru_bx': nrm(11, (L, W), 0.02),
        'lru_lambda': jnp.log(u) - jnp.log1p(-u),
        'rwkv_mu': unif(13, (L, RWKV_COLS), 0.0, 1.0),
        'rwkv_w0': unif(14, (L, W), -6.0, -1.0),
        'rwkv_w2': nrm(15, (L, DECAY_LORA, W), 0.1),
        'rwkv_a0': nrm(16, (L, W), 0.1),
        'rwkv_a2': nrm(17, (L, AAA_LORA, W), AAA_LORA ** -0.5),
        'rwkv_g2': nrm(18, (L, GATE_LORA, W), GATE_LORA ** -0.5),
        'rwkv_k_k': 0.85 + nrm(19, (L, W), 0.05),
        'rwkv_k_a': 1.0 + nrm(20, (L, W), 0.05),
        'rwkv_r_k': nrm(21, (L, RWKV_HEADS, RWKV_DIM), 0.1),
        'rwkv_ln_g': 1.0 + nrm(22, (L, W), 0.02),
        'rwkv_ln_b': nrm(23, (L, W), 0.02),
        'w_branch': nrm(24, (L, N_BRANCH, W, D), W ** -0.5),
        'w_out': nrm(25, (L, D, D), D ** -0.5),
        'norm2_g': 1.0 + nrm(26, (L, D), 0.02),
        'moe_wg': nrm(27, (L, D, N_GROUPS), D ** -0.5),
        'moe_bg': nrm(28, (L, N_GROUPS), 0.01),
        'moe_we': nrm(29, (L, D, N_EXPERTS), D ** -0.5),
        'moe_be': nrm(30, (L, N_EXPERTS), 0.01),
        'moe_w_gate': nrm(31, (L, N_EXPERTS, D, D_EXPERT), D ** -0.5),
        'moe_w_up': nrm(32, (L, N_EXPERTS, D, D_EXPERT), D ** -0.5),
        'moe_w_down': nrm(33, (L, N_EXPERTS, D_EXPERT, D), D_EXPERT ** -0.5),
    }


def reference(x, meta_tokens, final_norm_g, norm1_g, w_in, ret_gn_g, lru_conv_w, lru_conv_b,
              lru_wa, lru_ba, lru_wx, lru_bx, lru_lambda, rwkv_mu, rwkv_w0, rwkv_w2, rwkv_a0,
              rwkv_a2, rwkv_g2, rwkv_k_k, rwkv_k_a, rwkv_r_k, rwkv_ln_g, rwkv_ln_b, w_branch,
              w_out, norm2_g, moe_wg, moe_bg, moe_we, moe_be, moe_w_gate, moe_w_up, moe_w_down):
    B = x.shape[0]
    meta = jnp.broadcast_to(meta_tokens.astype(x.dtype)[None], (B, N_META, D_MODEL))
    h = jnp.concatenate([meta, x], axis=1)
    for l in range(DEPTH):
        a = rms_norm(h, norm1_g[l])
        h = h + hybrid_mixer(a, w_in[l], ret_gn_g[l], lru_conv_w[l], lru_conv_b[l], lru_wa[l],
                             lru_ba[l], lru_wx[l], lru_bx[l], lru_lambda[l], rwkv_mu[l],
                             rwkv_w0[l], rwkv_w2[l], rwkv_a0[l], rwkv_a2[l], rwkv_g2[l],
                             rwkv_k_k[l], rwkv_k_a[l], rwkv_r_k[l], rwkv_ln_g[l], rwkv_ln_b[l],
                             w_branch[l], w_out[l]).astype(h.dtype)
        m = rms_norm(h, norm2_g[l])
        h = h + hier_moe(m, moe_wg[l], moe_bg[l], moe_we[l], moe_be[l], moe_w_gate[l],
                         moe_w_up[l], moe_w_down[l]).astype(h.dtype)
    return rms_norm(h, final_norm_g)[:, N_META:]
```

```python
import functools
import math

import jax
import jax.numpy as jnp
from jax import lax
from jax.experimental import pallas as pl
from jax.experimental.pallas import tpu as pltpu

F32 = jnp.float32
BF16 = jnp.bfloat16
HIGHEST = lax.Precision.HIGHEST

LANES = 128
SUBLANES = 8
VMEM_BYTES_V7X = 64 * 1024 * 1024
VMEM_LIMIT = VMEM_BYTES_V7X - 8 * 1024 * 1024

N_META = 16
CHUNK = 128
RET_DIM = 128
ROPE_BASE = 10000.0
RET_EPS = 1e-5
LRU_BLOCK = 128
CONV_W = 4
LRU_C = 8.0
RWKV_DIM = 64
DECAY_LORA = 64
AAA_LORA = 64
GATE_LORA = 160
RWKV_EPS = 64e-5
N_GROUPS = 4
EXPERTS_PER_GROUP = 8
N_EXPERTS = N_GROUPS * EXPERTS_PER_GROUP
TOP_K = 2
RMS_EPS = 1e-6

RWKV_CHUNK = 64
RWKV_TBLK = 128
ROW_TILE = 512
POINT_TILE = 256
MOE_TILE = 512


def _cparams(*sem):
    return pltpu.CompilerParams(dimension_semantics=sem, vmem_limit_bytes=VMEM_LIMIT)


def _dot(a, b, precision=None):
    return jnp.dot(a, b, preferred_element_type=F32, precision=precision)


def _dot_nt(a, b, precision=None):
    return lax.dot_general(a, b, (((1,), (1,)), ((), ())), preferred_element_type=F32, precision=precision)


def _iota(shape, dim):
    return lax.broadcasted_iota(jnp.int32, shape, dim)


def _blk(i, n):
    shift = n.bit_length() - 1
    assert 1 << shift == n
    return lax.shift_right_logical(i, jnp.int32(shift))


def _rmsnorm_kernel(x_ref, g_ref, o_ref):
    x = x_ref[...]
    y = x * lax.rsqrt(jnp.mean(x * x, axis=-1, keepdims=True) + RMS_EPS)
    o_ref[...] = (y * g_ref[...]).astype(o_ref.dtype)


def rmsnorm(x, g, out_dtype, tm=POINT_TILE):
    m, d = x.shape
    return pl.pallas_call(
        _rmsnorm_kernel,
        out_shape=jax.ShapeDtypeStruct((m, d), out_dtype),
        grid=(m // tm,),
        in_specs=[pl.BlockSpec((tm, d), lambda i: (i, 0)), pl.BlockSpec((1, d), lambda i: (0, 0))],
        out_specs=pl.BlockSpec((tm, d), lambda i: (i, 0)),
        compiler_params=_cparams("parallel"),
        name="rmsnorm",
    )(x, g.reshape(1, d))


def _matmul_kernel(a_ref, w_ref, o_ref, *, act):
    acc = _dot(a_ref[...], w_ref[...])
    if act == "sigmoid":
        acc = jax.nn.sigmoid(acc)
    o_ref[...] = acc.astype(o_ref.dtype)


def _matmul_res_kernel(a_ref, w_ref, r_ref, o_ref):
    o_ref[...] = r_ref[...] + _dot(a_ref[...], w_ref[...])


def matmul(a, w, out_dtype, *, tm, tn, act=None, residual=None, name="matmul"):
    m, k = a.shape
    n = w.shape[1]
    assert m % tm == 0 and n % tn == 0, (m, n, tm, tn)
    in_specs = [pl.BlockSpec((tm, k), lambda i, j: (i, 0)), pl.BlockSpec((k, tn), lambda i, j: (0, j))]
    args = [a, w]
    if residual is None:
        body = functools.partial(_matmul_kernel, act=act)
    else:
        body = _matmul_res_kernel
        in_specs.append(pl.BlockSpec((tm, tn), lambda i, j: (i, j)))
        args.append(residual)
    return pl.pallas_call(
        body,
        out_shape=jax.ShapeDtypeStruct((m, n), out_dtype),
        grid=(m // tm, n // tn),
        in_specs=in_specs,
        out_specs=pl.BlockSpec((tm, tn), lambda i, j: (i, j)),
        compiler_params=_cparams("parallel", "arbitrary"),
        name=name,
    )(*args)


def _retention_kernel(q_ref, k_ref, v_ref, g_ref, cos_ref, sin_ref, gn_ref, o_ref, state_ref, *, heads):
    c = pl.program_id(1)

    @pl.when(c == 0)
    def _():
        state_ref[...] = jnp.zeros_like(state_ref)

    d = RET_DIM
    cos = cos_ref[...]
    sin = sin_ref[...]
    row = _iota((CHUNK, CHUNK), 0)
    col = _iota((CHUNK, CHUNK), 1)
    diff = (row - col).astype(F32)
    causal = row >= col
    idx = _iota((CHUNK, 1), 0).astype(F32)
    for h in range(heads):
        lg = math.log1p(-(2.0 ** (-5.0 - h)))
        sl = slice(h * d, (h + 1) * d)
        q = q_ref[0, :, sl].astype(F32)
        k = k_ref[0, :, sl].astype(F32)
        v = v_ref[0, :, sl].astype(BF16)
        g = g_ref[0, :, sl].astype(F32)
        qr = q * cos + pltpu.roll(q, d // 2, 1) * sin
        kr = (k * cos + pltpu.roll(k, d // 2, 1) * sin) * (d ** -0.5)
        decay = jnp.where(causal, jnp.exp(lg * jnp.maximum(diff, 0.0)), 0.0)
        scores = _dot_nt(qr.astype(BF16), kr.astype(BF16)) * decay
        o = _dot(scores.astype(BF16), v)
        xi = jnp.exp(lg * (idx + 1.0))
        st = state_ref[h]
        o = o + _dot((qr * xi).astype(BF16), st.astype(BF16))
        zeta = jnp.exp(lg * (CHUNK - 1.0 - idx))
        kv = _dot((kr * zeta).T.astype(BF16), v)
        state_ref[h] = math.exp(lg * CHUNK) * st + kv
        mu = jnp.mean(o, axis=-1, keepdims=True)
        oc = o - mu
        var = jnp.mean(oc * oc, axis=-1, keepdims=True)
        on = oc * lax.rsqrt(var + RET_EPS) * gn_ref[:, sl]
        o_ref[0, :, sl] = (g * jax.nn.sigmoid(g) * on).astype(o_ref.dtype)


def _rope_tables(tp):
    half = RET_DIM // 2
    inv_freq = ROPE_BASE ** (-jnp.arange(half, dtype=F32) / half)
    ang = jnp.arange(tp, dtype=F32)[:, None] * inv_freq[None, :]
    cos, sin = jnp.cos(ang), jnp.sin(ang)
    return jnp.concatenate([cos, cos], axis=-1), jnp.concatenate([-sin, sin], axis=-1)


def retention(pr, gn_g):
    b, tp, w4 = pr.shape
    w = w4 // 4
    heads = w // RET_DIM
    cos, sin = _rope_tables(tp)
    spec = lambda j: pl.BlockSpec((1, CHUNK, w), lambda bi, c, j=j: (bi, c, j))
    return pl.pallas_call(
        functools.partial(_retention_kernel, heads=heads),
        out_shape=jax.ShapeDtypeStruct((b, tp, w), BF16),
        grid=(b, tp // CHUNK),
        in_specs=[spec(0), spec(1), spec(2), spec(3),
                  pl.BlockSpec((CHUNK, RET_DIM), lambda bi, c: (c, 0)),
                  pl.BlockSpec((CHUNK, RET_DIM), lambda bi, c: (c, 0)),
                  pl.BlockSpec((1, w), lambda bi, c: (0, 0))],
        out_specs=pl.BlockSpec((1, CHUNK, w), lambda bi, c: (bi, c, 0)),
        scratch_shapes=[pltpu.VMEM((heads, RET_DIM, RET_DIM), F32)],
        compiler_params=_cparams("parallel", "arbitrary"),
        name="retention",
    )(pr, pr, pr, pr, cos, sin, gn_g.reshape(1, w))


def _lru_kernel(x_ref, y_ref, cw_ref, cb_ref, wa_ref, ba_ref, wx_ref, bx_ref, lam_ref, o_ref,
                xs_ref, h_ref, *, tc):
    c = pl.program_id(1)
    w = x_ref.shape[-1]

    @pl.when(c == 0)
    def _():
        xs_ref[0:SUBLANES, :] = jnp.zeros((SUBLANES, w), F32)
        h_ref[...] = jnp.zeros_like(h_ref)

    x = x_ref[0].astype(F32)
    xs_ref[SUBLANES:, :] = x
    xc = cb_ref[...] + cw_ref[CONV_W - 1:CONV_W, :] * x
    for j in range(CONV_W - 1):
        off = SUBLANES - (CONV_W - 1) + j
        xc = xc + cw_ref[j:j + 1, :] * xs_ref[off:off + tc, :]
    xs_ref[0:SUBLANES, :] = x[tc - SUBLANES:, :]

    xcb = xc.astype(BF16)
    nb = w // LRU_BLOCK
    ra = jnp.concatenate(
        [_dot(xcb[:, n * LRU_BLOCK:(n + 1) * LRU_BLOCK], wa_ref[n]) for n in range(nb)], axis=-1)
    ri = jnp.concatenate(
        [_dot(xcb[:, n * LRU_BLOCK:(n + 1) * LRU_BLOCK], wx_ref[n]) for n in range(nb)], axis=-1)
    r = jax.nn.sigmoid(ra + ba_ref[...])
    i = jax.nn.sigmoid(ri + bx_ref[...])
    nlam = -lam_ref[...]
    softplus = jnp.maximum(nlam, 0.0) + jnp.log1p(jnp.exp(-jnp.abs(nlam)))
    log_a = -LRU_C * r * softplus
    a = jnp.exp(log_a)
    bb = jnp.sqrt(1.0 - jnp.exp(2.0 * log_a)) * (i * xc)

    row = _iota((tc, w), 0)
    s = 1
    while s < tc:
        a_sh = pltpu.roll(a, s, 0)
        b_sh = pltpu.roll(bb, s, 0)
        m = row >= s
        bb = jnp.where(m, a * b_sh + bb, bb)
        a = jnp.where(m, a * a_sh, a)
        s *= 2
    h = a * h_ref[...] + bb
    h_ref[...] = h[tc - 1:tc, :]
    y = y_ref[0].astype(F32)
    o_ref[0] = (h * jax.nn.gelu(y)).astype(o_ref.dtype)


def rg_lru(pl_xy, conv_w, conv_b, wa, ba, wx, bx, lam, tc=256):
    b, tp, w2 = pl_xy.shape
    w = w2 // 2
    if tp % tc:
        tc = CHUNK
    nb = w // LRU_BLOCK
    vec = lambda: pl.BlockSpec((1, w), lambda bi, c: (0, 0))
    return pl.pallas_call(
        functools.partial(_lru_kernel, tc=tc),
        out_shape=jax.ShapeDtypeStruct((b, tp, w), BF16),
        grid=(b, tp // tc),
        in_specs=[pl.BlockSpec((1, tc, w), lambda bi, c: (bi, c, 0)),
                  pl.BlockSpec((1, tc, w), lambda bi, c: (bi, c, 1)),
                  pl.BlockSpec((CONV_W, w), lambda bi, c: (0, 0)), vec(),
                  pl.BlockSpec((nb, LRU_BLOCK, LRU_BLOCK), lambda bi, c: (0, 0, 0)), vec(),
                  pl.BlockSpec((nb, LRU_BLOCK, LRU_BLOCK), lambda bi, c: (0, 0, 0)), vec(), vec()],
        out_specs=pl.BlockSpec((1, tc, w), lambda bi, c: (bi, c, 0)),
        scratch_shapes=[pltpu.VMEM((tc + SUBLANES, w), F32), pltpu.VMEM((1, w), F32)],
        compiler_params=_cparams("parallel", "arbitrary"),
        name="rg_lru",
    )(pl_xy, pl_xy, conv_w, conv_b.reshape(1, w), wa.astype(BF16), ba.reshape(1, w),
      wx.astype(BF16), bx.reshape(1, w), lam.reshape(1, w))


def _rwkv_proj_kernel(p_ref, prev_ref, mu_ref, w0_ref, w2_ref, a0_ref, a2_ref, g2_ref,
                      r_ref, k_ref, v_ref, lw_ref, a_ref, g_ref, *, w):
    i = pl.program_id(1)
    p = p_ref[0]
    prev = prev_ref[0][SUBLANES - 1:SUBLANES, :]
    prev = jnp.where(i == 0, jnp.zeros_like(prev), prev)
    row = _iota(p.shape, 0)
    p_prev = jnp.where(row == 0, prev, pltpu.roll(p, 1, 0))
    pm = p + (p_prev - p) * mu_ref[...]
    r_ref[0] = pm[:, 0:w]
    k_ref[0] = pm[:, w:2 * w]
    v_ref[0] = pm[:, 2 * w:3 * w]
    xwa = pm[:, 3 * w:3 * w + LANES]
    xg = pm[:, 3 * w + LANES:]
    ww = w0_ref[...] + _dot(jnp.tanh(xwa), w2_ref[...], HIGHEST)
    nw = -ww
    softplus = jnp.maximum(nw, 0.0) + jnp.log1p(jnp.exp(-jnp.abs(nw)))
    lw_ref[0] = -jnp.exp(-softplus - 0.5)
    a_ref[0] = jax.nn.sigmoid(a0_ref[...] + _dot(xwa, a2_ref[...], HIGHEST))
    g_ref[0] = _dot(jax.nn.sigmoid(xg), g2_ref[...], HIGHEST)


def rwkv_proj(pp, mu, w0, w2, a0, a2, g2, w, tm=256):
    b, tp, n = pp.shape
    if tp % tm:
        tm = CHUNK
    nl = n - 3 * w - LANES
    mu_p = jnp.zeros((1, n), F32).at[0, :mu.shape[0]].set(mu)
    w2_p = jnp.zeros((LANES, w), F32).at[:DECAY_LORA].set(w2)
    a2_p = jnp.zeros((LANES, w), F32).at[DECAY_LORA:DECAY_LORA + AAA_LORA].set(a2)
    g2_p = jnp.zeros((nl, w), F32).at[:GATE_LORA].set(g2)
    rows8 = tm // SUBLANES
    out = jax.ShapeDtypeStruct((b, tp, w), F32)
    ospec = pl.BlockSpec((1, tm, w), lambda bi, i: (bi, i, 0))
    vec = lambda: pl.BlockSpec((1, w), lambda bi, i: (0, 0))
    return pl.pallas_call(
        functools.partial(_rwkv_proj_kernel, w=w),
        out_shape=[out] * 6,
        grid=(b, tp // tm),
        in_specs=[pl.BlockSpec((1, tm, n), lambda bi, i: (bi, i, 0)),
                  pl.BlockSpec((1, SUBLANES, n), lambda bi, i: (bi, jnp.maximum(i * rows8 - 1, 0), 0)),
                  pl.BlockSpec((1, n), lambda bi, i: (0, 0)),
                  vec(), pl.BlockSpec((LANES, w), lambda bi, i: (0, 0)),
                  vec(), pl.BlockSpec((LANES, w), lambda bi, i: (0, 0)),
                  pl.BlockSpec((nl, w), lambda bi, i: (0, 0))],
        out_specs=[ospec] * 6,
        compiler_params=_cparams("parallel", "arbitrary"),
        name="rwkv_proj",
    )(pp, pp, mu_p, w0.reshape(1, w), w2_p, a0.reshape(1, w), a2_p, g2_p)


def _rwkv_scan_kernel(r_ref, k_ref, v_ref, lw_ref, a_ref, g_ref, kk_ref, ka_ref, rk_ref, lng_ref, lnb_ref,
                      o_ref, s_ref):
    n = pl.program_id(2)

    @pl.when(n == 0)
    def _():
        s_ref[...] = jnp.zeros_like(s_ref)

    tb = r_ref.shape[1]
    C = RWKV_CHUNK
    nch = tb // C
    hd = RWKV_DIM
    dot = functools.partial(_dot, precision=HIGHEST)

    lane = _iota((1, LANES), 1)
    hmask = [(lane < hd).astype(F32), (lane >= hd).astype(F32)]
    li = _iota((LANES, LANES), 0)
    lj = _iota((LANES, LANES), 1)
    same_head = ((li >= hd) == (lj >= hd)).astype(F32)
    eye = (li == lj).astype(F32)

    r = r_ref[0]
    k = k_ref[0]
    v = v_ref[0]
    lw = lw_ref[0]
    a = a_ref[0]

    kkr = k * kk_ref[...]
    ss = dot(kkr * kkr, same_head)
    kk = kkr * lax.rsqrt(jnp.maximum(ss, 1e-24))
    kmod = k * (1.0 + (a - 1.0) * ka_ref[...])
    be = kk * a

    ti = _iota((tb, tb), 0)
    tj = _iota((tb, tb), 1)
    same_chunk = _blk(ti, C) == _blk(tj, C)
    cw = dot((same_chunk & (ti >= tj)).astype(F32), lw)
    cl = dot(same_chunk.astype(F32), lw)
    e_in = jnp.exp(cw)
    e_out = jnp.exp(-cw)
    e_rem = jnp.exp(cl - cw)
    rt = r * e_in
    at = -kk * jnp.exp(cw - lw)
    kt = kmod * e_out
    bt = be * e_out
    kh = kmod * e_rem
    bh = be * e_rem

    def stack_masked(x):
        return jnp.concatenate([x[c * C:(c + 1) * C] * hmask[h] for c in range(nch) for h in range(2)], axis=0)

    def stack_dup(x):
        return jnp.concatenate([x[c * C:(c + 1) * C] for c in range(nch) for h in range(2)], axis=0)

    sr = 2 * nch * C
    at_s = stack_masked(at)
    rt_s = stack_masked(rt)
    v_s = stack_masked(v)
    gram = _dot_nt(jnp.concatenate([at_s, rt_s], axis=0),
                   jnp.concatenate([stack_dup(bt), stack_dup(kt)], axis=0), HIGHEST)
    ui = _iota((sr, sr), 0)
    uj = _iota((sr, sr), 1)
    unit = _blk(ui, C) == _blk(uj, C)
    strict = unit & (ui > uj)
    incl = unit & (ui >= uj)
    a_ab = jnp.where(strict, gram[:sr, :sr], 0.0)
    a_ak = jnp.where(strict, gram[:sr, sr:], 0.0)
    a_rb = jnp.where(incl, gram[sr:, :sr], 0.0)
    a_rk = jnp.where(incl, gram[sr:, sr:], 0.0)

    x = (ui == uj).astype(F32) + jnp.where(_blk(ui, 2) == _blk(uj, 2), a_ab, 0.0)
    nblk = 2
    while nblk < C:
        e = jnp.where((_blk(ui, 2 * nblk) == _blk(uj, 2 * nblk)) & (_blk(ui, nblk) != _blk(uj, nblk)), a_ab, 0.0)
        x = x + dot(dot(x, e), x)
        nblk *= 2

    ah = dot(x, at_s)
    ul = dot(x, dot(a_ak, v_s))
    rh = rt_s + dot(a_rb, ah)
    yl = dot(a_rb, ul) + dot(a_rk, v_s)
    bh_s = stack_masked(bh)
    kh_s = stack_masked(kh)

    s = s_ref[...]
    ys = []
    for c in range(nch):
        u0 = slice(2 * c * C, (2 * c + 1) * C)
        u1 = slice((2 * c + 1) * C, (2 * c + 2) * C)
        uc = slice(2 * c * C, (2 * c + 2) * C)
        ys.append(dot(rh[u0] + rh[u1], s) + yl[u0] + yl[u1])
        wc = jnp.exp(cl[c * C:c * C + 1, :])
        p = eye * wc + dot(bh_s[uc].T, ah[uc])
        z = dot(bh_s[uc].T, ul[uc]) + dot(kh_s[uc].T, v_s[uc])
        s = dot(p, s) + z
    s_ref[...] = s
    y = jnp.concatenate(ys, axis=0)

    mean_m = same_head * (1.0 / hd)
    mu = dot(y, mean_m)
    yc = y - mu
    var = dot(yc * yc, mean_m)
    yn = yc * lax.rsqrt(var + RWKV_EPS) * lng_ref[...] + lnb_ref[...]
    bonus = dot(r * kmod * rk_ref[...], same_head) * v
    o_ref[0] = ((yn + bonus) * g_ref[0]).astype(o_ref.dtype)


def rwkv_scan(r, k, v, lw, a, g, k_k, k_a, r_k, ln_g, ln_b):
    b, tp, w = r.shape
    tb = RWKV_TBLK
    seq = pl.BlockSpec((1, tb, LANES), lambda bi, j, n: (bi, n, j))
    vec = pl.BlockSpec((1, LANES), lambda bi, j, n: (0, j))
    return pl.pallas_call(
        _rwkv_scan_kernel,
        out_shape=jax.ShapeDtypeStruct((b, tp, w), BF16),
        grid=(b, w // LANES, tp // tb),
        in_specs=[seq] * 6 + [vec] * 5,
        out_specs=seq,
        scratch_shapes=[pltpu.VMEM((LANES, LANES), F32)],
        compiler_params=_cparams("parallel", "parallel", "arbitrary"),
        name="rwkv_scan",
    )(r, k, v, lw, a, g, k_k.reshape(1, w), k_a.reshape(1, w), r_k.reshape(1, w),
      ln_g.reshape(1, w), ln_b.reshape(1, w))


def _merge_kernel(o1_ref, o2_ref, o3_ref, wb_ref, g1_ref, g2_ref, g3_ref, out_ref):
    acc = g1_ref[...].astype(F32) * _dot(o1_ref[...], wb_ref[0])
    acc = acc + g2_ref[...].astype(F32) * _dot(o2_ref[...], wb_ref[1])
    acc = acc + g3_ref[...].astype(F32) * _dot(o3_ref[...], wb_ref[2])
    out_ref[...] = acc.astype(out_ref.dtype)


def merge(o1, o2, o3, w_branch, gates, tm=ROW_TILE, tn=1024):
    m, w = o1.shape
    d = w_branch.shape[2]
    nj = d // tn
    ospec = pl.BlockSpec((tm, w), lambda i, j: (i, 0))
    gspec = lambda b: pl.BlockSpec((tm, tn), lambda i, j, b=b: (i, b * nj + j))
    return pl.pallas_call(
        _merge_kernel,
        out_shape=jax.ShapeDtypeStruct((m, d), BF16),
        grid=(m // tm, nj),
        in_specs=[ospec, ospec, ospec, pl.BlockSpec((3, w, tn), lambda i, j: (0, 0, j)),
                  gspec(0), gspec(1), gspec(2)],
        out_specs=pl.BlockSpec((tm, tn), lambda i, j: (i, j)),
        compiler_params=_cparams("parallel", "arbitrary"),
        name="merge",
    )(o1, o2, o3, w_branch, gates, gates, gates)


def _router_kernel(h_ref, g_ref, wr_ref, br_ref, m_ref, eid_ref, wt_ref):
    x = h_ref[...]
    m = x * lax.rsqrt(jnp.mean(x * x, axis=-1, keepdims=True) + RMS_EPS) * g_ref[...]
    m_ref[...] = m.astype(m_ref.dtype)
    logits = _dot(m, wr_ref[...], HIGHEST) + br_ref[...]
    lane = _iota(logits.shape, 1)
    lanef = lane.astype(F32)
    neg = -jnp.inf
    big = float(LANES)
    glog = jnp.where(lane < N_GROUPS, logits, neg)
    gmax = jnp.max(glog, axis=-1, keepdims=True)
    gsel = jnp.min(jnp.where(glog == gmax, lanef, big), axis=-1, keepdims=True)
    pg = 1.0 / jnp.sum(jnp.exp(glog - gmax), axis=-1, keepdims=True)
    lo = N_GROUPS + gsel * EXPERTS_PER_GROUP
    el = jnp.where((lanef >= lo) & (lanef < lo + EXPERTS_PER_GROUP), logits, neg)
    v1 = jnp.max(el, axis=-1, keepdims=True)
    i1 = jnp.min(jnp.where(el == v1, lanef, big), axis=-1, keepdims=True)
    el2 = jnp.where(lanef == i1, neg, el)
    v2 = jnp.max(el2, axis=-1, keepdims=True)
    i2 = jnp.min(jnp.where(el2 == v2, lanef, big), axis=-1, keepdims=True)
    e2 = jnp.exp(v2 - v1)
    w1 = pg / (1.0 + e2)
    w2 = pg * e2 / (1.0 + e2)
    eid = jnp.where(lane == 0, i1 - N_GROUPS, jnp.where(lane == 1, i2 - N_GROUPS, 0.0))
    eid_ref[...] = eid.astype(jnp.int32)
    wt_ref[...] = jnp.where(lane == 0, w1, jnp.where(lane == 1, w2, 0.0))


def router(h, g, wg, bg, we, be, tm=POINT_TILE):
    m, d = h.shape
    nr = N_GROUPS + N_EXPERTS
    wr = jnp.zeros((d, LANES), F32).at[:, :N_GROUPS].set(wg).at[:, N_GROUPS:nr].set(we)
    br = jnp.zeros((1, LANES), F32).at[0, :N_GROUPS].set(bg).at[0, N_GROUPS:nr].set(be)
    row = pl.BlockSpec((tm, d), lambda i: (i, 0))
    slab = pl.BlockSpec((tm, LANES), lambda i: (i, 0))
    return pl.pallas_call(
        _router_kernel,
        out_shape=[jax.ShapeDtypeStruct((m, d), BF16), jax.ShapeDtypeStruct((m, LANES), jnp.int32),
                   jax.ShapeDtypeStruct((m, LANES), F32)],
        grid=(m // tm,),
        in_specs=[row, pl.BlockSpec((1, d), lambda i: (0, 0)), pl.BlockSpec((d, LANES), lambda i: (0, 0)),
                  pl.BlockSpec((1, LANES), lambda i: (0, 0))],
        out_specs=[row, slab, slab],
        compiler_params=_cparams("parallel"),
        name="moe_router",
    )(h, g.reshape(1, d), wr, br)


def _expert_kernel(te_ref, nu_ref, x_ref, wgu_ref, wd_ref, y_ref, *, de):
    i = pl.program_id(0)

    @pl.when(i < nu_ref[0])
    def _():
        gu = _dot(x_ref[...], wgu_ref[0])
        gt = gu[:, :de]
        act = (gt * jax.nn.sigmoid(gt) * gu[:, de:]).astype(BF16)
        y_ref[...] = _dot(act, wd_ref[0]).astype(y_ref.dtype)

    @pl.when(i >= nu_ref[0])
    def _():
        y_ref[...] = jnp.zeros_like(y_ref)


def expert_ffn(xs, tile_expert, n_used, w_gu, w_down, tm=MOE_TILE):
    p, d = xs.shape
    de = w_down.shape[1]
    grid_spec = pltpu.PrefetchScalarGridSpec(
        num_scalar_prefetch=2,
        grid=(p // tm,),
        in_specs=[pl.BlockSpec((tm, d), lambda i, te, nu: (i, 0)),
                  pl.BlockSpec((1, d, 2 * de), lambda i, te, nu: (te[i], 0, 0)),
                  pl.BlockSpec((1, de, d), lambda i, te, nu: (te[i], 0, 0))],
        out_specs=pl.BlockSpec((tm, d), lambda i, te, nu: (i, 0)),
    )
    return pl.pallas_call(
        functools.partial(_expert_kernel, de=de),
        out_shape=jax.ShapeDtypeStruct((p, d), BF16),
        grid_spec=grid_spec,
        compiler_params=_cparams("arbitrary"),
        name="moe_experts",
    )(tile_expert, n_used, xs, w_gu, w_down)


def _combine_kernel(h_ref, y_ref, wt_ref, o_ref):
    d = h_ref.shape[1]
    wt = wt_ref[...]
    o_ref[...] = (h_ref[...] + wt[:, 0:1] * y_ref[:, :d].astype(F32)
                  + wt[:, 1:2] * y_ref[:, d:].astype(F32))


def combine(h, yp, wt, tm=POINT_TILE):
    m, d = h.shape
    return pl.pallas_call(
        _combine_kernel,
        out_shape=jax.ShapeDtypeStruct((m, d), F32),
        grid=(m // tm,),
        in_specs=[pl.BlockSpec((tm, d), lambda i: (i, 0)), pl.BlockSpec((tm, TOP_K * d), lambda i: (i, 0)),
                  pl.BlockSpec((tm, LANES), lambda i: (i, 0))],
        out_specs=pl.BlockSpec((tm, d), lambda i: (i, 0)),
        compiler_params=_cparams("parallel"),
        name="moe_combine",
    )(h, yp, wt)


def hier_moe(h, norm_g, wg, bg, we, be, w_gu, w_down, tm=MOE_TILE):
    m, d = h.shape
    xm, eid_slab, wt = router(h, norm_g, wg, bg, we, be)
    eid = eid_slab[:, :TOP_K].reshape(m * TOP_K)
    onehot = (eid[:, None] == jnp.arange(N_EXPERTS, dtype=jnp.int32)[None, :]).astype(jnp.int32)
    counts = jnp.sum(onehot, axis=0)
    rank = jnp.sum((jnp.cumsum(onehot, axis=0) - 1) * onehot, axis=1)
    padded = ((counts + tm - 1) // tm) * tm
    ends = jnp.cumsum(padded)
    starts = ends - padded
    dest = starts[eid] + rank
    p_rows = ((m * TOP_K + tm - 1) // tm + N_EXPERTS) * tm
    src = jnp.zeros((p_rows,), jnp.int32).at[dest].set(jnp.arange(m * TOP_K, dtype=jnp.int32) // TOP_K)
    n_tiles = p_rows // tm
    tile_start = jnp.arange(n_tiles, dtype=jnp.int32) * tm
    tile_expert = jnp.minimum(jnp.searchsorted(ends, tile_start, side="right"), N_EXPERTS - 1).astype(jnp.int32)
    n_used = (ends[-1] // tm).astype(jnp.int32).reshape(1)
    tile_expert = jnp.where(tile_start < ends[-1], tile_expert, tile_expert[jnp.maximum(n_used[0] - 1, 0)])
    xs = jnp.take(xm, src, axis=0)
    ys = expert_ffn(xs, tile_expert, n_used, w_gu, w_down, tm)
    yp = jnp.take(ys, dest, axis=0).reshape(m, TOP_K * d)
    return combine(h, yp, wt)


def kernel(x, meta_tokens, final_norm_g, norm1_g, w_in, ret_gn_g, lru_conv_w, lru_conv_b, lru_wa, lru_ba,
           lru_wx, lru_bx, lru_lambda, rwkv_mu, rwkv_w0, rwkv_w2, rwkv_a0, rwkv_a2, rwkv_g2, rwkv_k_k,
           rwkv_k_a, rwkv_r_k, rwkv_ln_g, rwkv_ln_b, w_branch, w_out, norm2_g, moe_wg, moe_bg, moe_we,
           moe_be, moe_w_gate, moe_w_up, moe_w_down):
    b, seq, d = x.shape
    depth = w_in.shape[0]
    w = ret_gn_g.shape[1]
    t = N_META + seq
    tp = ((t + CHUNK - 1) // CHUNK) * CHUNK
    m = b * tp
    rwkv_cols = rwkv_mu.shape[1]
    rwkv_pad = 3 * w + LANES + 2 * LANES
    c_ret, c_lru, c_rwkv = 4 * w, 6 * w, 6 * w + rwkv_cols

    meta = jnp.broadcast_to(meta_tokens.astype(x.dtype)[None], (b, N_META, d))
    h = jnp.concatenate([meta, x, jnp.zeros((b, tp - t, d), x.dtype)], axis=1).reshape(m, d)

    for l in range(depth):
        wl = w_in[l]
        w_ret = wl[:, :c_ret].astype(BF16)
        w_lru = wl[:, c_ret:c_lru].astype(BF16)
        w_rwkv = jnp.pad(wl[:, c_lru:c_rwkv], ((0, 0), (0, rwkv_pad - rwkv_cols))).astype(BF16)
        w_gates = wl[:, c_rwkv:].astype(BF16)

        a = rmsnorm(h, norm1_g[l], BF16)
        p_ret = matmul(a, w_ret, BF16, tm=ROW_TILE, tn=1024, name="proj_ret")
        p_lru = matmul(a, w_lru, F32, tm=ROW_TILE, tn=1024, name="proj_lru")
        p_rwkv = matmul(a, w_rwkv, F32, tm=ROW_TILE, tn=rwkv_pad // 3, name="proj_rwkv")
        gates = matmul(a, w_gates, BF16, tm=ROW_TILE, tn=1024, act="sigmoid", name="proj_gates")

        o_ret = retention(p_ret.reshape(b, tp, 4 * w), ret_gn_g[l])
        o_lru = rg_lru(p_lru.reshape(b, tp, 2 * w), lru_conv_w[l], lru_conv_b[l], lru_wa[l], lru_ba[l],
                       lru_wx[l], lru_bx[l], lru_lambda[l])
        r_, k_, v_, lw_, a_, g_ = rwkv_proj(p_rwkv.reshape(b, tp, rwkv_pad), rwkv_mu[l], rwkv_w0[l],
                                            rwkv_w2[l], rwkv_a0[l], rwkv_a2[l], rwkv_g2[l], w)
        o_rwkv = rwkv_scan(r_, k_, v_, lw_, a_, g_, rwkv_k_k[l], rwkv_k_a[l], rwkv_r_k[l],
                           rwkv_ln_g[l], rwkv_ln_b[l])

        merged = merge(o_ret.reshape(m, w), o_lru.reshape(m, w), o_rwkv.reshape(m, w),
                       w_branch[l].astype(BF16), gates)
        h = matmul(merged, w_out[l].astype(BF16), F32, tm=ROW_TILE, tn=1024, residual=h, name="proj_out")

        w_gu = jnp.concatenate([moe_w_gate[l], moe_w_up[l]], axis=-1).astype(BF16)
        h = hier_moe(h, norm2_g[l], moe_wg[l], moe_bg[l], moe_we[l], moe_be[l], w_gu,
                     moe_w_down[l].astype(BF16))

    out = rmsnorm(h, final_norm_g, F32).reshape(b, tp, d)
    return out[:, N_META:t]
```

```python
import functools
import math

import jax
import jax.numpy as jnp
from jax import lax
from jax.experimental import pallas as pl
from jax.experimental.pallas import tpu as pltpu

F32 = jnp.float32
BF16 = jnp.bfloat16
HIGHEST = lax.Precision.HIGHEST

LANES = 128
SUBLANES = 8
VMEM_BYTES_V7X = 64 * 1024 * 1024
VMEM_LIMIT = VMEM_BYTES_V7X - 8 * 1024 * 1024

N_META = 16
CHUNK = 128
RET_DIM = 128
ROPE_BASE = 10000.0
RET_EPS = 1e-5
LRU_BLOCK = 128
CONV_W = 4
LRU_C = 8.0
RWKV_DIM = 64
DECAY_LORA = 64
AAA_LORA = 64
GATE_LORA = 160
RWKV_EPS = 64e-5
N_GROUPS = 4
EXPERTS_PER_GROUP = 8
N_EXPERTS = N_GROUPS * EXPERTS_PER_GROUP
TOP_K = 2
RMS_EPS = 1e-6

RWKV_CHUNK = 64
RWKV_TBLK = 128
ROW_TILE = 512
POINT_TILE = 256
MOE_TILE = 512


def _cparams(*sem):
    return pltpu.CompilerParams(dimension_semantics=sem, vmem_limit_bytes=VMEM_LIMIT)


def _dot(a, b, precision=None):
    return jnp.dot(a, b, preferred_element_type=F32, precision=precision)


def _dot_nt(a, b, precision=None):
    return lax.dot_general(a, b, (((1,), (1,)), ((), ())), preferred_element_type=F32, precision=precision)


def _iota(shape, dim):
    return lax.broadcasted_iota(jnp.int32, shape, dim)


def _blk(i, n):
    shift = n.bit_length() - 1
    assert 1 << shift == n
    return lax.shift_right_logical(i, jnp.int32(shift))


def _rmsnorm_kernel(x_ref, g_ref, o_ref):
    x = x_ref[...]
    y = x * lax.rsqrt(jnp.mean(x * x, axis=-1, keepdims=True) + RMS_EPS)
    o_ref[...] = (y * g_ref[...]).astype(o_ref.dtype)


def rmsnorm(x, g, out_dtype, tm=POINT_TILE):
    m, d = x.shape
    return pl.pallas_call(
        _rmsnorm_kernel,
        out_shape=jax.ShapeDtypeStruct((m, d), out_dtype),
        grid=(m // tm,),
        in_specs=[pl.BlockSpec((tm, d), lambda i: (i, 0)), pl.BlockSpec((1, d), lambda i: (0, 0))],
        out_specs=pl.BlockSpec((tm, d), lambda i: (i, 0)),
        compiler_params=_cparams("parallel"),
        name="rmsnorm",
    )(x, g.reshape(1, d))


def _matmul_kernel(a_ref, w_ref, o_ref, *, act):
    acc = _dot(a_ref[...], w_ref[...])
    if act == "sigmoid":
        acc = jax.nn.sigmoid(acc)
    o_ref[...] = acc.astype(o_ref.dtype)


def _matmul_res_kernel(a_ref, w_ref, r_ref, o_ref):
    o_ref[...] = r_ref[...] + _dot(a_ref[...], w_ref[...])


def matmul(a, w, out_dtype, *, tm, tn, act=None, residual=None, name="matmul"):
    m, k = a.shape
    n = w.shape[1]
    assert m % tm == 0 and n % tn == 0, (m, n, tm, tn)
    in_specs = [pl.BlockSpec((tm, k), lambda i, j: (i, 0)), pl.BlockSpec((k, tn), lambda i, j: (0, j))]
    args = [a, w]
    if residual is None:
        body = functools.partial(_matmul_kernel, act=act)
    else:
        body = _matmul_res_kernel
        in_specs.append(pl.BlockSpec((tm, tn), lambda i, j: (i, j)))
        args.append(residual)
    return pl.pallas_call(
        body,
        out_shape=jax.ShapeDtypeStruct((m, n), out_dtype),
        grid=(m // tm, n // tn),
        in_specs=in_specs,
        out_specs=pl.BlockSpec((tm, tn), lambda i, j: (i, j)),
        compiler_params=_cparams("parallel", "arbitrary"),
        name=name,
    )(*args)


def _retention_kernel(q_ref, k_ref, v_ref, g_ref, cos_ref, sin_ref, gn_ref, o_ref, state_ref, *, heads):
    c = pl.program_id(1)

    @pl.when(c == 0)
    def _():
        state_ref[...] = jnp.zeros_like(state_ref)

    d = RET_DIM
    cos = cos_ref[...]
    sin = sin_ref[...]
    row = _iota((CHUNK, CHUNK), 0)
    col = _iota((CHUNK, CHUNK), 1)
    diff = (row - col).astype(F32)
    causal = row >= col
    idx = _iota((CHUNK, 1), 0).astype(F32)
    for h in range(heads):
        lg = math.log1p(-(2.0 ** (-5.0 - h)))
        sl = slice(h * d, (h + 1) * d)
        q = q_ref[0, :, sl].astype(F32)
        k = k_ref[0, :, sl].astype(F32)
        v = v_ref[0, :, sl].astype(BF16)
        g = g_ref[0, :, sl].astype(F32)
        qr = q * cos + pltpu.roll(q, d // 2, 1) * sin
        kr = (k * cos + pltpu.roll(k, d // 2, 1) * sin) * (d ** -0.5)
        decay = jnp.where(causal, jnp.exp(lg * jnp.maximum(diff, 0.0)), 0.0)
        scores = _dot_nt(qr.astype(BF16), kr.astype(BF16)) * decay
        o = _dot(scores.astype(BF16), v)
        xi = jnp.exp(lg * (idx + 1.0))
        st = state_ref[h]
        o = o + _dot((qr * xi).astype(BF16), st.astype(BF16))
        zeta = jnp.exp(lg * (CHUNK - 1.0 - idx))
        kv = _dot((kr * zeta).T.astype(BF16), v)
        state_ref[h] = math.exp(lg * CHUNK) * st + kv
        mu = jnp.mean(o, axis=-1, keepdims=True)
        oc = o - mu
        var = jnp.mean(oc * oc, axis=-1, keepdims=True)
        on = oc * lax.rsqrt(var + RET_EPS) * gn_ref[:, sl]
        o_ref[0, :, sl] = (g * jax.nn.sigmoid(g) * on).astype(o_ref.dtype)


def _rope_tables(tp):
    half = RET_DIM // 2
    inv_freq = ROPE_BASE ** (-jnp.arange(half, dtype=F32) / half)
    ang = jnp.arange(tp, dtype=F32)[:, None] * inv_freq[None, :]
    cos, sin = jnp.cos(ang), jnp.sin(ang)
    return jnp.concatenate([cos, cos], axis=-1), jnp.concatenate([-sin, sin], axis=-1)


def retention(pr, gn_g):
    b, tp, w4 = pr.shape
    w = w4 // 4
    heads = w // RET_DIM
    cos, sin = _rope_tables(tp)
    spec = lambda j: pl.BlockSpec((1, CHUNK, w), lambda bi, c, j=j: (bi, c, j))
    return pl.pallas_call(
        functools.partial(_retention_kernel, heads=heads),
        out_shape=jax.ShapeDtypeStruct((b, tp, w), BF16),
        grid=(b, tp // CHUNK),
        in_specs=[spec(0), spec(1), spec(2), spec(3),
                  pl.BlockSpec((CHUNK, RET_DIM), lambda bi, c: (c, 0)),
                  pl.BlockSpec((CHUNK, RET_DIM), lambda bi, c: (c, 0)),
                  pl.BlockSpec((1, w), lambda bi, c: (0, 0))],
        out_specs=pl.BlockSpec((1, CHUNK, w), lambda bi, c: (bi, c, 0)),
        scratch_shapes=[pltpu.VMEM((heads, RET_DIM, RET_DIM), F32)],
        compiler_params=_cparams("parallel", "arbitrary"),
        name="retention",
    )(pr, pr, pr, pr, cos, sin, gn_g.reshape(1, w))


def _lru_kernel(x_ref, y_ref, cw_ref, cb_ref, wa_ref, ba_ref, wx_ref, bx_ref, lam_ref, o_ref,
                xs_ref, h_ref, *, tc):
    c = pl.program_id(1)
    w = x_ref.shape[-1]

    @pl.when(c == 0)
    def _():
        xs_ref[0:SUBLANES, :] = jnp.zeros((SUBLANES, w), F32)
        h_ref[...] = jnp.zeros_like(h_ref)

    x = x_ref[0].astype(F32)
    xs_ref[SUBLANES:, :] = x
    xc = cb_ref[...] + cw_ref[CONV_W - 1:CONV_W, :] * x
    for j in range(CONV_W - 1):
        off = SUBLANES - (CONV_W - 1) + j
        xc = xc + cw_ref[j:j + 1, :] * xs_ref[off:off + tc, :]
    xs_ref[0:SUBLANES, :] = x[tc - SUBLANES:, :]

    xcb = xc.astype(BF16)
    nb = w // LRU_BLOCK
    ra = jnp.concatenate(
        [_dot(xcb[:, n * LRU_BLOCK:(n + 1) * LRU_BLOCK], wa_ref[n]) for n in range(nb)], axis=-1)
    ri = jnp.concatenate(
        [_dot(xcb[:, n * LRU_BLOCK:(n + 1) * LRU_BLOCK], wx_ref[n]) for n in range(nb)], axis=-1)
    r = jax.nn.sigmoid(ra + ba_ref[...])
    i = jax.nn.sigmoid(ri + bx_ref[...])
    nlam = -lam_ref[...]
    softplus = jnp.maximum(nlam, 0.0) + jnp.log1p(jnp.exp(-jnp.abs(nlam)))
    log_a = -LRU_C * r * softplus
    a = jnp.exp(log_a)
    bb = jnp.sqrt(1.0 - jnp.exp(2.0 * log_a)) * (i * xc)

    row = _iota((tc, w), 0)
    s = 1
    while s < tc:
        a_sh = pltpu.roll(a, s, 0)
        b_sh = pltpu.roll(bb, s, 0)
        m = row >= s
        bb = jnp.where(m, a * b_sh + bb, bb)
        a = jnp.where(m, a * a_sh, a)
        s *= 2
    h = a * h_ref[...] + bb
    h_ref[...] = h[tc - 1:tc, :]
    y = y_ref[0].astype(F32)
    o_ref[0] = (h * jax.nn.gelu(y)).astype(o_ref.dtype)


def rg_lru(pl_xy, conv_w, conv_b, wa, ba, wx, bx, lam, tc=256):
    b, tp, w2 = pl_xy.shape
    w = w2 // 2
    if tp % tc:
        tc = CHUNK
    nb = w // LRU_BLOCK
    vec = lambda: pl.BlockSpec((1, w), lambda bi, c: (0, 0))
    return pl.pallas_call(
        functools.partial(_lru_kernel, tc=tc),
        out_shape=jax.ShapeDtypeStruct((b, tp, w), BF16),
        grid=(b, tp // tc),
        in_specs=[pl.BlockSpec((1, tc, w), lambda bi, c: (bi, c, 0)),
                  pl.BlockSpec((1, tc, w), lambda bi, c: (bi, c, 1)),
                  pl.BlockSpec((CONV_W, w), lambda bi, c: (0, 0)), vec(),
                  pl.BlockSpec((nb, LRU_BLOCK, LRU_BLOCK), lambda bi, c: (0, 0, 0)), vec(),
                  pl.BlockSpec((nb, LRU_BLOCK, LRU_BLOCK), lambda bi, c: (0, 0, 0)), vec(), vec()],
        out_specs=pl.BlockSpec((1, tc, w), lambda bi, c: (bi, c, 0)),
        scratch_shapes=[pltpu.VMEM((tc + SUBLANES, w), F32), pltpu.VMEM((1, w), F32)],
        compiler_params=_cparams("parallel", "arbitrary"),
        name="rg_lru",
    )(pl_xy, pl_xy, conv_w, conv_b.reshape(1, w), wa.astype(BF16), ba.reshape(1, w),
      wx.astype(BF16), bx.reshape(1, w), lam.reshape(1, w))


def _rwkv_proj_kernel(p_ref, prev_ref, mu_ref, w0_ref, w2_ref, a0_ref, a2_ref, g2_ref,
                      r_ref, k_ref, v_ref, lw_ref, a_ref, g_ref, *, w):
    i = pl.program_id(1)
    p = p_ref[0]
    prev = prev_ref[0][SUBLANES - 1:SUBLANES, :]
    prev = jnp.where(i == 0, jnp.zeros_like(prev), prev)
    row = _iota(p.shape, 0)
    p_prev = jnp.where(row == 0, prev, pltpu.roll(p, 1, 0))
    pm = p + (p_prev - p) * mu_ref[...]
    r_ref[0] = pm[:, 0:w]
    k_ref[0] = pm[:, w:2 * w]
    v_ref[0] = pm[:, 2 * w:3 * w]
    xwa = pm[:, 3 * w:3 * w + LANES]
    xg = pm[:, 3 * w + LANES:]
    ww = w0_ref[...] + _dot(jnp.tanh(xwa), w2_ref[...], HIGHEST)
    nw = -ww
    softplus = jnp.maximum(nw, 0.0) + jnp.log1p(jnp.exp(-jnp.abs(nw)))
    lw_ref[0] = -jnp.exp(-softplus - 0.5)
    a_ref[0] = jax.nn.sigmoid(a0_ref[...] + _dot(xwa, a2_ref[...], HIGHEST))
    g_ref[0] = _dot(jax.nn.sigmoid(xg), g2_ref[...], HIGHEST)


def rwkv_proj(pp, mu, w0, w2, a0, a2, g2, w, tm=256):
    b, tp, n = pp.shape
    if tp % tm:
        tm = CHUNK
    nl = n - 3 * w - LANES
    mu_p = jnp.zeros((1, n), F32).at[0, :mu.shape[0]].set(mu)
    w2_p = jnp.zeros((LANES, w), F32).at[:DECAY_LORA].set(w2)
    a2_p = jnp.zeros((LANES, w), F32).at[DECAY_LORA:DECAY_LORA + AAA_LORA].set(a2)
    g2_p = jnp.zeros((nl, w), F32).at[:GATE_LORA].set(g2)
    rows8 = tm // SUBLANES
    out = jax.ShapeDtypeStruct((b, tp, w), F32)
    ospec = pl.BlockSpec((1, tm, w), lambda bi, i: (bi, i, 0))
    vec = lambda: pl.BlockSpec((1, w), lambda bi, i: (0, 0))
    return pl.pallas_call(
        functools.partial(_rwkv_proj_kernel, w=w),
        out_shape=[out] * 6,
        grid=(b, tp // tm),
        in_specs=[pl.BlockSpec((1, tm, n), lambda bi, i: (bi, i, 0)),
                  pl.BlockSpec((1, SUBLANES, n), lambda bi, i: (bi, jnp.maximum(i * rows8 - 1, 0), 0)),
                  pl.BlockSpec((1, n), lambda bi, i: (0, 0)),
                  vec(), pl.BlockSpec((LANES, w), lambda bi, i: (0, 0)),
                  vec(), pl.BlockSpec((LANES, w), lambda bi, i: (0, 0)),
                  pl.BlockSpec((nl, w), lambda bi, i: (0, 0))],
        out_specs=[ospec] * 6,
        compiler_params=_cparams("parallel", "arbitrary"),
        name="rwkv_proj",
    )(pp, pp, mu_p, w0.reshape(1, w), w2_p, a0.reshape(1, w), a2_p, g2_p)


def _split_bf16(x):
    hi = x.astype(BF16)
    return hi, (x - hi.astype(F32)).astype(BF16)


def _mm_exact_rhs(a, b):
    ah, al = _split_bf16(a)
    b = b.astype(BF16)
    return _dot(ah, b) + _dot(al, b)


def _bmm(a, b):
    return jnp.einsum("pmk,pkn->pmn", a.astype(BF16), b.astype(BF16), preferred_element_type=F32)


def _bmm_nt(a, b):
    return jnp.einsum("pmk,pnk->pmn", a.astype(BF16), b.astype(BF16), preferred_element_type=F32)


def _bt(x):
    return jnp.stack([x[p].T for p in range(x.shape[0])], axis=0)


def _rwkv_scan_kernel(r_ref, k_ref, v_ref, lw_ref, a_ref, g_ref, kk_ref, ka_ref, rk_ref, lng_ref, lnb_ref,
                      o_ref, s_ref, *, npair):
    n = pl.program_id(2)

    @pl.when(n == 0)
    def _():
        s_ref[...] = jnp.zeros_like(s_ref)

    P = npair
    tb = r_ref.shape[1]
    C = RWKV_CHUNK
    nch = tb // C
    hd = RWKV_DIM

    def seq(ref):
        x = ref[0]
        return jnp.stack([x[:, p * LANES:(p + 1) * LANES] for p in range(P)], axis=0)

    def vec(ref):
        x = ref[...]
        return jnp.stack([x[:, p * LANES:(p + 1) * LANES] for p in range(P)], axis=0)

    lane = _iota((1, 1, LANES), 2)
    hmask = [(lane < hd).astype(F32), (lane >= hd).astype(F32)]
    li = _iota((LANES, LANES), 0)
    lj = _iota((LANES, LANES), 1)
    same_head = ((li >= hd) == (lj >= hd)).astype(F32)
    eye = (li == lj).astype(F32)

    def head_sum(x, scale=1.0):
        return _mm_exact_rhs(x.reshape(P * tb, LANES), same_head * scale).reshape(P, tb, LANES)

    r, k, v, lw, a = seq(r_ref), seq(k_ref), seq(v_ref), seq(lw_ref), seq(a_ref)

    kkr = k * vec(kk_ref)
    kk = kkr * lax.rsqrt(jnp.maximum(head_sum(kkr * kkr), 1e-24))
    kmod = k * (1.0 + (a - 1.0) * vec(ka_ref))
    be = kk * a

    pos = _iota((P * tb, LANES), 0) & (C - 1)
    cw2 = lw.reshape(P * tb, LANES)
    sh = 1
    while sh < C:
        cw2 = cw2 + jnp.where(pos >= sh, pltpu.roll(cw2, sh, 0), 0.0)
        sh *= 2
    cw = cw2.reshape(P, tb, LANES)
    cl = jnp.concatenate(
        [jnp.broadcast_to(cw[:, (c + 1) * C - 1:(c + 1) * C, :], (P, C, LANES)) for c in range(nch)], axis=1)
    e_out = jnp.exp(-cw)
    e_rem = jnp.exp(cl - cw)
    rt = r * jnp.exp(cw)
    at = -kk * jnp.exp(cw - lw)
    kt = kmod * e_out
    bt = be * e_out
    kh = kmod * e_rem
    bh = be * e_rem

    def stack_masked(x):
        return jnp.concatenate(
            [x[:, c * C:(c + 1) * C] * hmask[h] for c in range(nch) for h in range(2)], axis=1)

    def stack_dup(x):
        return jnp.concatenate([x[:, c * C:(c + 1) * C] for c in range(nch) for h in range(2)], axis=1)

    sr = 2 * nch * C
    at_s = stack_masked(at)
    rt_s = stack_masked(rt)
    v_s = stack_masked(v)
    gram = _bmm_nt(jnp.concatenate([at_s, rt_s], axis=1),
                   jnp.concatenate([stack_dup(bt), stack_dup(kt)], axis=1))
    ui = _iota((sr, sr), 0)
    uj = _iota((sr, sr), 1)
    unit = _blk(ui, C) == _blk(uj, C)
    strict = unit & (ui > uj)
    incl = unit & (ui >= uj)
    a_ab = jnp.where(strict, gram[:, :sr, :sr], 0.0)
    a_ak = jnp.where(strict, gram[:, :sr, sr:], 0.0)
    a_rb = jnp.where(incl, gram[:, sr:, :sr], 0.0)
    a_rk = jnp.where(incl, gram[:, sr:, sr:], 0.0)

    x = (ui == uj).astype(F32) + jnp.where(_blk(ui, 2) == _blk(uj, 2), a_ab, 0.0)
    nblk = 2
    while nblk < C:
        e = jnp.where((_blk(ui, 2 * nblk) == _blk(uj, 2 * nblk)) & (_blk(ui, nblk) != _blk(uj, nblk)), a_ab, 0.0)
        x = x + _bmm(_bmm(x, e), x)
        nblk *= 2

    ah = _bmm(x, at_s)
    ul = _bmm(x, _bmm(a_ak, v_s))
    rh = rt_s + _bmm(a_rb, ah)
    yl = _bmm(a_rb, ul) + _bmm(a_rk, v_s)
    bh_s = stack_masked(bh)
    kh_s = stack_masked(kh)

    s = s_ref[...]
    ys = []
    for c in range(nch):
        u0 = slice(2 * c * C, (2 * c + 1) * C)
        u1 = slice((2 * c + 1) * C, (2 * c + 2) * C)
        uc = slice(2 * c * C, (2 * c + 2) * C)
        ys.append(_bmm(rh[:, u0] + rh[:, u1], s) + yl[:, u0] + yl[:, u1])
        wc = jnp.exp(cl[:, c * C:c * C + 1, :])
        bht = _bt(bh_s[:, uc])
        pm = eye * wc + _bmm(bht, ah[:, uc])
        z = _bmm(bht, ul[:, uc]) + _bmm(_bt(kh_s[:, uc]), v_s[:, uc])
        s = _bmm(pm, s) + z
    s_ref[...] = s
    y = jnp.concatenate(ys, axis=1)

    mu = head_sum(y, 1.0 / hd)
    yc = y - mu
    var = head_sum(yc * yc, 1.0 / hd)
    yn = yc * lax.rsqrt(var + RWKV_EPS) * vec(lng_ref) + vec(lnb_ref)
    bonus = head_sum(r * kmod * vec(rk_ref)) * v
    out = (yn + bonus) * seq(g_ref)
    o_ref[0] = jnp.concatenate([out[p] for p in range(P)], axis=-1).astype(o_ref.dtype)


def rwkv_scan(r, k, v, lw, a, g, k_k, k_a, r_k, ln_g, ln_b, npair=4):
    b, tp, w = r.shape
    tb = RWKV_TBLK
    wl = npair * LANES
    seq = pl.BlockSpec((1, tb, wl), lambda bi, j, n: (bi, n, j))
    vec = pl.BlockSpec((1, wl), lambda bi, j, n: (0, j))
    return pl.pallas_call(
        functools.partial(_rwkv_scan_kernel, npair=npair),
        out_shape=jax.ShapeDtypeStruct((b, tp, w), BF16),
        grid=(b, w // wl, tp // tb),
        in_specs=[seq] * 6 + [vec] * 5,
        out_specs=seq,
        scratch_shapes=[pltpu.VMEM((npair, LANES, LANES), F32)],
        compiler_params=_cparams("parallel", "parallel", "arbitrary"),
        name="rwkv_scan",
    )(r, k, v, lw, a, g, k_k.reshape(1, w), k_a.reshape(1, w), r_k.reshape(1, w),
      ln_g.reshape(1, w), ln_b.reshape(1, w))


def _merge_kernel(o1_ref, o2_ref, o3_ref, wb_ref, g1_ref, g2_ref, g3_ref, out_ref):
    acc = g1_ref[...].astype(F32) * _dot(o1_ref[...], wb_ref[0])
    acc = acc + g2_ref[...].astype(F32) * _dot(o2_ref[...], wb_ref[1])
    acc = acc + g3_ref[...].astype(F32) * _dot(o3_ref[...], wb_ref[2])
    out_ref[...] = acc.astype(out_ref.dtype)


def merge(o1, o2, o3, w_branch, gates, tm=ROW_TILE, tn=1024):
    m, w = o1.shape
    d = w_branch.shape[2]
    nj = d // tn
    ospec = pl.BlockSpec((tm, w), lambda i, j: (i, 0))
    gspec = lambda b: pl.BlockSpec((tm, tn), lambda i, j, b=b: (i, b * nj + j))
    return pl.pallas_call(
        _merge_kernel,
        out_shape=jax.ShapeDtypeStruct((m, d), BF16),
        grid=(m // tm, nj),
        in_specs=[ospec, ospec, ospec, pl.BlockSpec((3, w, tn), lambda i, j: (0, 0, j)),
                  gspec(0), gspec(1), gspec(2)],
        out_specs=pl.BlockSpec((tm, tn), lambda i, j: (i, j)),
        compiler_params=_cparams("parallel", "arbitrary"),
        name="merge",
    )(o1, o2, o3, w_branch, gates, gates, gates)


def _router_kernel(h_ref, g_ref, wr_ref, br_ref, m_ref, eid_ref, wt_ref):
    x = h_ref[...]
    m = x * lax.rsqrt(jnp.mean(x * x, axis=-1, keepdims=True) + RMS_EPS) * g_ref[...]
    m_ref[...] = m.astype(m_ref.dtype)
    logits = _dot(m, wr_ref[...], HIGHEST) + br_ref[...]
    lane = _iota(logits.shape, 1)
    lanef = lane.astype(F32)
    neg = -jnp.inf
    big = float(LANES)
    glog = jnp.where(lane < N_GROUPS, logits, neg)
    gmax = jnp.max(glog, axis=-1, keepdims=True)
    gsel = jnp.min(jnp.where(glog == gmax, lanef, big), axis=-1, keepdims=True)
    pg = 1.0 / jnp.sum(jnp.exp(glog - gmax), axis=-1, keepdims=True)
    lo = N_GROUPS + gsel * EXPERTS_PER_GROUP
    el = jnp.where((lanef >= lo) & (lanef < lo + EXPERTS_PER_GROUP), logits, neg)
    v1 = jnp.max(el, axis=-1, keepdims=True)
    i1 = jnp.min(jnp.where(el == v1, lanef, big), axis=-1, keepdims=True)
    el2 = jnp.where(lanef == i1, neg, el)
    v2 = jnp.max(el2, axis=-1, keepdims=True)
    i2 = jnp.min(jnp.where(el2 == v2, lanef, big), axis=-1, keepdims=True)
    e2 = jnp.exp(v2 - v1)
    w1 = pg / (1.0 + e2)
    w2 = pg * e2 / (1.0 + e2)
    eid = jnp.where(lane == 0, i1 - N_GROUPS, jnp.where(lane == 1, i2 - N_GROUPS, 0.0))
    eid_ref[...] = eid.astype(jnp.int32)
    wt_ref[...] = jnp.where(lane == 0, w1, jnp.where(lane == 1, w2, 0.0))


def router(h, g, wg, bg, we, be, tm=POINT_TILE):
    m, d = h.shape
    nr = N_GROUPS + N_EXPERTS
    wr = jnp.zeros((d, LANES), F32).at[:, :N_GROUPS].set(wg).at[:, N_GROUPS:nr].set(we)
    br = jnp.zeros((1, LANES), F32).at[0, :N_GROUPS].set(bg).at[0, N_GROUPS:nr].set(be)
    row = pl.BlockSpec((tm, d), lambda i: (i, 0))
    slab = pl.BlockSpec((tm, LANES), lambda i: (i, 0))
    return pl.pallas_call(
        _router_kernel,
        out_shape=[jax.ShapeDtypeStruct((m, d), BF16), jax.ShapeDtypeStruct((m, LANES), jnp.int32),
                   jax.ShapeDtypeStruct((m, LANES), F32)],
        grid=(m // tm,),
        in_specs=[row, pl.BlockSpec((1, d), lambda i: (0, 0)), pl.BlockSpec((d, LANES), lambda i: (0, 0)),
                  pl.BlockSpec((1, LANES), lambda i: (0, 0))],
        out_specs=[row, slab, slab],
        compiler_params=_cparams("parallel"),
        name="moe_router",
    )(h, g.reshape(1, d), wr, br)


def _expert_kernel(te_ref, first_ref, nu_ref, x_ref, wg_ref, wu_ref, wd_ref, y_ref, wgu_s, wd_s, *, nk, dk):
    i = pl.program_id(0)
    kc = pl.program_id(1)
    used = i < nu_ref[0]
    de = nk * dk

    @pl.when(used & (first_ref[i] == 1))
    def _():
        for c in range(nk):
            @pl.when(kc == c)
            def _():
                wgu_s[:, c * dk:(c + 1) * dk] = wg_ref[0, 0].astype(BF16)
                wgu_s[:, de + c * dk:de + (c + 1) * dk] = wu_ref[0, 0].astype(BF16)
                wd_s[c * dk:(c + 1) * dk, :] = wd_ref[0, 0].astype(BF16)

    @pl.when(used & (kc == nk - 1))
    def _():
        gu = _dot(x_ref[...], wgu_s[...])
        gt = gu[:, :de]
        act = (gt * jax.nn.sigmoid(gt) * gu[:, de:]).astype(BF16)
        y_ref[...] = _dot(act, wd_s[...]).astype(y_ref.dtype)

    @pl.when(jnp.logical_not(used) & (kc == nk - 1))
    def _():
        y_ref[...] = jnp.zeros_like(y_ref)


def expert_ffn(xs, tile_expert, tile_first, n_used, w_gate, w_up, w_down, layer, tm=MOE_TILE, nk=4):
    p, d = xs.shape
    de = w_down.shape[2]
    dk = de // nk
    assert dk * nk == de and dk % LANES == 0

    def chunk(i, kc, first):
        return jnp.where(first[i] == 1, kc, nk - 1)

    grid_spec = pltpu.PrefetchScalarGridSpec(
        num_scalar_prefetch=3,
        grid=(p // tm, nk),
        in_specs=[pl.BlockSpec((tm, d), lambda i, kc, te, first, nu: (i, 0)),
                  pl.BlockSpec((1, 1, d, dk), lambda i, kc, te, first, nu: (layer, te[i], 0, chunk(i, kc, first))),
                  pl.BlockSpec((1, 1, d, dk), lambda i, kc, te, first, nu: (layer, te[i], 0, chunk(i, kc, first))),
                  pl.BlockSpec((1, 1, dk, d), lambda i, kc, te, first, nu: (layer, te[i], chunk(i, kc, first), 0))],
        out_specs=pl.BlockSpec((tm, d), lambda i, kc, te, first, nu: (i, 0)),
        scratch_shapes=[pltpu.VMEM((d, 2 * de), BF16), pltpu.VMEM((de, d), BF16)],
    )
    return pl.pallas_call(
        functools.partial(_expert_kernel, nk=nk, dk=dk),
        out_shape=jax.ShapeDtypeStruct((p, d), BF16),
        grid_spec=grid_spec,
        compiler_params=_cparams("arbitrary", "arbitrary"),
        name="moe_experts",
    )(tile_expert, tile_first, n_used, xs, w_gate, w_up, w_down)


def _combine_kernel(h_ref, y0_ref, y1_ref, wt_ref, o_ref):
    wt = wt_ref[...]
    o_ref[...] = (h_ref[...] + wt[:, 0:1] * y0_ref[...].astype(F32) + wt[:, 1:2] * y1_ref[...].astype(F32))


def combine(h, yp, wt, tm=POINT_TILE):
    m, d = h.shape
    nt = m // tm
    return pl.pallas_call(
        _combine_kernel,
        out_shape=jax.ShapeDtypeStruct((m, d), F32),
        grid=(nt,),
        in_specs=[pl.BlockSpec((tm, d), lambda i: (i, 0)), pl.BlockSpec((tm, d), lambda i: (i, 0)),
                  pl.BlockSpec((tm, d), lambda i: (i + nt, 0)), pl.BlockSpec((tm, LANES), lambda i: (i, 0))],
        out_specs=pl.BlockSpec((tm, d), lambda i: (i, 0)),
        compiler_params=_cparams("parallel"),
        name="moe_combine",
    )(h, yp, yp, wt)


def hier_moe(h, norm_g, wg, bg, we, be, w_gate, w_up, w_down, layer, tm=MOE_TILE):
    m, d = h.shape
    xm, eid_slab, wt = router(h, norm_g, wg, bg, we, be)
    npair = m * TOP_K
    eid = eid_slab[:, :TOP_K].T.reshape(npair)
    onehot = (eid[:, None] == jnp.arange(N_EXPERTS, dtype=jnp.int32)[None, :]).astype(jnp.int32)
    counts = jnp.sum(onehot, axis=0)
    rank = jnp.sum((jnp.cumsum(onehot, axis=0) - 1) * onehot, axis=1)
    padded = ((counts + tm - 1) // tm) * tm
    ends = jnp.cumsum(padded)
    starts = ends - padded
    dest = jnp.sum(onehot * starts[None, :], axis=1) + rank
    p_rows = ((npair + tm - 1) // tm + N_EXPERTS) * tm
    pair_token = jnp.arange(npair, dtype=jnp.int32) % m
    src = jnp.zeros((p_rows,), jnp.int32).at[dest].set(pair_token, mode="promise_in_bounds",
                                                         unique_indices=True)
    n_tiles = p_rows // tm
    tile_start = jnp.arange(n_tiles, dtype=jnp.int32) * tm
    n_used = (ends[-1] // tm).astype(jnp.int32).reshape(1)
    last_used = jnp.maximum(n_used[0] - 1, 0) * tm
    tile_expert = jnp.sum((ends[None, :] <= jnp.minimum(tile_start, last_used)[:, None]).astype(jnp.int32), axis=1)
    tile_expert = jnp.minimum(tile_expert, N_EXPERTS - 1)
    tile_first = jnp.concatenate([jnp.ones((1,), jnp.int32),
                                  (tile_expert[1:] != tile_expert[:-1]).astype(jnp.int32)])
    xs = xm.at[src].get(mode="promise_in_bounds")
    ys = expert_ffn(xs, tile_expert, tile_first, n_used, w_gate, w_up, w_down, layer, tm)
    yp = ys.at[dest].get(mode="promise_in_bounds")
    return combine(h, yp, wt)


def kernel(x, meta_tokens, final_norm_g, norm1_g, w_in, ret_gn_g, lru_conv_w, lru_conv_b, lru_wa, lru_ba,
           lru_wx, lru_bx, lru_lambda, rwkv_mu, rwkv_w0, rwkv_w2, rwkv_a0, rwkv_a2, rwkv_g2, rwkv_k_k,
           rwkv_k_a, rwkv_r_k, rwkv_ln_g, rwkv_ln_b, w_branch, w_out, norm2_g, moe_wg, moe_bg, moe_we,
           moe_be, moe_w_gate, moe_w_up, moe_w_down):
    b, seq, d = x.shape
    depth = w_in.shape[0]
    w = ret_gn_g.shape[1]
    t = N_META + seq
    tp = ((t + CHUNK - 1) // CHUNK) * CHUNK
    m = b * tp
    rwkv_cols = rwkv_mu.shape[1]
    rwkv_pad = 3 * w + LANES + 2 * LANES
    c_ret, c_lru, c_rwkv = 4 * w, 6 * w, 6 * w + rwkv_cols

    meta = jnp.broadcast_to(meta_tokens.astype(x.dtype)[None], (b, N_META, d))
    h = jnp.concatenate([meta, x, jnp.zeros((b, tp - t, d), x.dtype)], axis=1).reshape(m, d)

    for l in range(depth):
        wl = w_in[l]
        w_ret = wl[:, :c_ret].astype(BF16)
        w_lru = wl[:, c_ret:c_lru].astype(BF16)
        w_rwkv = jnp.pad(wl[:, c_lru:c_rwkv], ((0, 0), (0, rwkv_pad - rwkv_cols))).astype(BF16)
        w_gates = wl[:, c_rwkv:].astype(BF16)

        a = rmsnorm(h, norm1_g[l], BF16)
        p_ret = matmul(a, w_ret, BF16, tm=ROW_TILE, tn=1024, name="proj_ret")
        p_lru = matmul(a, w_lru, F32, tm=ROW_TILE, tn=1024, name="proj_lru")
        p_rwkv = matmul(a, w_rwkv, F32, tm=ROW_TILE, tn=rwkv_pad // 3, name="proj_rwkv")
        gates = matmul(a, w_gates, BF16, tm=ROW_TILE, tn=1024, act="sigmoid", name="proj_gates")

        o_ret = retention(p_ret.reshape(b, tp, 4 * w), ret_gn_g[l])
        o_lru = rg_lru(p_lru.reshape(b, tp, 2 * w), lru_conv_w[l], lru_conv_b[l], lru_wa[l], lru_ba[l],
                       lru_wx[l], lru_bx[l], lru_lambda[l])
        r_, k_, v_, lw_, a_, g_ = rwkv_proj(p_rwkv.reshape(b, tp, rwkv_pad), rwkv_mu[l], rwkv_w0[l],
                                            rwkv_w2[l], rwkv_a0[l], rwkv_a2[l], rwkv_g2[l], w)
        o_rwkv = rwkv_scan(r_, k_, v_, lw_, a_, g_, rwkv_k_k[l], rwkv_k_a[l], rwkv_r_k[l],
                           rwkv_ln_g[l], rwkv_ln_b[l])

        merged = merge(o_ret.reshape(m, w), o_lru.reshape(m, w), o_rwkv.reshape(m, w),
                       w_branch[l].astype(BF16), gates)
        h = matmul(merged, w_out[l].astype(BF16), F32, tm=ROW_TILE, tn=1024, residual=h, name="proj_out")

        h = hier_moe(h, norm2_g[l], moe_wg[l], moe_bg[l], moe_we[l], moe_be[l], moe_w_gate, moe_w_up,
                     moe_w_down, l)

    out = rmsnorm(h, final_norm_g, F32).reshape(b, tp, d)
    return out[:, N_META:t]
```

```python
import functools
import math

import jax
import jax.numpy as jnp
from jax import lax
from jax.experimental import pallas as pl
from jax.experimental.pallas import tpu as pltpu

F32 = jnp.float32
BF16 = jnp.bfloat16
HIGHEST = lax.Precision.HIGHEST

LANES = 128
SUBLANES = 8
VMEM_BYTES_V7X = 64 * 1024 * 1024
VMEM_LIMIT = VMEM_BYTES_V7X - 8 * 1024 * 1024

N_META = 16
CHUNK = 128
RET_DIM = 128
ROPE_BASE = 10000.0
RET_EPS = 1e-5
LRU_BLOCK = 128
CONV_W = 4
LRU_C = 8.0
RWKV_DIM = 64
DECAY_LORA = 64
AAA_LORA = 64
GATE_LORA = 160
RWKV_EPS = 64e-5
N_GROUPS = 4
EXPERTS_PER_GROUP = 8
N_EXPERTS = N_GROUPS * EXPERTS_PER_GROUP
TOP_K = 2
RMS_EPS = 1e-6

RWKV_CHUNK = 64
RWKV_TBLK = 128
ROW_TILE = 512
POINT_TILE = 256
MOE_TILE = 512


def _cparams(*sem):
    return pltpu.CompilerParams(dimension_semantics=sem, vmem_limit_bytes=VMEM_LIMIT)


def _dot(a, b, precision=None):
    return jnp.dot(a, b, preferred_element_type=F32, precision=precision)


def _dot_nt(a, b, precision=None):
    return lax.dot_general(a, b, (((1,), (1,)), ((), ())), preferred_element_type=F32, precision=precision)


def _iota(shape, dim):
    return lax.broadcasted_iota(jnp.int32, shape, dim)


def _blk(i, n):
    shift = n.bit_length() - 1
    assert 1 << shift == n
    return lax.shift_right_logical(i, jnp.int32(shift))


def _rmsnorm_kernel(x_ref, g_ref, o_ref):
    x = x_ref[...]
    y = x * lax.rsqrt(jnp.mean(x * x, axis=-1, keepdims=True) + RMS_EPS)
    o_ref[...] = (y * g_ref[...]).astype(o_ref.dtype)


def rmsnorm(x, g, out_dtype, tm=POINT_TILE):
    m, d = x.shape
    return pl.pallas_call(
        _rmsnorm_kernel,
        out_shape=jax.ShapeDtypeStruct((m, d), out_dtype),
        grid=(m // tm,),
        in_specs=[pl.BlockSpec((tm, d), lambda i: (i, 0)), pl.BlockSpec((1, d), lambda i: (0, 0))],
        out_specs=pl.BlockSpec((tm, d), lambda i: (i, 0)),
        compiler_params=_cparams("parallel"),
        name="rmsnorm",
    )(x, g.reshape(1, d))


def _matmul_kernel(a_ref, w_ref, o_ref, *, act):
    acc = _dot(a_ref[...], w_ref[...])
    if act == "sigmoid":
        acc = jax.nn.sigmoid(acc)
    o_ref[...] = acc.astype(o_ref.dtype)


def _matmul_res_kernel(a_ref, w_ref, r_ref, o_ref):
    o_ref[...] = r_ref[...] + _dot(a_ref[...], w_ref[...])


def matmul(a, w, out_dtype, *, tm, tn, act=None, residual=None, name="matmul"):
    m, k = a.shape
    n = w.shape[1]
    assert m % tm == 0 and n % tn == 0, (m, n, tm, tn)
    in_specs = [pl.BlockSpec((tm, k), lambda i, j: (i, 0)), pl.BlockSpec((k, tn), lambda i, j: (0, j))]
    args = [a, w]
    if residual is None:
        body = functools.partial(_matmul_kernel, act=act)
    else:
        body = _matmul_res_kernel
        in_specs.append(pl.BlockSpec((tm, tn), lambda i, j: (i, j)))
        args.append(residual)
    return pl.pallas_call(
        body,
        out_shape=jax.ShapeDtypeStruct((m, n), out_dtype),
        grid=(m // tm, n // tn),
        in_specs=in_specs,
        out_specs=pl.BlockSpec((tm, tn), lambda i, j: (i, j)),
        compiler_params=_cparams("parallel", "arbitrary"),
        name=name,
    )(*args)


def _retention_kernel(q_ref, k_ref, v_ref, g_ref, cos_ref, sin_ref, gn_ref, o_ref, state_ref, *, heads):
    c = pl.program_id(1)

    @pl.when(c == 0)
    def _():
        state_ref[...] = jnp.zeros_like(state_ref)

    d = RET_DIM
    cos = cos_ref[...]
    sin = sin_ref[...]
    row = _iota((CHUNK, CHUNK), 0)
    col = _iota((CHUNK, CHUNK), 1)
    diff = (row - col).astype(F32)
    causal = row >= col
    idx = _iota((CHUNK, 1), 0).astype(F32)
    for h in range(heads):
        lg = math.log1p(-(2.0 ** (-5.0 - h)))
        sl = slice(h * d, (h + 1) * d)
        q = q_ref[0, :, sl].astype(F32)
        k = k_ref[0, :, sl].astype(F32)
        v = v_ref[0, :, sl].astype(BF16)
        g = g_ref[0, :, sl].astype(F32)
        qr = q * cos + pltpu.roll(q, d // 2, 1) * sin
        kr = (k * cos + pltpu.roll(k, d // 2, 1) * sin) * (d ** -0.5)
        decay = jnp.where(causal, jnp.exp(lg * jnp.maximum(diff, 0.0)), 0.0)
        scores = _dot_nt(qr.astype(BF16), kr.astype(BF16)) * decay
        o = _dot(scores.astype(BF16), v)
        xi = jnp.exp(lg * (idx + 1.0))
        st = state_ref[h]
        o = o + _dot((qr * xi).astype(BF16), st.astype(BF16))
        zeta = jnp.exp(lg * (CHUNK - 1.0 - idx))
        kv = _dot((kr * zeta).T.astype(BF16), v)
        state_ref[h] = math.exp(lg * CHUNK) * st + kv
        mu = jnp.mean(o, axis=-1, keepdims=True)
        oc = o - mu
        var = jnp.mean(oc * oc, axis=-1, keepdims=True)
        on = oc * lax.rsqrt(var + RET_EPS) * gn_ref[:, sl]
        o_ref[0, :, sl] = (g * jax.nn.sigmoid(g) * on).astype(o_ref.dtype)


def _rope_tables(tp):
    half = RET_DIM // 2
    inv_freq = ROPE_BASE ** (-jnp.arange(half, dtype=F32) / half)
    ang = jnp.arange(tp, dtype=F32)[:, None] * inv_freq[None, :]
    cos, sin = jnp.cos(ang), jnp.sin(ang)
    return jnp.concatenate([cos, cos], axis=-1), jnp.concatenate([-sin, sin], axis=-1)


def retention(pr, gn_g):
    b, tp, w4 = pr.shape
    w = w4 // 4
    heads = w // RET_DIM
    cos, sin = _rope_tables(tp)
    spec = lambda j: pl.BlockSpec((1, CHUNK, w), lambda bi, c, j=j: (bi, c, j))
    return pl.pallas_call(
        functools.partial(_retention_kernel, heads=heads),
        out_shape=jax.ShapeDtypeStruct((b, tp, w), BF16),
        grid=(b, tp // CHUNK),
        in_specs=[spec(0), spec(1), spec(2), spec(3),
                  pl.BlockSpec((CHUNK, RET_DIM), lambda bi, c: (c, 0)),
                  pl.BlockSpec((CHUNK, RET_DIM), lambda bi, c: (c, 0)),
                  pl.BlockSpec((1, w), lambda bi, c: (0, 0))],
        out_specs=pl.BlockSpec((1, CHUNK, w), lambda bi, c: (bi, c, 0)),
        scratch_shapes=[pltpu.VMEM((heads, RET_DIM, RET_DIM), F32)],
        compiler_params=_cparams("parallel", "arbitrary"),
        name="retention",
    )(pr, pr, pr, pr, cos, sin, gn_g.reshape(1, w))


def _lru_kernel(x_ref, y_ref, cw_ref, cb_ref, wa_ref, ba_ref, wx_ref, bx_ref, lam_ref, o_ref,
                xs_ref, h_ref, *, tc):
    c = pl.program_id(1)
    w = x_ref.shape[-1]

    @pl.when(c == 0)
    def _():
        xs_ref[0:SUBLANES, :] = jnp.zeros((SUBLANES, w), F32)
        h_ref[...] = jnp.zeros_like(h_ref)

    x = x_ref[0].astype(F32)
    xs_ref[SUBLANES:, :] = x
    xc = cb_ref[...] + cw_ref[CONV_W - 1:CONV_W, :] * x
    for j in range(CONV_W - 1):
        off = SUBLANES - (CONV_W - 1) + j
        xc = xc + cw_ref[j:j + 1, :] * xs_ref[off:off + tc, :]
    xs_ref[0:SUBLANES, :] = x[tc - SUBLANES:, :]

    xcb = xc.astype(BF16)
    nb = w // LRU_BLOCK
    ra = jnp.concatenate(
        [_dot(xcb[:, n * LRU_BLOCK:(n + 1) * LRU_BLOCK], wa_ref[n]) for n in range(nb)], axis=-1)
    ri = jnp.concatenate(
        [_dot(xcb[:, n * LRU_BLOCK:(n + 1) * LRU_BLOCK], wx_ref[n]) for n in range(nb)], axis=-1)
    r = jax.nn.sigmoid(ra + ba_ref[...])
    i = jax.nn.sigmoid(ri + bx_ref[...])
    nlam = -lam_ref[...]
    softplus = jnp.maximum(nlam, 0.0) + jnp.log1p(jnp.exp(-jnp.abs(nlam)))
    log_a = -LRU_C * r * softplus
    a = jnp.exp(log_a)
    bb = jnp.sqrt(1.0 - jnp.exp(2.0 * log_a)) * (i * xc)

    row = _iota((tc, w), 0)
    s = 1
    while s < tc:
        a_sh = pltpu.roll(a, s, 0)
        b_sh = pltpu.roll(bb, s, 0)
        m = row >= s
        bb = jnp.where(m, a * b_sh + bb, bb)
        a = jnp.where(m, a * a_sh, a)
        s *= 2
    h = a * h_ref[...] + bb
    h_ref[...] = h[tc - 1:tc, :]
    y = y_ref[0].astype(F32)
    o_ref[0] = (h * jax.nn.gelu(y)).astype(o_ref.dtype)


def rg_lru(pl_xy, conv_w, conv_b, wa, ba, wx, bx, lam, tc=256):
    b, tp, w2 = pl_xy.shape
    w = w2 // 2
    if tp % tc:
        tc = CHUNK
    nb = w // LRU_BLOCK
    vec = lambda: pl.BlockSpec((1, w), lambda bi, c: (0, 0))
    return pl.pallas_call(
        functools.partial(_lru_kernel, tc=tc),
        out_shape=jax.ShapeDtypeStruct((b, tp, w), BF16),
        grid=(b, tp // tc),
        in_specs=[pl.BlockSpec((1, tc, w), lambda bi, c: (bi, c, 0)),
                  pl.BlockSpec((1, tc, w), lambda bi, c: (bi, c, 1)),
                  pl.BlockSpec((CONV_W, w), lambda bi, c: (0, 0)), vec(),
                  pl.BlockSpec((nb, LRU_BLOCK, LRU_BLOCK), lambda bi, c: (0, 0, 0)), vec(),
                  pl.BlockSpec((nb, LRU_BLOCK, LRU_BLOCK), lambda bi, c: (0, 0, 0)), vec(), vec()],
        out_specs=pl.BlockSpec((1, tc, w), lambda bi, c: (bi, c, 0)),
        scratch_shapes=[pltpu.VMEM((tc + SUBLANES, w), F32), pltpu.VMEM((1, w), F32)],
        compiler_params=_cparams("parallel", "arbitrary"),
        name="rg_lru",
    )(pl_xy, pl_xy, conv_w, conv_b.reshape(1, w), wa.astype(BF16), ba.reshape(1, w),
      wx.astype(BF16), bx.reshape(1, w), lam.reshape(1, w))


def _rwkv_proj_kernel(p_ref, prev_ref, mu_ref, w0_ref, w2_ref, a0_ref, a2_ref, g2_ref,
                      r_ref, k_ref, v_ref, lw_ref, a_ref, g_ref, *, w):
    i = pl.program_id(1)
    p = p_ref[0]
    prev = prev_ref[0][SUBLANES - 1:SUBLANES, :]
    prev = jnp.where(i == 0, jnp.zeros_like(prev), prev)
    row = _iota(p.shape, 0)
    p_prev = jnp.where(row == 0, prev, pltpu.roll(p, 1, 0))
    pm = p + (p_prev - p) * mu_ref[...]
    r_ref[0] = pm[:, 0:w]
    k_ref[0] = pm[:, w:2 * w]
    v_ref[0] = pm[:, 2 * w:3 * w]
    xwa = pm[:, 3 * w:3 * w + LANES]
    xg = pm[:, 3 * w + LANES:]
    ww = w0_ref[...] + _dot(jnp.tanh(xwa), w2_ref[...], HIGHEST)
    nw = -ww
    softplus = jnp.maximum(nw, 0.0) + jnp.log1p(jnp.exp(-jnp.abs(nw)))
    lw_ref[0] = -jnp.exp(-softplus - 0.5)
    a_ref[0] = jax.nn.sigmoid(a0_ref[...] + _dot(xwa, a2_ref[...], HIGHEST))
    g_ref[0] = _dot(jax.nn.sigmoid(xg), g2_ref[...], HIGHEST)


def rwkv_proj(pp, mu, w0, w2, a0, a2, g2, w, tm=256):
    b, tp, n = pp.shape
    if tp % tm:
        tm = CHUNK
    nl = n - 3 * w - LANES
    mu_p = jnp.zeros((1, n), F32).at[0, :mu.shape[0]].set(mu)
    w2_p = jnp.zeros((LANES, w), F32).at[:DECAY_LORA].set(w2)
    a2_p = jnp.zeros((LANES, w), F32).at[DECAY_LORA:DECAY_LORA + AAA_LORA].set(a2)
    g2_p = jnp.zeros((nl, w), F32).at[:GATE_LORA].set(g2)
    rows8 = tm // SUBLANES
    out = jax.ShapeDtypeStruct((b, tp, w), F32)
    ospec = pl.BlockSpec((1, tm, w), lambda bi, i: (bi, i, 0))
    vec = lambda: pl.BlockSpec((1, w), lambda bi, i: (0, 0))
    return pl.pallas_call(
        functools.partial(_rwkv_proj_kernel, w=w),
        out_shape=[out] * 6,
        grid=(b, tp // tm),
        in_specs=[pl.BlockSpec((1, tm, n), lambda bi, i: (bi, i, 0)),
                  pl.BlockSpec((1, SUBLANES, n), lambda bi, i: (bi, jnp.maximum(i * rows8 - 1, 0), 0)),
                  pl.BlockSpec((1, n), lambda bi, i: (0, 0)),
                  vec(), pl.BlockSpec((LANES, w), lambda bi, i: (0, 0)),
                  vec(), pl.BlockSpec((LANES, w), lambda bi, i: (0, 0)),
                  pl.BlockSpec((nl, w), lambda bi, i: (0, 0))],
        out_specs=[ospec] * 6,
        compiler_params=_cparams("parallel", "arbitrary"),
        name="rwkv_proj",
    )(pp, pp, mu_p, w0.reshape(1, w), w2_p, a0.reshape(1, w), a2_p, g2_p)


def _split_bf16(x):
    hi = x.astype(BF16)
    return hi, (x - hi.astype(F32)).astype(BF16)


def _mm_exact_rhs(a, b):
    ah, al = _split_bf16(a)
    b = b.astype(BF16)
    return _dot(ah, b) + _dot(al, b)


def _bmm(a, b):
    return jnp.einsum("pmk,pkn->pmn", a.astype(BF16), b.astype(BF16), preferred_element_type=F32)


def _bmm_nt(a, b):
    return jnp.einsum("pmk,pnk->pmn", a.astype(BF16), b.astype(BF16), preferred_element_type=F32)


def _bt(x):
    return jnp.stack([x[p].T for p in range(x.shape[0])], axis=0)


def _rwkv_scan_kernel(r_ref, k_ref, v_ref, lw_ref, a_ref, g_ref, kk_ref, ka_ref, rk_ref, lng_ref, lnb_ref,
                      o_ref, s_ref, *, npair):
    n = pl.program_id(2)

    @pl.when(n == 0)
    def _():
        s_ref[...] = jnp.zeros_like(s_ref)

    P = npair
    tb = r_ref.shape[1]
    C = RWKV_CHUNK
    nch = tb // C
    hd = RWKV_DIM

    def seq(ref):
        x = ref[0]
        return jnp.stack([x[:, p * LANES:(p + 1) * LANES] for p in range(P)], axis=0)

    def vec(ref):
        x = ref[...]
        return jnp.stack([x[:, p * LANES:(p + 1) * LANES] for p in range(P)], axis=0)

    lane = _iota((1, 1, LANES), 2)
    hmask = [(lane < hd).astype(F32), (lane >= hd).astype(F32)]
    li = _iota((LANES, LANES), 0)
    lj = _iota((LANES, LANES), 1)
    same_head = ((li >= hd) == (lj >= hd)).astype(F32)
    eye = (li == lj).astype(F32)

    def head_sum(x, scale=1.0):
        return _mm_exact_rhs(x.reshape(P * tb, LANES), same_head * scale).reshape(P, tb, LANES)

    r, k, v, lw, a = seq(r_ref), seq(k_ref), seq(v_ref), seq(lw_ref), seq(a_ref)

    kkr = k * vec(kk_ref)
    kk = kkr * lax.rsqrt(jnp.maximum(head_sum(kkr * kkr), 1e-24))
    kmod = k * (1.0 + (a - 1.0) * vec(ka_ref))
    be = kk * a

    pos = _iota((P * tb, LANES), 0) & (C - 1)
    cw2 = lw.reshape(P * tb, LANES)
    sh = 1
    while sh < C:
        cw2 = cw2 + jnp.where(pos >= sh, pltpu.roll(cw2, sh, 0), 0.0)
        sh *= 2
    cw = cw2.reshape(P, tb, LANES)
    cl = jnp.concatenate(
        [jnp.broadcast_to(cw[:, (c + 1) * C - 1:(c + 1) * C, :], (P, C, LANES)) for c in range(nch)], axis=1)
    e_out = jnp.exp(-cw)
    e_rem = jnp.exp(cl - cw)
    rt = r * jnp.exp(cw)
    at = -kk * jnp.exp(cw - lw)
    kt = kmod * e_out
    bt = be * e_out
    kh = kmod * e_rem
    bh = be * e_rem

    def stack_masked(x):
        return jnp.concatenate(
            [x[:, c * C:(c + 1) * C] * hmask[h] for c in range(nch) for h in range(2)], axis=1)

    def stack_dup(x):
        return jnp.concatenate([x[:, c * C:(c + 1) * C] for c in range(nch) for h in range(2)], axis=1)

    sr = 2 * nch * C
    at_s = stack_masked(at)
    rt_s = stack_masked(rt)
    v_s = stack_masked(v)
    gram = _bmm_nt(jnp.concatenate([at_s, rt_s], axis=1),
                   jnp.concatenate([stack_dup(bt), stack_dup(kt)], axis=1))
    ui = _iota((sr, sr), 0)
    uj = _iota((sr, sr), 1)
    unit = _blk(ui, C) == _blk(uj, C)
    strict = unit & (ui > uj)
    incl = unit & (ui >= uj)
    a_ab = jnp.where(strict, gram[:, :sr, :sr], 0.0)
    a_ak = jnp.where(strict, gram[:, :sr, sr:], 0.0)
    a_rb = jnp.where(incl, gram[:, sr:, :sr], 0.0)
    a_rk = jnp.where(incl, gram[:, sr:, sr:], 0.0)

    x = (ui == uj).astype(F32) + jnp.where(_blk(ui, 2) == _blk(uj, 2), a_ab, 0.0)
    nblk = 2
    while nblk < C:
        e = jnp.where((_blk(ui, 2 * nblk) == _blk(uj, 2 * nblk)) & (_blk(ui, nblk) != _blk(uj, nblk)), a_ab, 0.0)
        x = x + _bmm(_bmm(x, e), x)
        nblk *= 2

    ah = _bmm(x, at_s)
    ul = _bmm(x, _bmm(a_ak, v_s))
    rh = rt_s + _bmm(a_rb, ah)
    yl = _bmm(a_rb, ul) + _bmm(a_rk, v_s)
    bh_s = stack_masked(bh)
    kh_s = stack_masked(kh)

    s = s_ref[...]
    ys = []
    for c in range(nch):
        u0 = slice(2 * c * C, (2 * c + 1) * C)
        u1 = slice((2 * c + 1) * C, (2 * c + 2) * C)
        uc = slice(2 * c * C, (2 * c + 2) * C)
        ys.append(_bmm(rh[:, u0] + rh[:, u1], s) + yl[:, u0] + yl[:, u1])
        wc = jnp.exp(cl[:, c * C:c * C + 1, :])
        bht = _bt(bh_s[:, uc])
        pm = eye * wc + _bmm(bht, ah[:, uc])
        z = _bmm(bht, ul[:, uc]) + _bmm(_bt(kh_s[:, uc]), v_s[:, uc])
        s = _bmm(pm, s) + z
    s_ref[...] = s
    y = jnp.concatenate(ys, axis=1)

    mu = head_sum(y, 1.0 / hd)
    yc = y - mu
    var = head_sum(yc * yc, 1.0 / hd)
    yn = yc * lax.rsqrt(var + RWKV_EPS) * vec(lng_ref) + vec(lnb_ref)
    bonus = head_sum(r * kmod * vec(rk_ref)) * v
    out = (yn + bonus) * seq(g_ref)
    o_ref[0] = jnp.concatenate([out[p] for p in range(P)], axis=-1).astype(o_ref.dtype)


def rwkv_scan(r, k, v, lw, a, g, k_k, k_a, r_k, ln_g, ln_b, npair=4):
    b, tp, w = r.shape
    tb = RWKV_TBLK
    wl = npair * LANES
    seq = pl.BlockSpec((1, tb, wl), lambda bi, j, n: (bi, n, j))
    vec = pl.BlockSpec((1, wl), lambda bi, j, n: (0, j))
    return pl.pallas_call(
        functools.partial(_rwkv_scan_kernel, npair=npair),
        out_shape=jax.ShapeDtypeStruct((b, tp, w), BF16),
        grid=(b, w // wl, tp // tb),
        in_specs=[seq] * 6 + [vec] * 5,
        out_specs=seq,
        scratch_shapes=[pltpu.VMEM((npair, LANES, LANES), F32)],
        compiler_params=_cparams("parallel", "parallel", "arbitrary"),
        name="rwkv_scan",
    )(r, k, v, lw, a, g, k_k.reshape(1, w), k_a.reshape(1, w), r_k.reshape(1, w),
      ln_g.reshape(1, w), ln_b.reshape(1, w))


def _merge_kernel(o1_ref, o2_ref, o3_ref, wb_ref, g1_ref, g2_ref, g3_ref, out_ref):
    acc = g1_ref[...].astype(F32) * _dot(o1_ref[...], wb_ref[0])
    acc = acc + g2_ref[...].astype(F32) * _dot(o2_ref[...], wb_ref[1])
    acc = acc + g3_ref[...].astype(F32) * _dot(o3_ref[...], wb_ref[2])
    out_ref[...] = acc.astype(out_ref.dtype)


def merge(o1, o2, o3, w_branch, gates, tm=ROW_TILE, tn=1024):
    m, w = o1.shape
    d = w_branch.shape[2]
    nj = d // tn
    ospec = pl.BlockSpec((tm, w), lambda i, j: (i, 0))
    gspec = lambda b: pl.BlockSpec((tm, tn), lambda i, j, b=b: (i, b * nj + j))
    return pl.pallas_call(
        _merge_kernel,
        out_shape=jax.ShapeDtypeStruct((m, d), BF16),
        grid=(m // tm, nj),
        in_specs=[ospec, ospec, ospec, pl.BlockSpec((3, w, tn), lambda i, j: (0, 0, j)),
                  gspec(0), gspec(1), gspec(2)],
        out_specs=pl.BlockSpec((tm, tn), lambda i, j: (i, j)),
        compiler_params=_cparams("parallel", "arbitrary"),
        name="merge",
    )(o1, o2, o3, w_branch, gates, gates, gates)


def _pack_bf16_pair(hi, lo):
    uh = lax.bitcast_convert_type(hi.astype(BF16).astype(F32), jnp.uint32)
    ul = lax.bitcast_convert_type(lo.astype(BF16).astype(F32), jnp.uint32)
    return uh | lax.shift_right_logical(ul, jnp.uint32(16))


def _unpack_bf16_pair(w):
    hi = lax.bitcast_convert_type(w & jnp.uint32(0xFFFF0000), F32)
    lo = lax.bitcast_convert_type(lax.shift_left(w, jnp.uint32(16)), F32)
    return hi, lo


def _gather_rows_kernel(idx_ref, src_ref, out_ref, sem, *, rows):
    i = pl.program_id(0)
    base = i * rows
    slot = lax.rem(i, 2)

    def issue(r, carry):
        t = idx_ref[base + r]
        pltpu.make_async_copy(src_ref.at[pl.ds(t, 1), :], out_ref.at[pl.ds(base + r, 1), :],
                              sem.at[slot]).start()
        return carry

    lax.fori_loop(0, rows, issue, 0, unroll=8)

    def wait_step(s):
        pltpu.make_async_copy(src_ref.at[pl.ds(0, rows), :], out_ref.at[pl.ds(0, rows), :], sem.at[s]).wait()

    @pl.when(i > 0)
    def _():
        wait_step(1 - slot)

    @pl.when(i == pl.num_programs(0) - 1)
    def _():
        wait_step(slot)


def gather_rows(src, idx, rows):
    r = idx.shape[0]
    assert r % rows == 0
    grid_spec = pltpu.PrefetchScalarGridSpec(
        num_scalar_prefetch=1,
        grid=(r // rows,),
        in_specs=[pl.BlockSpec(memory_space=pl.ANY)],
        out_specs=pl.BlockSpec(memory_space=pl.ANY),
        scratch_shapes=[pltpu.SemaphoreType.DMA((2,))],
    )
    return pl.pallas_call(
        functools.partial(_gather_rows_kernel, rows=rows),
        out_shape=jax.ShapeDtypeStruct((r, src.shape[1]), src.dtype),
        grid_spec=grid_spec,
        compiler_params=pltpu.CompilerParams(dimension_semantics=("arbitrary",), has_side_effects=True),
        name="gather_rows",
    )(idx, src)


def _router_kernel(h_ref, g_ref, wr_ref, br_ref, m_ref, eid_ref, wt_ref):
    x = h_ref[...]
    m = x * lax.rsqrt(jnp.mean(x * x, axis=-1, keepdims=True) + RMS_EPS) * g_ref[...]
    half = m.shape[1] // 2
    m_ref[...] = _pack_bf16_pair(m[:, :half], m[:, half:])
    logits = _dot(m, wr_ref[...], HIGHEST) + br_ref[...]
    lane = _iota(logits.shape, 1)
    lanef = lane.astype(F32)
    neg = -jnp.inf
    big = float(LANES)
    glog = jnp.where(lane < N_GROUPS, logits, neg)
    gmax = jnp.max(glog, axis=-1, keepdims=True)
    gsel = jnp.min(jnp.where(glog == gmax, lanef, big), axis=-1, keepdims=True)
    pg = 1.0 / jnp.sum(jnp.exp(glog - gmax), axis=-1, keepdims=True)
    lo = N_GROUPS + gsel * EXPERTS_PER_GROUP
    el = jnp.where((lanef >= lo) & (lanef < lo + EXPERTS_PER_GROUP), logits, neg)
    v1 = jnp.max(el, axis=-1, keepdims=True)
    i1 = jnp.min(jnp.where(el == v1, lanef, big), axis=-1, keepdims=True)
    el2 = jnp.where(lanef == i1, neg, el)
    v2 = jnp.max(el2, axis=-1, keepdims=True)
    i2 = jnp.min(jnp.where(el2 == v2, lanef, big), axis=-1, keepdims=True)
    e2 = jnp.exp(v2 - v1)
    w1 = pg / (1.0 + e2)
    w2 = pg * e2 / (1.0 + e2)
    eid = jnp.where(lane == 0, i1 - N_GROUPS, jnp.where(lane == 1, i2 - N_GROUPS, 0.0))
    eid_ref[...] = eid.astype(jnp.int32)
    wt_ref[...] = jnp.where(lane == 0, w1, jnp.where(lane == 1, w2, 0.0))


def router(h, g, wg, bg, we, be, tm=POINT_TILE):
    m, d = h.shape
    nr = N_GROUPS + N_EXPERTS
    wr = jnp.zeros((d, LANES), F32).at[:, :N_GROUPS].set(wg).at[:, N_GROUPS:nr].set(we)
    br = jnp.zeros((1, LANES), F32).at[0, :N_GROUPS].set(bg).at[0, N_GROUPS:nr].set(be)
    row = pl.BlockSpec((tm, d), lambda i: (i, 0))
    slab = pl.BlockSpec((tm, LANES), lambda i: (i, 0))
    return pl.pallas_call(
        _router_kernel,
        out_shape=[jax.ShapeDtypeStruct((m, d // 2), jnp.uint32), jax.ShapeDtypeStruct((m, LANES), jnp.int32),
                   jax.ShapeDtypeStruct((m, LANES), F32)],
        grid=(m // tm,),
        in_specs=[row, pl.BlockSpec((1, d), lambda i: (0, 0)), pl.BlockSpec((d, LANES), lambda i: (0, 0)),
                  pl.BlockSpec((1, LANES), lambda i: (0, 0))],
        out_specs=[pl.BlockSpec((tm, d // 2), lambda i: (i, 0)), slab, slab],
        compiler_params=_cparams("parallel"),
        name="moe_router",
    )(h, g.reshape(1, d), wr, br)


def _expert_kernel(te_ref, first_ref, nu_ref, x_ref, wg_ref, wu_ref, wd_ref, y_ref, wgu_s, wd_s, *, nk, dk):
    i = pl.program_id(0)
    kc = pl.program_id(1)
    used = i < nu_ref[0]
    de = nk * dk

    @pl.when(used & (first_ref[i] == 1))
    def _():
        for c in range(nk):
            @pl.when(kc == c)
            def _():
                wgu_s[:, c * dk:(c + 1) * dk] = wg_ref[0, 0].astype(BF16)
                wgu_s[:, de + c * dk:de + (c + 1) * dk] = wu_ref[0, 0].astype(BF16)
                wd_s[c * dk:(c + 1) * dk, :] = wd_ref[0, 0].astype(BF16)

    @pl.when(used & (kc == nk - 1))
    def _():
        xh, xl = _unpack_bf16_pair(x_ref[...])
        x = jnp.concatenate([xh.astype(BF16), xl.astype(BF16)], axis=-1)
        gu = _dot(x, wgu_s[...])
        gt = gu[:, :de]
        act = (gt * jax.nn.sigmoid(gt) * gu[:, de:]).astype(BF16)
        y = _dot(act, wd_s[...])
        half = y.shape[1] // 2
        y_ref[...] = _pack_bf16_pair(y[:, :half], y[:, half:])

    @pl.when(jnp.logical_not(used) & (kc == nk - 1))
    def _():
        y_ref[...] = jnp.zeros_like(y_ref)


def expert_ffn(xs, tile_expert, tile_first, n_used, w_gate, w_up, w_down, layer, tm=MOE_TILE, nk=4):
    p, dh = xs.shape
    d = 2 * dh
    de = w_down.shape[2]
    dk = de // nk
    assert dk * nk == de and dk % LANES == 0

    def chunk(i, kc, first):
        return jnp.where(first[i] == 1, kc, nk - 1)

    grid_spec = pltpu.PrefetchScalarGridSpec(
        num_scalar_prefetch=3,
        grid=(p // tm, nk),
        in_specs=[pl.BlockSpec((tm, dh), lambda i, kc, te, first, nu: (i, 0)),
                  pl.BlockSpec((1, 1, d, dk), lambda i, kc, te, first, nu: (layer, te[i], 0, chunk(i, kc, first))),
                  pl.BlockSpec((1, 1, d, dk), lambda i, kc, te, first, nu: (layer, te[i], 0, chunk(i, kc, first))),
                  pl.BlockSpec((1, 1, dk, d), lambda i, kc, te, first, nu: (layer, te[i], chunk(i, kc, first), 0))],
        out_specs=pl.BlockSpec((tm, dh), lambda i, kc, te, first, nu: (i, 0)),
        scratch_shapes=[pltpu.VMEM((d, 2 * de), BF16), pltpu.VMEM((de, d), BF16)],
    )
    return pl.pallas_call(
        functools.partial(_expert_kernel, nk=nk, dk=dk),
        out_shape=jax.ShapeDtypeStruct((p, dh), jnp.uint32),
        grid_spec=grid_spec,
        compiler_params=_cparams("arbitrary", "arbitrary"),
        name="moe_experts",
    )(tile_expert, tile_first, n_used, xs, w_gate, w_up, w_down)


def _combine_kernel(h_ref, y0_ref, y1_ref, wt_ref, o_ref):
    wt = wt_ref[...]
    w0, w1 = wt[:, 0:1], wt[:, 1:2]
    half = h_ref.shape[1] // 2
    a0, b0 = _unpack_bf16_pair(y0_ref[...])
    a1, b1 = _unpack_bf16_pair(y1_ref[...])
    o_ref[:, :half] = h_ref[:, :half] + w0 * a0 + w1 * a1
    o_ref[:, half:] = h_ref[:, half:] + w0 * b0 + w1 * b1


def combine(h, yp, wt, tm=POINT_TILE):
    m, d = h.shape
    nt = m // tm
    return pl.pallas_call(
        _combine_kernel,
        out_shape=jax.ShapeDtypeStruct((m, d), F32),
        grid=(nt,),
        in_specs=[pl.BlockSpec((tm, d), lambda i: (i, 0)), pl.BlockSpec((tm, d // 2), lambda i: (i, 0)),
                  pl.BlockSpec((tm, d // 2), lambda i: (i + nt, 0)), pl.BlockSpec((tm, LANES), lambda i: (i, 0))],
        out_specs=pl.BlockSpec((tm, d), lambda i: (i, 0)),
        compiler_params=_cparams("parallel"),
        name="moe_combine",
    )(h, yp, yp, wt)


def hier_moe(h, norm_g, wg, bg, we, be, w_gate, w_up, w_down, layer, tm=MOE_TILE):
    m, d = h.shape
    xm, eid_slab, wt = router(h, norm_g, wg, bg, we, be)
    npair = m * TOP_K
    eid = eid_slab[:, :TOP_K].T.reshape(npair)
    onehot = (eid[:, None] == jnp.arange(N_EXPERTS, dtype=jnp.int32)[None, :]).astype(jnp.int32)
    counts = jnp.sum(onehot, axis=0)
    rank = jnp.sum((jnp.cumsum(onehot, axis=0) - 1) * onehot, axis=1)
    padded = ((counts + tm - 1) // tm) * tm
    ends = jnp.cumsum(padded)
    starts = ends - padded
    dest = jnp.sum(onehot * starts[None, :], axis=1) + rank
    p_rows = ((npair + tm - 1) // tm + N_EXPERTS) * tm
    pair_token = jnp.arange(npair, dtype=jnp.int32) % m
    src = jnp.zeros((p_rows,), jnp.int32).at[dest].set(pair_token, mode="promise_in_bounds",
                                                         unique_indices=True)
    n_tiles = p_rows // tm
    tile_start = jnp.arange(n_tiles, dtype=jnp.int32) * tm
    n_used = (ends[-1] // tm).astype(jnp.int32).reshape(1)
    last_used = jnp.maximum(n_used[0] - 1, 0) * tm
    tile_expert = jnp.sum((ends[None, :] <= jnp.minimum(tile_start, last_used)[:, None]).astype(jnp.int32), axis=1)
    tile_expert = jnp.minimum(tile_expert, N_EXPERTS - 1)
    tile_first = jnp.concatenate([jnp.ones((1,), jnp.int32),
                                  (tile_expert[1:] != tile_expert[:-1]).astype(jnp.int32)])
    xs = gather_rows(xm, src, tm)
    ys = expert_ffn(xs, tile_expert, tile_first, n_used, w_gate, w_up, w_down, layer, tm)
    yp = gather_rows(ys, dest.astype(jnp.int32), tm)
    return combine(h, yp, wt)


def kernel(x, meta_tokens, final_norm_g, norm1_g, w_in, ret_gn_g, lru_conv_w, lru_conv_b, lru_wa, lru_ba,
           lru_wx, lru_bx, lru_lambda, rwkv_mu, rwkv_w0, rwkv_w2, rwkv_a0, rwkv_a2, rwkv_g2, rwkv_k_k,
           rwkv_k_a, rwkv_r_k, rwkv_ln_g, rwkv_ln_b, w_branch, w_out, norm2_g, moe_wg, moe_bg, moe_we,
           moe_be, moe_w_gate, moe_w_up, moe_w_down):
    b, seq, d = x.shape
    depth = w_in.shape[0]
    w = ret_gn_g.shape[1]
    t = N_META + seq
    tp = ((t + CHUNK - 1) // CHUNK) * CHUNK
    m = b * tp
    rwkv_cols = rwkv_mu.shape[1]
    rwkv_pad = 3 * w + LANES + 2 * LANES
    c_ret, c_lru, c_rwkv = 4 * w, 6 * w, 6 * w + rwkv_cols

    meta = jnp.broadcast_to(meta_tokens.astype(x.dtype)[None], (b, N_META, d))
    h = jnp.concatenate([meta, x, jnp.zeros((b, tp - t, d), x.dtype)], axis=1).reshape(m, d)

    for l in range(depth):
        wl = w_in[l]
        w_ret = wl[:, :c_ret].astype(BF16)
        w_lru = wl[:, c_ret:c_lru].astype(BF16)
        w_rwkv = jnp.pad(wl[:, c_lru:c_rwkv], ((0, 0), (0, rwkv_pad - rwkv_cols))).astype(BF16)
        w_gates = wl[:, c_rwkv:].astype(BF16)

        a = rmsnorm(h, norm1_g[l], BF16)
        p_ret = matmul(a, w_ret, BF16, tm=ROW_TILE, tn=1024, name="proj_ret")
        p_lru = matmul(a, w_lru, F32, tm=ROW_TILE, tn=1024, name="proj_lru")
        p_rwkv = matmul(a, w_rwkv, F32, tm=ROW_TILE, tn=rwkv_pad // 3, name="proj_rwkv")
        gates = matmul(a, w_gates, BF16, tm=ROW_TILE, tn=1024, act="sigmoid", name="proj_gates")

        o_ret = retention(p_ret.reshape(b, tp, 4 * w), ret_gn_g[l])
        o_lru = rg_lru(p_lru.reshape(b, tp, 2 * w), lru_conv_w[l], lru_conv_b[l], lru_wa[l], lru_ba[l],
                       lru_wx[l], lru_bx[l], lru_lambda[l])
        r_, k_, v_, lw_, a_, g_ = rwkv_proj(p_rwkv.reshape(b, tp, rwkv_pad), rwkv_mu[l], rwkv_w0[l],
                                            rwkv_w2[l], rwkv_a0[l], rwkv_a2[l], rwkv_g2[l], w)
        o_rwkv = rwkv_scan(r_, k_, v_, lw_, a_, g_, rwkv_k_k[l], rwkv_k_a[l], rwkv_r_k[l],
                           rwkv_ln_g[l], rwkv_ln_b[l])

        merged = merge(o_ret.reshape(m, w), o_lru.reshape(m, w), o_rwkv.reshape(m, w),
                       w_branch[l].astype(BF16), gates)
        h = matmul(merged, w_out[l].astype(BF16), F32, tm=ROW_TILE, tn=1024, residual=h, name="proj_out")

        h = hier_moe(h, norm2_g[l], moe_wg[l], moe_bg[l], moe_we[l], moe_be[l], moe_w_gate, moe_w_up,
                     moe_w_down, l)

    out = rmsnorm(h, final_norm_g, F32).reshape(b, tp, d)
    return out[:, N_META:t]
```

```python
import functools
import math

import jax
import jax.numpy as jnp
from jax import lax
from jax.experimental import pallas as pl
from jax.experimental.pallas import tpu as pltpu

F32 = jnp.float32
BF16 = jnp.bfloat16
HIGHEST = lax.Precision.HIGHEST

LANES = 128
SUBLANES = 8
VMEM_BYTES_V7X = 64 * 1024 * 1024
VMEM_LIMIT = VMEM_BYTES_V7X - 8 * 1024 * 1024

N_META = 16
CHUNK = 128
RET_DIM = 128
ROPE_BASE = 10000.0
RET_EPS = 1e-5
LRU_BLOCK = 128
CONV_W = 4
LRU_C = 8.0
RWKV_DIM = 64
DECAY_LORA = 64
AAA_LORA = 64
GATE_LORA = 160
RWKV_EPS = 64e-5
N_GROUPS = 4
EXPERTS_PER_GROUP = 8
N_EXPERTS = N_GROUPS * EXPERTS_PER_GROUP
TOP_K = 2
RMS_EPS = 1e-6

RWKV_CHUNK = 64
RWKV_TBLK = 128
ROW_TILE = 512
POINT_TILE = 256
MOE_TILE = 512


def _cparams(*sem):
    return pltpu.CompilerParams(dimension_semantics=sem, vmem_limit_bytes=VMEM_LIMIT)


def _dot(a, b, precision=None):
    return jnp.dot(a, b, preferred_element_type=F32, precision=precision)


def _dot_nt(a, b, precision=None):
    return lax.dot_general(a, b, (((1,), (1,)), ((), ())), preferred_element_type=F32, precision=precision)


def _iota(shape, dim):
    return lax.broadcasted_iota(jnp.int32, shape, dim)


def _blk(i, n):
    shift = n.bit_length() - 1
    assert 1 << shift == n
    return lax.shift_right_logical(i, jnp.int32(shift))


def _rmsnorm_kernel(x_ref, g_ref, o_ref):
    x = x_ref[...]
    y = x * lax.rsqrt(jnp.mean(x * x, axis=-1, keepdims=True) + RMS_EPS)
    o_ref[...] = (y * g_ref[...]).astype(o_ref.dtype)


def rmsnorm(x, g, out_dtype, tm=POINT_TILE):
    m, d = x.shape
    return pl.pallas_call(
        _rmsnorm_kernel,
        out_shape=jax.ShapeDtypeStruct((m, d), out_dtype),
        grid=(m // tm,),
        in_specs=[pl.BlockSpec((tm, d), lambda i: (i, 0)), pl.BlockSpec((1, d), lambda i: (0, 0))],
        out_specs=pl.BlockSpec((tm, d), lambda i: (i, 0)),
        compiler_params=_cparams("parallel"),
        name="rmsnorm",
    )(x, g.reshape(1, d))


def _matmul_kernel(a_ref, w_ref, o_ref, *, act):
    acc = _dot(a_ref[...], w_ref[...])
    if act == "sigmoid":
        acc = jax.nn.sigmoid(acc)
    o_ref[...] = acc.astype(o_ref.dtype)


def _matmul_res_kernel(a_ref, w_ref, r_ref, o_ref):
    o_ref[...] = r_ref[...] + _dot(a_ref[...], w_ref[...])


def matmul(a, w, out_dtype, *, tm, tn, act=None, residual=None, name="matmul"):
    m, k = a.shape
    n = w.shape[1]
    assert m % tm == 0 and n % tn == 0, (m, n, tm, tn)
    in_specs = [pl.BlockSpec((tm, k), lambda i, j: (i, 0)), pl.BlockSpec((k, tn), lambda i, j: (0, j))]
    args = [a, w]
    if residual is None:
        body = functools.partial(_matmul_kernel, act=act)
    else:
        body = _matmul_res_kernel
        in_specs.append(pl.BlockSpec((tm, tn), lambda i, j: (i, j)))
        args.append(residual)
    return pl.pallas_call(
        body,
        out_shape=jax.ShapeDtypeStruct((m, n), out_dtype),
        grid=(m // tm, n // tn),
        in_specs=in_specs,
        out_specs=pl.BlockSpec((tm, tn), lambda i, j: (i, j)),
        compiler_params=_cparams("parallel", "arbitrary"),
        name=name,
    )(*args)


def _retention_kernel(q_ref, k_ref, v_ref, g_ref, cos_ref, sin_ref, gn_ref, o_ref, state_ref, *, heads):
    c = pl.program_id(1)

    @pl.when(c == 0)
    def _():
        state_ref[...] = jnp.zeros_like(state_ref)

    d = RET_DIM
    cos = cos_ref[...]
    sin = sin_ref[...]
    row = _iota((CHUNK, CHUNK), 0)
    col = _iota((CHUNK, CHUNK), 1)
    diff = (row - col).astype(F32)
    causal = row >= col
    idx = _iota((CHUNK, 1), 0).astype(F32)
    for h in range(heads):
        lg = math.log1p(-(2.0 ** (-5.0 - h)))
        sl = slice(h * d, (h + 1) * d)
        q = q_ref[0, :, sl].astype(F32)
        k = k_ref[0, :, sl].astype(F32)
        v = v_ref[0, :, sl].astype(BF16)
        g = g_ref[0, :, sl].astype(F32)
        qr = q * cos + pltpu.roll(q, d // 2, 1) * sin
        kr = (k * cos + pltpu.roll(k, d // 2, 1) * sin) * (d ** -0.5)
        decay = jnp.where(causal, jnp.exp(lg * jnp.maximum(diff, 0.0)), 0.0)
        scores = _dot_nt(qr.astype(BF16), kr.astype(BF16)) * decay
        o = _dot(scores.astype(BF16), v)
        xi = jnp.exp(lg * (idx + 1.0))
        st = state_ref[h]
        o = o + _dot((qr * xi).astype(BF16), st.astype(BF16))
        zeta = jnp.exp(lg * (CHUNK - 1.0 - idx))
        kv = _dot((kr * zeta).T.astype(BF16), v)
        state_ref[h] = math.exp(lg * CHUNK) * st + kv
        mu = jnp.mean(o, axis=-1, keepdims=True)
        oc = o - mu
        var = jnp.mean(oc * oc, axis=-1, keepdims=True)
        on = oc * lax.rsqrt(var + RET_EPS) * gn_ref[:, sl]
        o_ref[0, :, sl] = (g * jax.nn.sigmoid(g) * on).astype(o_ref.dtype)


def _rope_tables(tp):
    half = RET_DIM // 2
    inv_freq = ROPE_BASE ** (-jnp.arange(half, dtype=F32) / half)
    ang = jnp.arange(tp, dtype=F32)[:, None] * inv_freq[None, :]
    cos, sin = jnp.cos(ang), jnp.sin(ang)
    return jnp.concatenate([cos, cos], axis=-1), jnp.concatenate([-sin, sin], axis=-1)


def retention(pr, gn_g):
    b, tp, w4 = pr.shape
    w = w4 // 4
    heads = w // RET_DIM
    cos, sin = _rope_tables(tp)
    spec = lambda j: pl.BlockSpec((1, CHUNK, w), lambda bi, c, j=j: (bi, c, j))
    return pl.pallas_call(
        functools.partial(_retention_kernel, heads=heads),
        out_shape=jax.ShapeDtypeStruct((b, tp, w), BF16),
        grid=(b, tp // CHUNK),
        in_specs=[spec(0), spec(1), spec(2), spec(3),
                  pl.BlockSpec((CHUNK, RET_DIM), lambda bi, c: (c, 0)),
                  pl.BlockSpec((CHUNK, RET_DIM), lambda bi, c: (c, 0)),
                  pl.BlockSpec((1, w), lambda bi, c: (0, 0))],
        out_specs=pl.BlockSpec((1, CHUNK, w), lambda bi, c: (bi, c, 0)),
        scratch_shapes=[pltpu.VMEM((heads, RET_DIM, RET_DIM), F32)],
        compiler_params=_cparams("parallel", "arbitrary"),
        name="retention",
    )(pr, pr, pr, pr, cos, sin, gn_g.reshape(1, w))


def _lru_kernel(x_ref, y_ref, cw_ref, cb_ref, wa_ref, ba_ref, wx_ref, bx_ref, lam_ref, o_ref,
                xs_ref, h_ref, *, tc):
    c = pl.program_id(1)
    w = x_ref.shape[-1]

    @pl.when(c == 0)
    def _():
        xs_ref[0:SUBLANES, :] = jnp.zeros((SUBLANES, w), F32)
        h_ref[...] = jnp.zeros_like(h_ref)

    x = x_ref[0].astype(F32)
    xs_ref[SUBLANES:, :] = x
    xc = cb_ref[...] + cw_ref[CONV_W - 1:CONV_W, :] * x
    for j in range(CONV_W - 1):
        off = SUBLANES - (CONV_W - 1) + j
        xc = xc + cw_ref[j:j + 1, :] * xs_ref[off:off + tc, :]
    xs_ref[0:SUBLANES, :] = x[tc - SUBLANES:, :]

    xcb = xc.astype(BF16)
    nb = w // LRU_BLOCK
    ra = jnp.concatenate(
        [_dot(xcb[:, n * LRU_BLOCK:(n + 1) * LRU_BLOCK], wa_ref[n]) for n in range(nb)], axis=-1)
    ri = jnp.concatenate(
        [_dot(xcb[:, n * LRU_BLOCK:(n + 1) * LRU_BLOCK], wx_ref[n]) for n in range(nb)], axis=-1)
    r = jax.nn.sigmoid(ra + ba_ref[...])
    i = jax.nn.sigmoid(ri + bx_ref[...])
    nlam = -lam_ref[...]
    softplus = jnp.maximum(nlam, 0.0) + jnp.log1p(jnp.exp(-jnp.abs(nlam)))
    log_a = -LRU_C * r * softplus
    a = jnp.exp(log_a)
    bb = jnp.sqrt(1.0 - jnp.exp(2.0 * log_a)) * (i * xc)

    row = _iota((tc, w), 0)
    s = 1
    while s < tc:
        a_sh = pltpu.roll(a, s, 0)
        b_sh = pltpu.roll(bb, s, 0)
        m = row >= s
        bb = jnp.where(m, a * b_sh + bb, bb)
        a = jnp.where(m, a * a_sh, a)
        s *= 2
    h = a * h_ref[...] + bb
    h_ref[...] = h[tc - 1:tc, :]
    y = y_ref[0].astype(F32)
    o_ref[0] = (h * jax.nn.gelu(y)).astype(o_ref.dtype)


def rg_lru(pl_xy, conv_w, conv_b, wa, ba, wx, bx, lam, tc=256):
    b, tp, w2 = pl_xy.shape
    w = w2 // 2
    if tp % tc:
        tc = CHUNK
    nb = w // LRU_BLOCK
    vec = lambda: pl.BlockSpec((1, w), lambda bi, c: (0, 0))
    return pl.pallas_call(
        functools.partial(_lru_kernel, tc=tc),
        out_shape=jax.ShapeDtypeStruct((b, tp, w), BF16),
        grid=(b, tp // tc),
        in_specs=[pl.BlockSpec((1, tc, w), lambda bi, c: (bi, c, 0)),
                  pl.BlockSpec((1, tc, w), lambda bi, c: (bi, c, 1)),
                  pl.BlockSpec((CONV_W, w), lambda bi, c: (0, 0)), vec(),
                  pl.BlockSpec((nb, LRU_BLOCK, LRU_BLOCK), lambda bi, c: (0, 0, 0)), vec(),
                  pl.BlockSpec((nb, LRU_BLOCK, LRU_BLOCK), lambda bi, c: (0, 0, 0)), vec(), vec()],
        out_specs=pl.BlockSpec((1, tc, w), lambda bi, c: (bi, c, 0)),
        scratch_shapes=[pltpu.VMEM((tc + SUBLANES, w), F32), pltpu.VMEM((1, w), F32)],
        compiler_params=_cparams("parallel", "arbitrary"),
        name="rg_lru",
    )(pl_xy, pl_xy, conv_w, conv_b.reshape(1, w), wa.astype(BF16), ba.reshape(1, w),
      wx.astype(BF16), bx.reshape(1, w), lam.reshape(1, w))


def _split_bf16(x):
    hi = x.astype(BF16)
    return hi, (x - hi.astype(F32)).astype(BF16)


def _mm_exact_rhs(a, b):
    ah, al = _split_bf16(a)
    b = b.astype(BF16)
    return _dot(ah, b) + _dot(al, b)


def _mm3(a, b):
    ah, al = _split_bf16(a)
    bh, bl = _split_bf16(b)
    return _dot(ah, bh) + (_dot(al, bh) + _dot(ah, bl))


def _bmm(a, b):
    return jnp.einsum("pmk,pkn->pmn", a.astype(BF16), b.astype(BF16), preferred_element_type=F32)


def _bmm_nt(a, b):
    return jnp.einsum("pmk,pnk->pmn", a.astype(BF16), b.astype(BF16), preferred_element_type=F32)


def _bt(x):
    return jnp.stack([x[p].T for p in range(x.shape[0])], axis=0)


def _rwkv_scan_kernel(pr_ref, pk_ref, pv_ref, pl_ref, qr_ref, qk_ref, qv_ref, ql_ref,
                      mur_ref, muk_ref, muv_ref, mul_ref, w0_ref, w2_ref, a0_ref, a2_ref, g2_ref,
                      kk_ref, ka_ref, rk_ref, lng_ref, lnb_ref, o_ref, s_ref, *, npair):
    n = pl.program_id(2)

    @pl.when(n == 0)
    def _():
        s_ref[...] = jnp.zeros_like(s_ref)

    P = npair
    tb = pr_ref.shape[1]
    C = RWKV_CHUNK
    nch = tb // C
    hd = RWKV_DIM

    def seq(x):
        return jnp.stack([x[:, p * LANES:(p + 1) * LANES] for p in range(P)], axis=0)

    def shifted(p_ref, q_ref, mu_ref):
        x = p_ref[0]
        prev = jnp.where(n == 0, 0.0, q_ref[0][SUBLANES - 1:SUBLANES, :])
        x_prev = jnp.where(_iota(x.shape, 0) == 0, prev, pltpu.roll(x, 1, 0))
        return x + (x_prev - x) * mu_ref[...]

    def vec(ref):
        x = ref[...]
        return jnp.stack([x[:, p * LANES:(p + 1) * LANES] for p in range(P)], axis=0)

    lane = _iota((1, 1, LANES), 2)
    hmask = [(lane < hd).astype(F32), (lane >= hd).astype(F32)]
    li = _iota((LANES, LANES), 0)
    lj = _iota((LANES, LANES), 1)
    same_head = ((li >= hd) == (lj >= hd)).astype(F32)
    eye = (li == lj).astype(F32)

    def head_sum(x, scale=1.0):
        return _mm_exact_rhs(x.reshape(P * tb, LANES), same_head * scale).reshape(P, tb, LANES)

    r, k, v = (seq(shifted(p, q, mu)) for p, q, mu in
               ((pr_ref, qr_ref, mur_ref), (pk_ref, qk_ref, muk_ref), (pv_ref, qv_ref, muv_ref)))
    lora = shifted(pl_ref, ql_ref, mul_ref)
    xwa = lora[:, :LANES]
    xg = lora[:, LANES:]
    nw = -(w0_ref[...] + _mm3(jnp.tanh(xwa), w2_ref[...]))
    softplus = jnp.maximum(nw, 0.0) + jnp.log1p(jnp.exp(-jnp.abs(nw)))
    lw = seq(-jnp.exp(-softplus - 0.5))
    a = seq(jax.nn.sigmoid(a0_ref[...] + _mm3(xwa, a2_ref[...])))
    g = seq(_mm3(jax.nn.sigmoid(xg), g2_ref[...]))

    kkr = k * vec(kk_ref)
    kk = kkr * lax.rsqrt(jnp.maximum(head_sum(kkr * kkr), 1e-24))
    kmod = k * (1.0 + (a - 1.0) * vec(ka_ref))
    be = kk * a

    pos = _iota((P * tb, LANES), 0) & (C - 1)
    cw2 = lw.reshape(P * tb, LANES)
    sh = 1
    while sh < C:
        cw2 = cw2 + jnp.where(pos >= sh, pltpu.roll(cw2, sh, 0), 0.0)
        sh *= 2
    cw = cw2.reshape(P, tb, LANES)
    cl = jnp.concatenate(
        [jnp.broadcast_to(cw[:, (c + 1) * C - 1:(c + 1) * C, :], (P, C, LANES)) for c in range(nch)], axis=1)
    e_out = jnp.exp(-cw)
    e_rem = jnp.exp(cl - cw)
    rt = r * jnp.exp(cw)
    at = -kk * jnp.exp(cw - lw)
    kt = kmod * e_out
    bt = be * e_out
    kh = kmod * e_rem
    bh = be * e_rem

    def stack_masked(x):
        return jnp.concatenate(
            [x[:, c * C:(c + 1) * C] * hmask[h] for c in range(nch) for h in range(2)], axis=1)

    def stack_dup(x):
        return jnp.concatenate([x[:, c * C:(c + 1) * C] for c in range(nch) for h in range(2)], axis=1)

    sr = 2 * nch * C
    at_s = stack_masked(at)
    rt_s = stack_masked(rt)
    v_s = stack_masked(v)
    gram = _bmm_nt(jnp.concatenate([at_s, rt_s], axis=1),
                   jnp.concatenate([stack_dup(bt), stack_dup(kt)], axis=1))
    ui = _iota((sr, sr), 0)
    uj = _iota((sr, sr), 1)
    unit = _blk(ui, C) == _blk(uj, C)
    strict = unit & (ui > uj)
    incl = unit & (ui >= uj)
    a_ab = jnp.where(strict, gram[:, :sr, :sr], 0.0)
    a_ak = jnp.where(strict, gram[:, :sr, sr:], 0.0)
    a_rb = jnp.where(incl, gram[:, sr:, :sr], 0.0)
    a_rk = jnp.where(incl, gram[:, sr:, sr:], 0.0)

    x = (ui == uj).astype(F32) + jnp.where(_blk(ui, 2) == _blk(uj, 2), a_ab, 0.0)
    nblk = 2
    while nblk < C:
        e = jnp.where((_blk(ui, 2 * nblk) == _blk(uj, 2 * nblk)) & (_blk(ui, nblk) != _blk(uj, nblk)), a_ab, 0.0)
        x = x + _bmm(_bmm(x, e), x)
        nblk *= 2

    ah = _bmm(x, at_s)
    ul = _bmm(x, _bmm(a_ak, v_s))
    rh = rt_s + _bmm(a_rb, ah)
    yl = _bmm(a_rb, ul) + _bmm(a_rk, v_s)
    bh_s = stack_masked(bh)
    kh_s = stack_masked(kh)

    s = s_ref[...]
    ys = []
    for c in range(nch):
        u0 = slice(2 * c * C, (2 * c + 1) * C)
        u1 = slice((2 * c + 1) * C, (2 * c + 2) * C)
        uc = slice(2 * c * C, (2 * c + 2) * C)
        ys.append(_bmm(rh[:, u0] + rh[:, u1], s) + yl[:, u0] + yl[:, u1])
        wc = jnp.exp(cl[:, c * C:c * C + 1, :])
        bht = _bt(bh_s[:, uc])
        pm = eye * wc + _bmm(bht, ah[:, uc])
        z = _bmm(bht, ul[:, uc]) + _bmm(_bt(kh_s[:, uc]), v_s[:, uc])
        s = _bmm(pm, s) + z
    s_ref[...] = s
    y = jnp.concatenate(ys, axis=1)

    mu = head_sum(y, 1.0 / hd)
    yc = y - mu
    var = head_sum(yc * yc, 1.0 / hd)
    yn = yc * lax.rsqrt(var + RWKV_EPS) * vec(lng_ref) + vec(lnb_ref)
    bonus = head_sum(r * kmod * vec(rk_ref)) * v
    out = (yn + bonus) * g
    o_ref[0] = jnp.concatenate([out[p] for p in range(P)], axis=-1).astype(o_ref.dtype)


def rwkv_mix(pp, mu, w0, w2, a0, a2, g2, k_k, k_a, r_k, ln_g, ln_b, w, npair=4):
    b, tp, n = pp.shape
    tb = RWKV_TBLK
    wl = npair * LANES
    nl = n - 3 * w
    assert w % wl == 0 and (3 * w) % nl == 0 and nl == 3 * LANES
    nj = w // wl
    lblk = 3 * w // nl
    rows8 = tb // SUBLANES
    mu_p = jnp.zeros((1, n), F32).at[0, :mu.shape[0]].set(mu)
    w2_p = jnp.zeros((LANES, w), F32).at[:DECAY_LORA].set(w2)
    a2_p = jnp.zeros((LANES, w), F32).at[DECAY_LORA:DECAY_LORA + AAA_LORA].set(a2)
    g2_p = jnp.zeros((nl - LANES, w), F32).at[:GATE_LORA].set(g2)

    def cur(width, col):
        return pl.BlockSpec((1, tb, width), lambda bi, j, t: (bi, t, col(j)))

    def prev(width, col):
        return pl.BlockSpec((1, SUBLANES, width), lambda bi, j, t: (bi, jnp.maximum(t * rows8 - 1, 0), col(j)))

    def mu_spec(width, col):
        return pl.BlockSpec((1, width), lambda bi, j, t: (0, col(j)))

    cols = [lambda j: j, lambda j: nj + j, lambda j: 2 * nj + j]
    lcol = lambda j: lblk
    vec = pl.BlockSpec((1, wl), lambda bi, j, t: (0, j))
    mat = lambda rows: pl.BlockSpec((rows, wl), lambda bi, j, t: (0, j))
    in_specs = ([cur(wl, c) for c in cols] + [cur(nl, lcol)] + [prev(wl, c) for c in cols] + [prev(nl, lcol)]
                + [mu_spec(wl, c) for c in cols] + [mu_spec(nl, lcol)]
                + [vec, mat(LANES), vec, mat(LANES), mat(nl - LANES)] + [vec] * 5)
    return pl.pallas_call(
        functools.partial(_rwkv_scan_kernel, npair=npair),
        out_shape=jax.ShapeDtypeStruct((b, tp, w), BF16),
        grid=(b, nj, tp // tb),
        in_specs=in_specs,
        out_specs=pl.BlockSpec((1, tb, wl), lambda bi, j, t: (bi, t, j)),
        scratch_shapes=[pltpu.VMEM((npair, LANES, LANES), F32)],
        compiler_params=_cparams("parallel", "parallel", "arbitrary"),
        name="rwkv_mix",
    )(pp, pp, pp, pp, pp, pp, pp, pp, mu_p, mu_p, mu_p, mu_p,
      w0.reshape(1, w), w2_p, a0.reshape(1, w), a2_p, g2_p,
      k_k.reshape(1, w), k_a.reshape(1, w), r_k.reshape(1, w), ln_g.reshape(1, w), ln_b.reshape(1, w))


def _merge_kernel(o1_ref, o2_ref, o3_ref, wb_ref, g1_ref, g2_ref, g3_ref, out_ref):
    acc = g1_ref[...].astype(F32) * _dot(o1_ref[...], wb_ref[0])
    acc = acc + g2_ref[...].astype(F32) * _dot(o2_ref[...], wb_ref[1])
    acc = acc + g3_ref[...].astype(F32) * _dot(o3_ref[...], wb_ref[2])
    out_ref[...] = acc.astype(out_ref.dtype)


def merge(o1, o2, o3, w_branch, gates, tm=ROW_TILE, tn=1024):
    m, w = o1.shape
    d = w_branch.shape[2]
    nj = d // tn
    ospec = pl.BlockSpec((tm, w), lambda i, j: (i, 0))
    gspec = lambda b: pl.BlockSpec((tm, tn), lambda i, j, b=b: (i, b * nj + j))
    return pl.pallas_call(
        _merge_kernel,
        out_shape=jax.ShapeDtypeStruct((m, d), BF16),
        grid=(m // tm, nj),
        in_specs=[ospec, ospec, ospec, pl.BlockSpec((3, w, tn), lambda i, j: (0, 0, j)),
                  gspec(0), gspec(1), gspec(2)],
        out_specs=pl.BlockSpec((tm, tn), lambda i, j: (i, j)),
        compiler_params=_cparams("parallel", "arbitrary"),
        name="merge",
    )(o1, o2, o3, w_branch, gates, gates, gates)


def _pack_bf16_pair(hi, lo):
    uh = lax.bitcast_convert_type(hi.astype(BF16).astype(F32), jnp.uint32)
    ul = lax.bitcast_convert_type(lo.astype(BF16).astype(F32), jnp.uint32)
    return uh | lax.shift_right_logical(ul, jnp.uint32(16))


def _unpack_bf16_pair(w):
    hi = lax.bitcast_convert_type(w & jnp.uint32(0xFFFF0000), F32)
    lo = lax.bitcast_convert_type(lax.shift_left(w, jnp.uint32(16)), F32)
    return hi, lo


def _store_token_rows(ref, val):
    tm, n = val.shape
    seg = n // LANES
    for s in range(seg):
        ref[pl.ds(s, tm, stride=seg), :] = val[:, s * LANES:(s + 1) * LANES]


def _load_token_rows(ref, seg):
    tm = ref.shape[0] // seg
    return jnp.concatenate([ref[pl.ds(s, tm, stride=seg), :] for s in range(seg)], axis=-1)


def _gather_rows_kernel(idx_ref, src_ref, out_ref, sem, *, rows, seg):
    i = pl.program_id(0)
    base = i * rows
    slot = lax.rem(i, 2)

    def issue(r, carry):
        t = pl.multiple_of(idx_ref[base + r] * seg, seg)
        o = pl.multiple_of((base + r) * seg, seg)
        pltpu.make_async_copy(src_ref.at[pl.ds(t, seg), :], out_ref.at[pl.ds(o, seg), :], sem.at[slot]).start()
        return carry

    lax.fori_loop(0, rows, issue, 0, unroll=8)

    def wait_step(s):
        pltpu.make_async_copy(src_ref.at[pl.ds(0, rows * seg), :], out_ref.at[pl.ds(0, rows * seg), :],
                              sem.at[s]).wait()

    @pl.when(i > 0)
    def _():
        wait_step(1 - slot)

    @pl.when(i == pl.num_programs(0) - 1)
    def _():
        wait_step(slot)


def gather_rows(src, idx, rows, seg):
    r = idx.shape[0]
    assert r % rows == 0 and seg % SUBLANES == 0
    grid_spec = pltpu.PrefetchScalarGridSpec(
        num_scalar_prefetch=1,
        grid=(r // rows,),
        in_specs=[pl.BlockSpec(memory_space=pl.ANY)],
        out_specs=pl.BlockSpec(memory_space=pl.ANY),
        scratch_shapes=[pltpu.SemaphoreType.DMA((2,))],
    )
    return pl.pallas_call(
        functools.partial(_gather_rows_kernel, rows=rows, seg=seg),
        out_shape=jax.ShapeDtypeStruct((r * seg, LANES), src.dtype),
        grid_spec=grid_spec,
        compiler_params=pltpu.CompilerParams(dimension_semantics=("arbitrary",), has_side_effects=True),
        name="gather_rows",
    )(idx, src)


def _router_kernel(h_ref, g_ref, wr_ref, br_ref, m_ref, eid_ref, wt_ref):
    x = h_ref[...]
    m = x * lax.rsqrt(jnp.mean(x * x, axis=-1, keepdims=True) + RMS_EPS) * g_ref[...]
    half = m.shape[1] // 2
    _store_token_rows(m_ref, _pack_bf16_pair(m[:, :half], m[:, half:]))
    logits = _dot(m, wr_ref[...], HIGHEST) + br_ref[...]
    lane = _iota(logits.shape, 1)
    lanef = lane.astype(F32)
    neg = -jnp.inf
    big = float(LANES)
    glog = jnp.where(lane < N_GROUPS, logits, neg)
    gmax = jnp.max(glog, axis=-1, keepdims=True)
    gsel = jnp.min(jnp.where(glog == gmax, lanef, big), axis=-1, keepdims=True)
    pg = 1.0 / jnp.sum(jnp.exp(glog - gmax), axis=-1, keepdims=True)
    lo = N_GROUPS + gsel * EXPERTS_PER_GROUP
    el = jnp.where((lanef >= lo) & (lanef < lo + EXPERTS_PER_GROUP), logits, neg)
    v1 = jnp.max(el, axis=-1, keepdims=True)
    i1 = jnp.min(jnp.where(el == v1, lanef, big), axis=-1, keepdims=True)
    el2 = jnp.where(lanef == i1, neg, el)
    v2 = jnp.max(el2, axis=-1, keepdims=True)
    i2 = jnp.min(jnp.where(el2 == v2, lanef, big), axis=-1, keepdims=True)
    e2 = jnp.exp(v2 - v1)
    w1 = pg / (1.0 + e2)
    w2 = pg * e2 / (1.0 + e2)
    eid = jnp.where(lane == 0, i1 - N_GROUPS, jnp.where(lane == 1, i2 - N_GROUPS, 0.0))
    eid_ref[...] = eid.astype(jnp.int32)
    wt_ref[...] = jnp.where(lane == 0, w1, jnp.where(lane == 1, w2, 0.0))


def router(h, g, wg, bg, we, be, tm=POINT_TILE):
    m, d = h.shape
    nr = N_GROUPS + N_EXPERTS
    wr = jnp.zeros((d, LANES), F32).at[:, :N_GROUPS].set(wg).at[:, N_GROUPS:nr].set(we)
    br = jnp.zeros((1, LANES), F32).at[0, :N_GROUPS].set(bg).at[0, N_GROUPS:nr].set(be)
    row = pl.BlockSpec((tm, d), lambda i: (i, 0))
    slab = pl.BlockSpec((tm, LANES), lambda i: (i, 0))
    seg = d // 2 // LANES
    return pl.pallas_call(
        _router_kernel,
        out_shape=[jax.ShapeDtypeStruct((m * seg, LANES), jnp.uint32), jax.ShapeDtypeStruct((m, LANES), jnp.int32),
                   jax.ShapeDtypeStruct((m, LANES), F32)],
        grid=(m // tm,),
        in_specs=[row, pl.BlockSpec((1, d), lambda i: (0, 0)), pl.BlockSpec((d, LANES), lambda i: (0, 0)),
                  pl.BlockSpec((1, LANES), lambda i: (0, 0))],
        out_specs=[pl.BlockSpec((tm * seg, LANES), lambda i: (i, 0)), slab, slab],
        compiler_params=_cparams("parallel"),
        name="moe_router",
    )(h, g.reshape(1, d), wr, br)


def _expert_kernel(te_ref, first_ref, nu_ref, x_ref, wg_ref, wu_ref, wd_ref, y_ref, wgu_s, wd_s, *, nk, dk):
    i = pl.program_id(0)
    kc = pl.program_id(1)
    used = i < nu_ref[0]
    de = nk * dk

    @pl.when(used & (first_ref[i] == 1))
    def _():
        for c in range(nk):
            @pl.when(kc == c)
            def _():
                wgu_s[:, c * dk:(c + 1) * dk] = wg_ref[0, 0].astype(BF16)
                wgu_s[:, de + c * dk:de + (c + 1) * dk] = wu_ref[0, 0].astype(BF16)
                wd_s[c * dk:(c + 1) * dk, :] = wd_ref[0, 0].astype(BF16)

    @pl.when(used & (kc == nk - 1))
    def _():
        xh, xl = _unpack_bf16_pair(_load_token_rows(x_ref, wd_s.shape[1] // 2 // LANES))
        x = jnp.concatenate([xh.astype(BF16), xl.astype(BF16)], axis=-1)
        gu = _dot(x, wgu_s[...])
        gt = gu[:, :de]
        act = (gt * jax.nn.sigmoid(gt) * gu[:, de:]).astype(BF16)
        y = _dot(act, wd_s[...])
        half = y.shape[1] // 2
        _store_token_rows(y_ref, _pack_bf16_pair(y[:, :half], y[:, half:]))

    @pl.when(jnp.logical_not(used) & (kc == nk - 1))
    def _():
        y_ref[...] = jnp.zeros_like(y_ref)


def expert_ffn(xs, tile_expert, tile_first, n_used, w_gate, w_up, w_down, layer, tm=MOE_TILE, nk=4):
    d = w_down.shape[3]
    seg = d // 2 // LANES
    p = xs.shape[0] // seg
    de = w_down.shape[2]
    dk = de // nk
    assert dk * nk == de and dk % LANES == 0

    def chunk(i, kc, first):
        return jnp.where(first[i] == 1, kc, nk - 1)

    grid_spec = pltpu.PrefetchScalarGridSpec(
        num_scalar_prefetch=3,
        grid=(p // tm, nk),
        in_specs=[pl.BlockSpec((tm * seg, LANES), lambda i, kc, te, first, nu: (i, 0)),
                  pl.BlockSpec((1, 1, d, dk), lambda i, kc, te, first, nu: (layer, te[i], 0, chunk(i, kc, first))),
                  pl.BlockSpec((1, 1, d, dk), lambda i, kc, te, first, nu: (layer, te[i], 0, chunk(i, kc, first))),
                  pl.BlockSpec((1, 1, dk, d), lambda i, kc, te, first, nu: (layer, te[i], chunk(i, kc, first), 0))],
        out_specs=pl.BlockSpec((tm * seg, LANES), lambda i, kc, te, first, nu: (i, 0)),
        scratch_shapes=[pltpu.VMEM((d, 2 * de), BF16), pltpu.VMEM((de, d), BF16)],
    )
    return pl.pallas_call(
        functools.partial(_expert_kernel, nk=nk, dk=dk),
        out_shape=jax.ShapeDtypeStruct((p * seg, LANES), jnp.uint32),
        grid_spec=grid_spec,
        compiler_params=_cparams("arbitrary", "arbitrary"),
        name="moe_experts",
    )(tile_expert, tile_first, n_used, xs, w_gate, w_up, w_down)


def _combine_kernel(h_ref, y0_ref, y1_ref, wt_ref, g_ref, *out_refs, emit_h):
    n_ref = out_refs[-1]
    wt = wt_ref[...]
    w0, w1 = wt[:, 0:1], wt[:, 1:2]
    d = h_ref.shape[1]
    half = d // 2
    seg = half // LANES
    a0, b0 = _unpack_bf16_pair(_load_token_rows(y0_ref, seg))
    a1, b1 = _unpack_bf16_pair(_load_token_rows(y1_ref, seg))
    lo = h_ref[:, :half] + w0 * a0 + w1 * a1
    hi = h_ref[:, half:] + w0 * b0 + w1 * b1
    if emit_h:
        out_refs[0][:, :half] = lo
        out_refs[0][:, half:] = hi
    ms = (jnp.sum(lo * lo, axis=-1, keepdims=True) + jnp.sum(hi * hi, axis=-1, keepdims=True)) * (1.0 / d)
    inv = lax.rsqrt(ms + RMS_EPS)
    n_ref[:, :half] = (lo * inv * g_ref[:, :half]).astype(n_ref.dtype)
    n_ref[:, half:] = (hi * inv * g_ref[:, half:]).astype(n_ref.dtype)


def combine(h, yp, wt, g_next, norm_dtype, emit_h, tm=POINT_TILE):
    m, d = h.shape
    nt = m // tm
    seg = d // 2 // LANES
    row = pl.BlockSpec((tm, d), lambda i: (i, 0))
    outs = [jax.ShapeDtypeStruct((m, d), F32)] * emit_h + [jax.ShapeDtypeStruct((m, d), norm_dtype)]
    return pl.pallas_call(
        functools.partial(_combine_kernel, emit_h=emit_h),
        out_shape=outs,
        grid=(nt,),
        in_specs=[row, pl.BlockSpec((tm * seg, LANES), lambda i: (i, 0)),
                  pl.BlockSpec((tm * seg, LANES), lambda i: (i + nt, 0)),
                  pl.BlockSpec((tm, LANES), lambda i: (i, 0)), pl.BlockSpec((1, d), lambda i: (0, 0))],
        out_specs=[row] * len(outs),
        compiler_params=_cparams("parallel"),
        name="moe_combine",
    )(h, yp, yp, wt, g_next.reshape(1, d))


def hier_moe(h, norm_g, wg, bg, we, be, w_gate, w_up, w_down, layer, g_next, norm_dtype, emit_h, tm=MOE_TILE):
    m, d = h.shape
    xm, eid_slab, wt = router(h, norm_g, wg, bg, we, be)
    npair = m * TOP_K
    eid = eid_slab[:, :TOP_K].T.reshape(npair)
    onehot = (eid[:, None] == jnp.arange(N_EXPERTS, dtype=jnp.int32)[None, :]).astype(jnp.int32)
    counts = jnp.sum(onehot, axis=0)
    rank = jnp.sum((jnp.cumsum(onehot, axis=0) - 1) * onehot, axis=1)
    padded = ((counts + tm - 1) // tm) * tm
    ends = jnp.cumsum(padded)
    starts = ends - padded
    dest = jnp.sum(onehot * starts[None, :], axis=1) + rank
    p_rows = ((npair + tm - 1) // tm + N_EXPERTS) * tm
    pair_token = jnp.arange(npair, dtype=jnp.int32) % m
    src = jnp.zeros((p_rows,), jnp.int32).at[dest].set(pair_token, mode="promise_in_bounds",
                                                         unique_indices=True)
    n_tiles = p_rows // tm
    tile_start = jnp.arange(n_tiles, dtype=jnp.int32) * tm
    n_used = (ends[-1] // tm).astype(jnp.int32).reshape(1)
    last_used = jnp.maximum(n_used[0] - 1, 0) * tm
    tile_expert = jnp.sum((ends[None, :] <= jnp.minimum(tile_start, last_used)[:, None]).astype(jnp.int32), axis=1)
    tile_expert = jnp.minimum(tile_expert, N_EXPERTS - 1)
    tile_first = jnp.concatenate([jnp.ones((1,), jnp.int32),
                                  (tile_expert[1:] != tile_expert[:-1]).astype(jnp.int32)])
    seg = d // 2 // LANES
    xs = gather_rows(xm, src, tm, seg)
    ys = expert_ffn(xs, tile_expert, tile_first, n_used, w_gate, w_up, w_down, layer, tm)
    yp = gather_rows(ys, dest.astype(jnp.int32), tm, seg)
    return combine(h, yp, wt, g_next, norm_dtype, emit_h)


def kernel(x, meta_tokens, final_norm_g, norm1_g, w_in, ret_gn_g, lru_conv_w, lru_conv_b, lru_wa, lru_ba,
           lru_wx, lru_bx, lru_lambda, rwkv_mu, rwkv_w0, rwkv_w2, rwkv_a0, rwkv_a2, rwkv_g2, rwkv_k_k,
           rwkv_k_a, rwkv_r_k, rwkv_ln_g, rwkv_ln_b, w_branch, w_out, norm2_g, moe_wg, moe_bg, moe_we,
           moe_be, moe_w_gate, moe_w_up, moe_w_down):
    b, seq, d = x.shape
    depth = w_in.shape[0]
    w = ret_gn_g.shape[1]
    t = N_META + seq
    tp = ((t + CHUNK - 1) // CHUNK) * CHUNK
    m = b * tp
    rwkv_cols = rwkv_mu.shape[1]
    rwkv_pad = 3 * w + LANES + 2 * LANES
    c_ret, c_lru, c_rwkv = 4 * w, 6 * w, 6 * w + rwkv_cols

    meta = jnp.broadcast_to(meta_tokens.astype(x.dtype)[None], (b, N_META, d))
    h = jnp.concatenate([meta, x, jnp.zeros((b, tp - t, d), x.dtype)], axis=1).reshape(m, d)

    a = rmsnorm(h, norm1_g[0], BF16)
    for l in range(depth):
        wl = w_in[l]
        w_ret = wl[:, :c_ret].astype(BF16)
        w_lru = wl[:, c_ret:c_lru].astype(BF16)
        w_rwkv = jnp.pad(wl[:, c_lru:c_rwkv], ((0, 0), (0, rwkv_pad - rwkv_cols))).astype(BF16)
        w_gates = wl[:, c_rwkv:].astype(BF16)

        p_ret = matmul(a, w_ret, BF16, tm=ROW_TILE, tn=1024, name="proj_ret")
        p_lru = matmul(a, w_lru, F32, tm=ROW_TILE, tn=1024, name="proj_lru")
        p_rwkv = matmul(a, w_rwkv, F32, tm=ROW_TILE, tn=rwkv_pad // 3, name="proj_rwkv")
        gates = matmul(a, w_gates, BF16, tm=ROW_TILE, tn=1024, act="sigmoid", name="proj_gates")

        o_ret = retention(p_ret.reshape(b, tp, 4 * w), ret_gn_g[l])
        o_lru = rg_lru(p_lru.reshape(b, tp, 2 * w), lru_conv_w[l], lru_conv_b[l], lru_wa[l], lru_ba[l],
                       lru_wx[l], lru_bx[l], lru_lambda[l])
        o_rwkv = rwkv_mix(p_rwkv.reshape(b, tp, rwkv_pad), rwkv_mu[l], rwkv_w0[l], rwkv_w2[l], rwkv_a0[l],
                          rwkv_a2[l], rwkv_g2[l], rwkv_k_k[l], rwkv_k_a[l], rwkv_r_k[l], rwkv_ln_g[l],
                          rwkv_ln_b[l], w)

        merged = merge(o_ret.reshape(m, w), o_lru.reshape(m, w), o_rwkv.reshape(m, w),
                       w_branch[l].astype(BF16), gates)
        h = matmul(merged, w_out[l].astype(BF16), F32, tm=ROW_TILE, tn=1024, residual=h, name="proj_out")

        moe_args = (h, norm2_g[l], moe_wg[l], moe_bg[l], moe_we[l], moe_be[l], moe_w_gate, moe_w_up, moe_w_down, l)
        if l + 1 < depth:
            h, a = hier_moe(*moe_args, norm1_g[l + 1], BF16, True)
        else:
            (out,) = hier_moe(*moe_args, final_norm_g, F32, False)

    return out.reshape(b, tp, d)[:, N_META:t]
```

```python
import functools
import math

import jax
import jax.numpy as jnp
from jax import lax
from jax.experimental import pallas as pl
from jax.experimental.pallas import tpu as pltpu

F32 = jnp.float32
BF16 = jnp.bfloat16
HIGHEST = lax.Precision.HIGHEST

LANES = 128
SUBLANES = 8
VMEM_BYTES_V7X = 64 * 1024 * 1024
VMEM_LIMIT = VMEM_BYTES_V7X - 8 * 1024 * 1024

N_META = 16
CHUNK = 128
RET_DIM = 128
ROPE_BASE = 10000.0
RET_EPS = 1e-5
LRU_BLOCK = 128
CONV_W = 4
LRU_C = 8.0
RWKV_DIM = 64
DECAY_LORA = 64
AAA_LORA = 64
GATE_LORA = 160
RWKV_EPS = 64e-5
N_GROUPS = 4
EXPERTS_PER_GROUP = 8
N_EXPERTS = N_GROUPS * EXPERTS_PER_GROUP
TOP_K = 2
RMS_EPS = 1e-6

RWKV_CHUNK = 64
RWKV_TBLK = 128
ROW_TILE = 512
POINT_TILE = 256
MOE_TILE = 512


def _cparams(*sem):
    return pltpu.CompilerParams(dimension_semantics=sem, vmem_limit_bytes=VMEM_LIMIT)


def _dot(a, b, precision=None):
    return jnp.dot(a, b, preferred_element_type=F32, precision=precision)


def _dot_nt(a, b, precision=None):
    return lax.dot_general(a, b, (((1,), (1,)), ((), ())), preferred_element_type=F32, precision=precision)


def _iota(shape, dim):
    return lax.broadcasted_iota(jnp.int32, shape, dim)


def _blk(i, n):
    shift = n.bit_length() - 1
    assert 1 << shift == n
    return lax.shift_right_logical(i, jnp.int32(shift))


def _rmsnorm_kernel(x_ref, g_ref, o_ref):
    x = x_ref[...]
    y = x * lax.rsqrt(jnp.mean(x * x, axis=-1, keepdims=True) + RMS_EPS)
    o_ref[...] = (y * g_ref[...]).astype(o_ref.dtype)


def rmsnorm(x, g, out_dtype, tm=POINT_TILE):
    m, d = x.shape
    return pl.pallas_call(
        _rmsnorm_kernel,
        out_shape=jax.ShapeDtypeStruct((m, d), out_dtype),
        grid=(m // tm,),
        in_specs=[pl.BlockSpec((tm, d), lambda i: (i, 0)), pl.BlockSpec((1, d), lambda i: (0, 0))],
        out_specs=pl.BlockSpec((tm, d), lambda i: (i, 0)),
        compiler_params=_cparams("parallel"),
        name="rmsnorm",
    )(x, g.reshape(1, d))


def _matmul_kernel(a_ref, w_ref, o_ref, *, act):
    acc = _dot(a_ref[...], w_ref[...])
    if act == "sigmoid":
        acc = jax.nn.sigmoid(acc)
    o_ref[...] = acc.astype(o_ref.dtype)


def _matmul_res_kernel(a_ref, w_ref, r_ref, o_ref):
    o_ref[...] = r_ref[...] + _dot(a_ref[...], w_ref[...])


def matmul(a, w, out_dtype, *, tm, tn, act=None, residual=None, name="matmul"):
    m, k = a.shape
    n = w.shape[1]
    assert m % tm == 0 and n % tn == 0, (m, n, tm, tn)
    in_specs = [pl.BlockSpec((tm, k), lambda i, j: (i, 0)), pl.BlockSpec((k, tn), lambda i, j: (0, j))]
    args = [a, w]
    if residual is None:
        body = functools.partial(_matmul_kernel, act=act)
    else:
        body = _matmul_res_kernel
        in_specs.append(pl.BlockSpec((tm, tn), lambda i, j: (i, j)))
        args.append(residual)
    return pl.pallas_call(
        body,
        out_shape=jax.ShapeDtypeStruct((m, n), out_dtype),
        grid=(m // tm, n // tn),
        in_specs=in_specs,
        out_specs=pl.BlockSpec((tm, tn), lambda i, j: (i, j)),
        compiler_params=_cparams("parallel", "arbitrary"),
        name=name,
    )(*args)


def _retention_kernel(q_ref, k_ref, v_ref, g_ref, cos_ref, sin_ref, gn_ref, o_ref, state_ref, *, heads):
    c = pl.program_id(1)

    @pl.when(c == 0)
    def _():
        state_ref[...] = jnp.zeros_like(state_ref)

    d = RET_DIM
    cos = cos_ref[...]
    sin = sin_ref[...]
    row = _iota((CHUNK, CHUNK), 0)
    col = _iota((CHUNK, CHUNK), 1)
    diff = (row - col).astype(F32)
    causal = row >= col
    idx = _iota((CHUNK, 1), 0).astype(F32)
    for h in range(heads):
        lg = math.log1p(-(2.0 ** (-5.0 - h)))
        sl = slice(h * d, (h + 1) * d)
        q = q_ref[0, :, sl].astype(F32)
        k = k_ref[0, :, sl].astype(F32)
        v = v_ref[0, :, sl].astype(BF16)
        g = g_ref[0, :, sl].astype(F32)
        qr = q * cos + pltpu.roll(q, d // 2, 1) * sin
        kr = (k * cos + pltpu.roll(k, d // 2, 1) * sin) * (d ** -0.5)
        decay = jnp.where(causal, jnp.exp(lg * jnp.maximum(diff, 0.0)), 0.0)
        scores = _dot_nt(qr.astype(BF16), kr.astype(BF16)) * decay
        o = _dot(scores.astype(BF16), v)
        xi = jnp.exp(lg * (idx + 1.0))
        st = state_ref[h]
        o = o + _dot((qr * xi).astype(BF16), st.astype(BF16))
        zeta = jnp.exp(lg * (CHUNK - 1.0 - idx))
        kv = _dot((kr * zeta).T.astype(BF16), v)
        state_ref[h] = math.exp(lg * CHUNK) * st + kv
        mu = jnp.mean(o, axis=-1, keepdims=True)
        oc = o - mu
        var = jnp.mean(oc * oc, axis=-1, keepdims=True)
        on = oc * lax.rsqrt(var + RET_EPS) * gn_ref[:, sl]
        o_ref[0, :, sl] = (g * jax.nn.sigmoid(g) * on).astype(o_ref.dtype)


def _rope_tables(tp):
    half = RET_DIM // 2
    inv_freq = ROPE_BASE ** (-jnp.arange(half, dtype=F32) / half)
    ang = jnp.arange(tp, dtype=F32)[:, None] * inv_freq[None, :]
    cos, sin = jnp.cos(ang), jnp.sin(ang)
    return jnp.concatenate([cos, cos], axis=-1), jnp.concatenate([-sin, sin], axis=-1)


def retention(pr, gn_g):
    b, tp, w4 = pr.shape
    w = w4 // 4
    heads = w // RET_DIM
    cos, sin = _rope_tables(tp)
    spec = lambda j: pl.BlockSpec((1, CHUNK, w), lambda bi, c, j=j: (bi, c, j))
    return pl.pallas_call(
        functools.partial(_retention_kernel, heads=heads),
        out_shape=jax.ShapeDtypeStruct((b, tp, w), BF16),
        grid=(b, tp // CHUNK),
        in_specs=[spec(0), spec(1), spec(2), spec(3),
                  pl.BlockSpec((CHUNK, RET_DIM), lambda bi, c: (c, 0)),
                  pl.BlockSpec((CHUNK, RET_DIM), lambda bi, c: (c, 0)),
                  pl.BlockSpec((1, w), lambda bi, c: (0, 0))],
        out_specs=pl.BlockSpec((1, CHUNK, w), lambda bi, c: (bi, c, 0)),
        scratch_shapes=[pltpu.VMEM((heads, RET_DIM, RET_DIM), F32)],
        compiler_params=_cparams("parallel", "arbitrary"),
        name="retention",
    )(pr, pr, pr, pr, cos, sin, gn_g.reshape(1, w))


def _lru_kernel(x_ref, y_ref, cw_ref, cb_ref, wa_ref, ba_ref, wx_ref, bx_ref, lam_ref, o_ref,
                xs_ref, h_ref, *, tc):
    c = pl.program_id(1)
    w = x_ref.shape[-1]

    @pl.when(c == 0)
    def _():
        xs_ref[0:SUBLANES, :] = jnp.zeros((SUBLANES, w), F32)
        h_ref[...] = jnp.zeros_like(h_ref)

    x = x_ref[0].astype(F32)
    xs_ref[SUBLANES:, :] = x
    xc = cb_ref[...] + cw_ref[CONV_W - 1:CONV_W, :] * x
    for j in range(CONV_W - 1):
        off = SUBLANES - (CONV_W - 1) + j
        xc = xc + cw_ref[j:j + 1, :] * xs_ref[off:off + tc, :]
    xs_ref[0:SUBLANES, :] = x[tc - SUBLANES:, :]

    xcb = xc.astype(BF16)
    nb = w // LRU_BLOCK
    ra = jnp.concatenate(
        [_dot(xcb[:, n * LRU_BLOCK:(n + 1) * LRU_BLOCK], wa_ref[n]) for n in range(nb)], axis=-1)
    ri = jnp.concatenate(
        [_dot(xcb[:, n * LRU_BLOCK:(n + 1) * LRU_BLOCK], wx_ref[n]) for n in range(nb)], axis=-1)
    r = jax.nn.sigmoid(ra + ba_ref[...])
    i = jax.nn.sigmoid(ri + bx_ref[...])
    nlam = -lam_ref[...]
    softplus = jnp.maximum(nlam, 0.0) + jnp.log1p(jnp.exp(-jnp.abs(nlam)))
    log_a = -LRU_C * r * softplus
    a = jnp.exp(log_a)
    bb = jnp.sqrt(1.0 - jnp.exp(2.0 * log_a)) * (i * xc)

    row = _iota((tc, w), 0)
    s = 1
    while s < tc:
        a_sh = pltpu.roll(a, s, 0)
        b_sh = pltpu.roll(bb, s, 0)
        m = row >= s
        bb = jnp.where(m, a * b_sh + bb, bb)
        a = jnp.where(m, a * a_sh, a)
        s *= 2
    h = a * h_ref[...] + bb
    h_ref[...] = h[tc - 1:tc, :]
    y = y_ref[0].astype(F32)
    o_ref[0] = (h * jax.nn.gelu(y)).astype(o_ref.dtype)


def rg_lru(pl_xy, conv_w, conv_b, wa, ba, wx, bx, lam, tc=256):
    b, tp, w2 = pl_xy.shape
    w = w2 // 2
    if tp % tc:
        tc = CHUNK
    nb = w // LRU_BLOCK
    vec = lambda: pl.BlockSpec((1, w), lambda bi, c: (0, 0))
    return pl.pallas_call(
        functools.partial(_lru_kernel, tc=tc),
        out_shape=jax.ShapeDtypeStruct((b, tp, w), BF16),
        grid=(b, tp // tc),
        in_specs=[pl.BlockSpec((1, tc, w), lambda bi, c: (bi, c, 0)),
                  pl.BlockSpec((1, tc, w), lambda bi, c: (bi, c, 1)),
                  pl.BlockSpec((CONV_W, w), lambda bi, c: (0, 0)), vec(),
                  pl.BlockSpec((nb, LRU_BLOCK, LRU_BLOCK), lambda bi, c: (0, 0, 0)), vec(),
                  pl.BlockSpec((nb, LRU_BLOCK, LRU_BLOCK), lambda bi, c: (0, 0, 0)), vec(), vec()],
        out_specs=pl.BlockSpec((1, tc, w), lambda bi, c: (bi, c, 0)),
        scratch_shapes=[pltpu.VMEM((tc + SUBLANES, w), F32), pltpu.VMEM((1, w), F32)],
        compiler_params=_cparams("parallel", "arbitrary"),
        name="rg_lru",
    )(pl_xy, pl_xy, conv_w, conv_b.reshape(1, w), wa.astype(BF16), ba.reshape(1, w),
      wx.astype(BF16), bx.reshape(1, w), lam.reshape(1, w))


def _split_bf16(x):
    hi = x.astype(BF16)
    return hi, (x - hi.astype(F32)).astype(BF16)


def _mm_exact_rhs(a, b):
    ah, al = _split_bf16(a)
    b = b.astype(BF16)
    return _dot(ah, b) + _dot(al, b)


def _mm3(a, b):
    ah, al = _split_bf16(a)
    bh, bl = _split_bf16(b)
    return _dot(ah, bh) + (_dot(al, bh) + _dot(ah, bl))


def _bmm(a, b):
    return jnp.einsum("pmk,pkn->pmn", a.astype(BF16), b.astype(BF16), preferred_element_type=F32)


def _bmm_nt(a, b):
    return jnp.einsum("pmk,pnk->pmn", a.astype(BF16), b.astype(BF16), preferred_element_type=F32)


def _bt(x):
    return jnp.stack([x[p].T for p in range(x.shape[0])], axis=0)


def _rwkv_scan_kernel(pr_ref, pk_ref, pv_ref, pl_ref, qr_ref, qk_ref, qv_ref, ql_ref,
                      mur_ref, muk_ref, muv_ref, mul_ref, w0_ref, w2_ref, a0_ref, a2_ref, g2_ref,
                      kk_ref, ka_ref, rk_ref, lng_ref, lnb_ref, o_ref, s_ref, *, npair):
    n = pl.program_id(2)

    @pl.when(n == 0)
    def _():
        s_ref[...] = jnp.zeros_like(s_ref)

    P = npair
    tb = pr_ref.shape[1]
    C = RWKV_CHUNK
    nch = tb // C
    hd = RWKV_DIM

    def seq(x):
        return jnp.stack([x[:, p * LANES:(p + 1) * LANES] for p in range(P)], axis=0)

    def shifted(p_ref, q_ref, mu_ref):
        x = p_ref[0]
        prev = jnp.where(n == 0, 0.0, q_ref[0][SUBLANES - 1:SUBLANES, :])
        x_prev = jnp.where(_iota(x.shape, 0) == 0, prev, pltpu.roll(x, 1, 0))
        return x + (x_prev - x) * mu_ref[...]

    def vec(ref):
        x = ref[...]
        return jnp.stack([x[:, p * LANES:(p + 1) * LANES] for p in range(P)], axis=0)

    lane = _iota((1, 1, LANES), 2)
    hmask = [(lane < hd).astype(F32), (lane >= hd).astype(F32)]
    li = _iota((LANES, LANES), 0)
    lj = _iota((LANES, LANES), 1)
    same_head = ((li >= hd) == (lj >= hd)).astype(F32)
    eye = (li == lj).astype(F32)

    def head_sum(x, scale=1.0):
        return _mm_exact_rhs(x.reshape(P * tb, LANES), same_head * scale).reshape(P, tb, LANES)

    r, k, v = (seq(shifted(p, q, mu)) for p, q, mu in
               ((pr_ref, qr_ref, mur_ref), (pk_ref, qk_ref, muk_ref), (pv_ref, qv_ref, muv_ref)))
    lora = shifted(pl_ref, ql_ref, mul_ref)
    xwa = lora[:, :LANES]
    xg = lora[:, LANES:]
    nw = -(w0_ref[...] + _mm3(jnp.tanh(xwa), w2_ref[...]))
    softplus = jnp.maximum(nw, 0.0) + jnp.log1p(jnp.exp(-jnp.abs(nw)))
    lw = seq(-jnp.exp(-softplus - 0.5))
    a = seq(jax.nn.sigmoid(a0_ref[...] + _mm3(xwa, a2_ref[...])))
    g = seq(_mm3(jax.nn.sigmoid(xg), g2_ref[...]))

    kkr = k * vec(kk_ref)
    kk = kkr * lax.rsqrt(jnp.maximum(head_sum(kkr * kkr), 1e-24))
    kmod = k * (1.0 + (a - 1.0) * vec(ka_ref))
    be = kk * a

    pos = _iota((P * tb, LANES), 0) & (C - 1)
    cw2 = lw.reshape(P * tb, LANES)
    sh = 1
    while sh < C:
        cw2 = cw2 + jnp.where(pos >= sh, pltpu.roll(cw2, sh, 0), 0.0)
        sh *= 2
    cw = cw2.reshape(P, tb, LANES)
    cl = jnp.concatenate(
        [jnp.broadcast_to(cw[:, (c + 1) * C - 1:(c + 1) * C, :], (P, C, LANES)) for c in range(nch)], axis=1)
    e_out = jnp.exp(-cw)
    e_rem = jnp.exp(cl - cw)
    rt = r * jnp.exp(cw)
    at = -kk * jnp.exp(cw - lw)
    kt = kmod * e_out
    bt = be * e_out
    kh = kmod * e_rem
    bh = be * e_rem

    def stack_masked(x):
        return jnp.concatenate(
            [x[:, c * C:(c + 1) * C] * hmask[h] for c in range(nch) for h in range(2)], axis=1)

    def stack_dup(x):
        return jnp.concatenate([x[:, c * C:(c + 1) * C] for c in range(nch) for h in range(2)], axis=1)

    sr = 2 * nch * C
    at_s = stack_masked(at)
    rt_s = stack_masked(rt)
    v_s = stack_masked(v)
    gram = _bmm_nt(jnp.concatenate([at_s, rt_s], axis=1),
                   jnp.concatenate([stack_dup(bt), stack_dup(kt)], axis=1))
    ui = _iota((sr, sr), 0)
    uj = _iota((sr, sr), 1)
    unit = _blk(ui, C) == _blk(uj, C)
    strict = unit & (ui > uj)
    incl = unit & (ui >= uj)
    a_ab = jnp.where(strict, gram[:, :sr, :sr], 0.0)
    a_ak = jnp.where(strict, gram[:, :sr, sr:], 0.0)
    a_rb = jnp.where(incl, gram[:, sr:, :sr], 0.0)
    a_rk = jnp.where(incl, gram[:, sr:, sr:], 0.0)

    x = (ui == uj).astype(F32) + jnp.where(_blk(ui, 2) == _blk(uj, 2), a_ab, 0.0)
    nblk = 2
    while nblk < C:
        e = jnp.where((_blk(ui, 2 * nblk) == _blk(uj, 2 * nblk)) & (_blk(ui, nblk) != _blk(uj, nblk)), a_ab, 0.0)
        x = x + _bmm(_bmm(x, e), x)
        nblk *= 2

    ah = _bmm(x, at_s)
    ul = _bmm(x, _bmm(a_ak, v_s))
    rh = rt_s + _bmm(a_rb, ah)
    yl = _bmm(a_rb, ul) + _bmm(a_rk, v_s)
    bh_s = stack_masked(bh)
    kh_s = stack_masked(kh)

    s = s_ref[...]
    ys = []
    for c in range(nch):
        u0 = slice(2 * c * C, (2 * c + 1) * C)
        u1 = slice((2 * c + 1) * C, (2 * c + 2) * C)
        uc = slice(2 * c * C, (2 * c + 2) * C)
        ys.append(_bmm(rh[:, u0] + rh[:, u1], s) + yl[:, u0] + yl[:, u1])
        wc = jnp.exp(cl[:, c * C:c * C + 1, :])
        bht = _bt(bh_s[:, uc])
        pm = eye * wc + _bmm(bht, ah[:, uc])
        z = _bmm(bht, ul[:, uc]) + _bmm(_bt(kh_s[:, uc]), v_s[:, uc])
        s = _bmm(pm, s) + z
    s_ref[...] = s
    y = jnp.concatenate(ys, axis=1)

    mu = head_sum(y, 1.0 / hd)
    yc = y - mu
    var = head_sum(yc * yc, 1.0 / hd)
    yn = yc * lax.rsqrt(var + RWKV_EPS) * vec(lng_ref) + vec(lnb_ref)
    bonus = head_sum(r * kmod * vec(rk_ref)) * v
    out = (yn + bonus) * g
    o_ref[0] = jnp.concatenate([out[p] for p in range(P)], axis=-1).astype(o_ref.dtype)


def rwkv_mix(pp, mu, w0, w2, a0, a2, g2, k_k, k_a, r_k, ln_g, ln_b, w, npair=4):
    b, tp, n = pp.shape
    tb = RWKV_TBLK
    wl = npair * LANES
    nl = n - 3 * w
    assert w % wl == 0 and (3 * w) % nl == 0 and nl == 3 * LANES
    nj = w // wl
    lblk = 3 * w // nl
    rows8 = tb // SUBLANES
    mu_p = jnp.zeros((1, n), F32).at[0, :mu.shape[0]].set(mu)
    w2_p = jnp.zeros((LANES, w), F32).at[:DECAY_LORA].set(w2)
    a2_p = jnp.zeros((LANES, w), F32).at[DECAY_LORA:DECAY_LORA + AAA_LORA].set(a2)
    g2_p = jnp.zeros((nl - LANES, w), F32).at[:GATE_LORA].set(g2)

    def cur(width, col):
        return pl.BlockSpec((1, tb, width), lambda bi, j, t: (bi, t, col(j)))

    def prev(width, col):
        return pl.BlockSpec((1, SUBLANES, width), lambda bi, j, t: (bi, jnp.maximum(t * rows8 - 1, 0), col(j)))

    def mu_spec(width, col):
        return pl.BlockSpec((1, width), lambda bi, j, t: (0, col(j)))

    cols = [lambda j: j, lambda j: nj + j, lambda j: 2 * nj + j]
    lcol = lambda j: lblk
    vec = pl.BlockSpec((1, wl), lambda bi, j, t: (0, j))
    mat = lambda rows: pl.BlockSpec((rows, wl), lambda bi, j, t: (0, j))
    in_specs = ([cur(wl, c) for c in cols] + [cur(nl, lcol)] + [prev(wl, c) for c in cols] + [prev(nl, lcol)]
                + [mu_spec(wl, c) for c in cols] + [mu_spec(nl, lcol)]
                + [vec, mat(LANES), vec, mat(LANES), mat(nl - LANES)] + [vec] * 5)
    return pl.pallas_call(
        functools.partial(_rwkv_scan_kernel, npair=npair),
        out_shape=jax.ShapeDtypeStruct((b, tp, w), BF16),
        grid=(b, nj, tp // tb),
        in_specs=in_specs,
        out_specs=pl.BlockSpec((1, tb, wl), lambda bi, j, t: (bi, t, j)),
        scratch_shapes=[pltpu.VMEM((npair, LANES, LANES), F32)],
        compiler_params=_cparams("parallel", "parallel", "arbitrary"),
        name="rwkv_mix",
    )(pp, pp, pp, pp, pp, pp, pp, pp, mu_p, mu_p, mu_p, mu_p,
      w0.reshape(1, w), w2_p, a0.reshape(1, w), a2_p, g2_p,
      k_k.reshape(1, w), k_a.reshape(1, w), r_k.reshape(1, w), ln_g.reshape(1, w), ln_b.reshape(1, w))


def _merge_kernel(o1_ref, o2_ref, o3_ref, wb_ref, g1_ref, g2_ref, g3_ref, out_ref):
    acc = g1_ref[...].astype(F32) * _dot(o1_ref[...], wb_ref[0])
    acc = acc + g2_ref[...].astype(F32) * _dot(o2_ref[...], wb_ref[1])
    acc = acc + g3_ref[...].astype(F32) * _dot(o3_ref[...], wb_ref[2])
    out_ref[...] = acc.astype(out_ref.dtype)


def merge(o1, o2, o3, w_branch, gates, tm=ROW_TILE, tn=1024):
    m, w = o1.shape
    d = w_branch.shape[2]
    nj = d // tn
    ospec = pl.BlockSpec((tm, w), lambda i, j: (i, 0))
    gspec = lambda b: pl.BlockSpec((tm, tn), lambda i, j, b=b: (i, b * nj + j))
    return pl.pallas_call(
        _merge_kernel,
        out_shape=jax.ShapeDtypeStruct((m, d), BF16),
        grid=(m // tm, nj),
        in_specs=[ospec, ospec, ospec, pl.BlockSpec((3, w, tn), lambda i, j: (0, 0, j)),
                  gspec(0), gspec(1), gspec(2)],
        out_specs=pl.BlockSpec((tm, tn), lambda i, j: (i, j)),
        compiler_params=_cparams("parallel", "arbitrary"),
        name="merge",
    )(o1, o2, o3, w_branch, gates, gates, gates)


def _pack_bf16_pair(hi, lo):
    uh = lax.bitcast_convert_type(hi.astype(BF16).astype(F32), jnp.uint32)
    ul = lax.bitcast_convert_type(lo.astype(BF16).astype(F32), jnp.uint32)
    return uh | lax.shift_right_logical(ul, jnp.uint32(16))


def _unpack_bf16_pair(w):
    hi = lax.bitcast_convert_type(w & jnp.uint32(0xFFFF0000), F32)
    lo = lax.bitcast_convert_type(lax.shift_left(w, jnp.uint32(16)), F32)
    return hi, lo


def _store_token_rows(ref, val):
    tm, n = val.shape
    seg = n // LANES
    for s in range(seg):
        ref[pl.ds(s, tm, stride=seg), :] = val[:, s * LANES:(s + 1) * LANES]


def _load_token_rows(ref, seg):
    tm = ref.shape[0] // seg
    return jnp.concatenate([ref[pl.ds(s, tm, stride=seg), :] for s in range(seg)], axis=-1)


def _start_token_gather(idx_ref, idx_base, n, src_hbm, dst_vmem, dst_base, sem, seg):
    def body(r, carry):
        t = pl.multiple_of(idx_ref[idx_base + r] * seg, seg)
        o = pl.multiple_of((dst_base + r) * seg, seg)
        pltpu.make_async_copy(src_hbm.at[pl.ds(t, seg), :], dst_vmem.at[pl.ds(o, seg), :], sem).start()
        return carry

    lax.fori_loop(0, n, body, 0, unroll=8)


def _wait_token_gather(n, src_hbm, dst_vmem, sem, seg):
    pltpu.make_async_copy(src_hbm.at[pl.ds(0, n * seg), :], dst_vmem.at[pl.ds(0, n * seg), :], sem).wait()


def _router_kernel(h_ref, g_ref, wr_ref, br_ref, m_ref, eid_ref, wt_ref):
    x = h_ref[...]
    m = x * lax.rsqrt(jnp.mean(x * x, axis=-1, keepdims=True) + RMS_EPS) * g_ref[...]
    half = m.shape[1] // 2
    _store_token_rows(m_ref, _pack_bf16_pair(m[:, :half], m[:, half:]))
    logits = _dot(m, wr_ref[...], HIGHEST) + br_ref[...]
    lane = _iota(logits.shape, 1)
    lanef = lane.astype(F32)
    neg = -jnp.inf
    big = float(LANES)
    glog = jnp.where(lane < N_GROUPS, logits, neg)
    gmax = jnp.max(glog, axis=-1, keepdims=True)
    gsel = jnp.min(jnp.where(glog == gmax, lanef, big), axis=-1, keepdims=True)
    pg = 1.0 / jnp.sum(jnp.exp(glog - gmax), axis=-1, keepdims=True)
    lo = N_GROUPS + gsel * EXPERTS_PER_GROUP
    el = jnp.where((lanef >= lo) & (lanef < lo + EXPERTS_PER_GROUP), logits, neg)
    v1 = jnp.max(el, axis=-1, keepdims=True)
    i1 = jnp.min(jnp.where(el == v1, lanef, big), axis=-1, keepdims=True)
    el2 = jnp.where(lanef == i1, neg, el)
    v2 = jnp.max(el2, axis=-1, keepdims=True)
    i2 = jnp.min(jnp.where(el2 == v2, lanef, big), axis=-1, keepdims=True)
    e2 = jnp.exp(v2 - v1)
    w1 = pg / (1.0 + e2)
    w2 = pg * e2 / (1.0 + e2)
    eid = jnp.where(lane == 0, i1 - N_GROUPS, jnp.where(lane == 1, i2 - N_GROUPS, 0.0))
    eid_ref[...] = eid.astype(jnp.int32)
    wt_ref[...] = jnp.where(lane == 0, w1, jnp.where(lane == 1, w2, 0.0))


def router(h, g, wg, bg, we, be, tm=POINT_TILE):
    m, d = h.shape
    nr = N_GROUPS + N_EXPERTS
    wr = jnp.zeros((d, LANES), F32).at[:, :N_GROUPS].set(wg).at[:, N_GROUPS:nr].set(we)
    br = jnp.zeros((1, LANES), F32).at[0, :N_GROUPS].set(bg).at[0, N_GROUPS:nr].set(be)
    row = pl.BlockSpec((tm, d), lambda i: (i, 0))
    slab = pl.BlockSpec((tm, LANES), lambda i: (i, 0))
    seg = d // 2 // LANES
    return pl.pallas_call(
        _router_kernel,
        out_shape=[jax.ShapeDtypeStruct((m * seg, LANES), jnp.uint32), jax.ShapeDtypeStruct((m, LANES), jnp.int32),
                   jax.ShapeDtypeStruct((m, LANES), F32)],
        grid=(m // tm,),
        in_specs=[row, pl.BlockSpec((1, d), lambda i: (0, 0)), pl.BlockSpec((d, LANES), lambda i: (0, 0)),
                  pl.BlockSpec((1, LANES), lambda i: (0, 0))],
        out_specs=[pl.BlockSpec((tm * seg, LANES), lambda i: (i, 0)), slab, slab],
        compiler_params=_cparams("parallel"),
        name="moe_router",
    )(h, g.reshape(1, d), wr, br)


def _expert_kernel(te_ref, first_ref, nu_ref, src_ref, x_hbm, wg_ref, wu_ref, wd_ref, y_ref,
                   wgu_s, wd_s, xbuf, sem, *, nk, dk, tm):
    i = pl.program_id(0)
    kc = pl.program_id(1)
    nu = nu_ref[0]
    used = i < nu
    de = nk * dk
    seg = wd_s.shape[1] // 2 // LANES
    slot = lax.rem(i, 2)

    def start_gather(tile, s):
        _start_token_gather(src_ref, tile * tm, tm, x_hbm, xbuf.at[s], 0, sem.at[s], seg)

    @pl.when((i == 0) & (kc == 0) & used)
    def _():
        start_gather(0, 0)

    @pl.when((kc == nk - 1) & (i + 1 < nu))
    def _():
        start_gather(i + 1, 1 - slot)

    @pl.when(used & (first_ref[i] == 1))
    def _():
        for c in range(nk):
            @pl.when(kc == c)
            def _():
                wgu_s[:, c * dk:(c + 1) * dk] = wg_ref[0, 0].astype(BF16)
                wgu_s[:, de + c * dk:de + (c + 1) * dk] = wu_ref[0, 0].astype(BF16)
                wd_s[c * dk:(c + 1) * dk, :] = wd_ref[0, 0].astype(BF16)

    @pl.when(used & (kc == nk - 1))
    def _():
        _wait_token_gather(tm, x_hbm, xbuf.at[slot], sem.at[slot], seg)
        xh, xl = _unpack_bf16_pair(_load_token_rows(xbuf.at[slot], seg))
        x = jnp.concatenate([xh.astype(BF16), xl.astype(BF16)], axis=-1)
        gu = _dot(x, wgu_s[...])
        gt = gu[:, :de]
        act = (gt * jax.nn.sigmoid(gt) * gu[:, de:]).astype(BF16)
        y = _dot(act, wd_s[...])
        half = y.shape[1] // 2
        _store_token_rows(y_ref, _pack_bf16_pair(y[:, :half], y[:, half:]))

    @pl.when(jnp.logical_not(used) & (kc == nk - 1))
    def _():
        y_ref[...] = jnp.zeros_like(y_ref)


def expert_ffn(xm, src, tile_expert, tile_first, n_used, w_gate, w_up, w_down, layer, tm=MOE_TILE, nk=4):
    d = w_down.shape[3]
    seg = d // 2 // LANES
    p = src.shape[0]
    de = w_down.shape[2]
    dk = de // nk
    assert dk * nk == de and dk % LANES == 0

    def chunk(i, kc, first):
        return jnp.where(first[i] == 1, kc, nk - 1)

    grid_spec = pltpu.PrefetchScalarGridSpec(
        num_scalar_prefetch=4,
        grid=(p // tm, nk),
        in_specs=[pl.BlockSpec(memory_space=pl.ANY),
                  pl.BlockSpec((1, 1, d, dk), lambda i, kc, te, first, *_: (layer, te[i], 0, chunk(i, kc, first))),
                  pl.BlockSpec((1, 1, d, dk), lambda i, kc, te, first, *_: (layer, te[i], 0, chunk(i, kc, first))),
                  pl.BlockSpec((1, 1, dk, d), lambda i, kc, te, first, *_: (layer, te[i], chunk(i, kc, first), 0))],
        out_specs=pl.BlockSpec((tm * seg, LANES), lambda i, kc, *_: (i, 0)),
        scratch_shapes=[pltpu.VMEM((d, 2 * de), BF16), pltpu.VMEM((de, d), BF16),
                        pltpu.VMEM((2, tm * seg, LANES), jnp.uint32), pltpu.SemaphoreType.DMA((2,))],
    )
    return pl.pallas_call(
        functools.partial(_expert_kernel, nk=nk, dk=dk, tm=tm),
        out_shape=jax.ShapeDtypeStruct((p * seg, LANES), jnp.uint32),
        grid_spec=grid_spec,
        compiler_params=_cparams("arbitrary", "arbitrary"),
        name="moe_experts",
    )(tile_expert, tile_first, n_used, src, xm, w_gate, w_up, w_down)


def _combine_kernel(dest_ref, h_ref, ys_hbm, wt_ref, g_ref, *rest, emit_h, m):
    out_refs, (ybuf, sem) = rest[:-2], rest[-2:]
    n_ref = out_refs[-1]
    i = pl.program_id(0)
    slot = lax.rem(i, 2)
    tm, d = h_ref.shape
    half = d // 2
    seg = half // LANES

    def start_gather(step, s):
        for k in range(TOP_K):
            _start_token_gather(dest_ref, k * m + step * tm, tm, ys_hbm, ybuf.at[s], k * tm, sem.at[s], seg)

    @pl.when(i == 0)
    def _():
        start_gather(0, 0)

    @pl.when(i + 1 < pl.num_programs(0))
    def _():
        start_gather(i + 1, 1 - slot)

    _wait_token_gather(TOP_K * tm, ys_hbm, ybuf.at[slot], sem.at[slot], seg)
    wt = wt_ref[...]
    w0, w1 = wt[:, 0:1], wt[:, 1:2]
    yb = ybuf.at[slot]
    a0, b0 = _unpack_bf16_pair(_load_token_rows(yb.at[pl.ds(0, tm * seg), :], seg))
    a1, b1 = _unpack_bf16_pair(_load_token_rows(yb.at[pl.ds(tm * seg, tm * seg), :], seg))
    lo = h_ref[:, :half] + w0 * a0 + w1 * a1
    hi = h_ref[:, half:] + w0 * b0 + w1 * b1
    if emit_h:
        out_refs[0][:, :half] = lo
        out_refs[0][:, half:] = hi
    ms = (jnp.sum(lo * lo, axis=-1, keepdims=True) + jnp.sum(hi * hi, axis=-1, keepdims=True)) * (1.0 / d)
    inv = lax.rsqrt(ms + RMS_EPS)
    n_ref[:, :half] = (lo * inv * g_ref[:, :half]).astype(n_ref.dtype)
    n_ref[:, half:] = (hi * inv * g_ref[:, half:]).astype(n_ref.dtype)


def combine(h, ys, dest, wt, g_next, norm_dtype, emit_h, tm=POINT_TILE):
    m, d = h.shape
    seg = d // 2 // LANES
    row = pl.BlockSpec((tm, d), lambda i, dest: (i, 0))
    outs = [jax.ShapeDtypeStruct((m, d), F32)] * emit_h + [jax.ShapeDtypeStruct((m, d), norm_dtype)]
    grid_spec = pltpu.PrefetchScalarGridSpec(
        num_scalar_prefetch=1,
        grid=(m // tm,),
        in_specs=[row, pl.BlockSpec(memory_space=pl.ANY), pl.BlockSpec((tm, LANES), lambda i, dest: (i, 0)),
                  pl.BlockSpec((1, d), lambda i, dest: (0, 0))],
        out_specs=[row] * len(outs),
        scratch_shapes=[pltpu.VMEM((2, TOP_K * tm * seg, LANES), jnp.uint32), pltpu.SemaphoreType.DMA((2,))],
    )
    return pl.pallas_call(
        functools.partial(_combine_kernel, emit_h=emit_h, m=m),
        out_shape=outs,
        grid_spec=grid_spec,
        compiler_params=_cparams("arbitrary"),
        name="moe_combine",
    )(dest, h, ys, wt, g_next.reshape(1, d))


def hier_moe(h, norm_g, wg, bg, we, be, w_gate, w_up, w_down, layer, g_next, norm_dtype, emit_h, tm=MOE_TILE):
    m, d = h.shape
    xm, eid_slab, wt = router(h, norm_g, wg, bg, we, be)
    npair = m * TOP_K
    eid = eid_slab[:, :TOP_K].T.reshape(npair)
    onehot = (eid[:, None] == jnp.arange(N_EXPERTS, dtype=jnp.int32)[None, :]).astype(jnp.int32)
    counts = jnp.sum(onehot, axis=0)
    rank = jnp.sum((jnp.cumsum(onehot, axis=0) - 1) * onehot, axis=1)
    padded = ((counts + tm - 1) // tm) * tm
    ends = jnp.cumsum(padded)
    starts = ends - padded
    dest = jnp.sum(onehot * starts[None, :], axis=1) + rank
    p_rows = ((npair + tm - 1) // tm + N_EXPERTS) * tm
    pair_token = jnp.arange(npair, dtype=jnp.int32) % m
    src = jnp.zeros((p_rows,), jnp.int32).at[dest].set(pair_token, mode="promise_in_bounds",
                                                         unique_indices=True)
    n_tiles = p_rows // tm
    tile_start = jnp.arange(n_tiles, dtype=jnp.int32) * tm
    n_used = (ends[-1] // tm).astype(jnp.int32).reshape(1)
    last_used = jnp.maximum(n_used[0] - 1, 0) * tm
    tile_expert = jnp.sum((ends[None, :] <= jnp.minimum(tile_start, last_used)[:, None]).astype(jnp.int32), axis=1)
    tile_expert = jnp.minimum(tile_expert, N_EXPERTS - 1)
    tile_first = jnp.concatenate([jnp.ones((1,), jnp.int32),
                                  (tile_expert[1:] != tile_expert[:-1]).astype(jnp.int32)])
    ys = expert_ffn(xm, src, tile_expert, tile_first, n_used, w_gate, w_up, w_down, layer, tm)
    return combine(h, ys, dest.astype(jnp.int32), wt, g_next, norm_dtype, emit_h)


def kernel(x, meta_tokens, final_norm_g, norm1_g, w_in, ret_gn_g, lru_conv_w, lru_conv_b, lru_wa, lru_ba,
           lru_wx, lru_bx, lru_lambda, rwkv_mu, rwkv_w0, rwkv_w2, rwkv_a0, rwkv_a2, rwkv_g2, rwkv_k_k,
           rwkv_k_a, rwkv_r_k, rwkv_ln_g, rwkv_ln_b, w_branch, w_out, norm2_g, moe_wg, moe_bg, moe_we,
           moe_be, moe_w_gate, moe_w_up, moe_w_down):
    b, seq, d = x.shape
    depth = w_in.shape[0]
    w = ret_gn_g.shape[1]
    t = N_META + seq
    tp = ((t + CHUNK - 1) // CHUNK) * CHUNK
    m = b * tp
    rwkv_cols = rwkv_mu.shape[1]
    rwkv_pad = 3 * w + LANES + 2 * LANES
    c_ret, c_lru, c_rwkv = 4 * w, 6 * w, 6 * w + rwkv_cols

    meta = jnp.broadcast_to(meta_tokens.astype(x.dtype)[None], (b, N_META, d))
    h = jnp.concatenate([meta, x, jnp.zeros((b, tp - t, d), x.dtype)], axis=1).reshape(m, d)

    a = rmsnorm(h, norm1_g[0], BF16)
    for l in range(depth):
        wl = w_in[l]
        w_ret = wl[:, :c_ret].astype(BF16)
        w_lru = wl[:, c_ret:c_lru].astype(BF16)
        w_rwkv = jnp.pad(wl[:, c_lru:c_rwkv], ((0, 0), (0, rwkv_pad - rwkv_cols))).astype(BF16)
        w_gates = wl[:, c_rwkv:].astype(BF16)

        p_ret = matmul(a, w_ret, BF16, tm=ROW_TILE, tn=1024, name="proj_ret")
        p_lru = matmul(a, w_lru, F32, tm=ROW_TILE, tn=1024, name="proj_lru")
        p_rwkv = matmul(a, w_rwkv, F32, tm=ROW_TILE, tn=rwkv_pad // 3, name="proj_rwkv")
        gates = matmul(a, w_gates, BF16, tm=ROW_TILE, tn=1024, act="sigmoid", name="proj_gates")

        o_ret = retention(p_ret.reshape(b, tp, 4 * w), ret_gn_g[l])
        o_lru = rg_lru(p_lru.reshape(b, tp, 2 * w), lru_conv_w[l], lru_conv_b[l], lru_wa[l], lru_ba[l],
                       lru_wx[l], lru_bx[l], lru_lambda[l])
        o_rwkv = rwkv_mix(p_rwkv.reshape(b, tp, rwkv_pad), rwkv_mu[l], rwkv_w0[l], rwkv_w2[l], rwkv_a0[l],
                          rwkv_a2[l], rwkv_g2[l], rwkv_k_k[l], rwkv_k_a[l], rwkv_r_k[l], rwkv_ln_g[l],
                          rwkv_ln_b[l], w)

        merged = merge(o_ret.reshape(m, w), o_lru.reshape(m, w), o_rwkv.reshape(m, w),
                       w_branch[l].astype(BF16), gates)
        h = matmul(merged, w_out[l].astype(BF16), F32, tm=ROW_TILE, tn=1024, residual=h, name="proj_out")

        moe_args = (h, norm2_g[l], moe_wg[l], moe_bg[l], moe_we[l], moe_be[l], moe_w_gate, moe_w_up, moe_w_down, l)
        if l + 1 < depth:
            h, a = hier_moe(*moe_args, norm1_g[l + 1], BF16, True)
        else:
            (out,) = hier_moe(*moe_args, final_norm_g, F32, False)

    return out.reshape(b, tp, d)[:, N_META:t]
```

```python
import functools
import math

import jax
import jax.numpy as jnp
from jax import lax
from jax.experimental import pallas as pl
from jax.experimental.pallas import tpu as pltpu

F32 = jnp.float32
BF16 = jnp.bfloat16
FP8 = jnp.float8_e4m3fn
FP8_MAX = 448.0

LANES = 128
SUBLANES = 8
VMEM_BYTES_V7X = 64 * 1024 * 1024
VMEM_LIMIT = VMEM_BYTES_V7X - 8 * 1024 * 1024

N_META = 16
CHUNK = 128
RET_DIM = 128
ROPE_BASE = 10000.0
RET_EPS = 1e-5
LRU_BLOCK = 128
CONV_W = 4
LRU_C = 8.0
RWKV_DIM = 64
DECAY_LORA = 64
AAA_LORA = 64
GATE_LORA = 160
RWKV_EPS = 64e-5
N_GROUPS = 4
EXPERTS_PER_GROUP = 8
N_EXPERTS = N_GROUPS * EXPERTS_PER_GROUP
TOP_K = 2
RMS_EPS = 1e-6

RWKV_CHUNK = 64
RWKV_TBLK = 128
ROW_TILE = 512
POINT_TILE = 256
MOE_TILE = 512


def _cparams(*sem):
    return pltpu.CompilerParams(dimension_semantics=sem, vmem_limit_bytes=VMEM_LIMIT)


def _dot(a, b, precision=None):
    return jnp.dot(a, b, preferred_element_type=F32, precision=precision)


def _dot_nt(a, b, precision=None):
    return lax.dot_general(a, b, (((1,), (1,)), ((), ())), preferred_element_type=F32, precision=precision)


def _iota(shape, dim):
    return lax.broadcasted_iota(jnp.int32, shape, dim)


def _blk(i, n):
    shift = n.bit_length() - 1
    assert 1 << shift == n
    return lax.shift_right_logical(i, jnp.int32(shift))


def _rmsnorm_kernel(x_ref, g_ref, o_ref):
    x = x_ref[...]
    y = x * lax.rsqrt(jnp.mean(x * x, axis=-1, keepdims=True) + RMS_EPS)
    o_ref[...] = (y * g_ref[...]).astype(o_ref.dtype)


def rmsnorm(x, g, out_dtype, tm=POINT_TILE):
    m, d = x.shape
    return pl.pallas_call(
        _rmsnorm_kernel,
        out_shape=jax.ShapeDtypeStruct((m, d), out_dtype),
        grid=(m // tm,),
        in_specs=[pl.BlockSpec((tm, d), lambda i: (i, 0)), pl.BlockSpec((1, d), lambda i: (0, 0))],
        out_specs=pl.BlockSpec((tm, d), lambda i: (i, 0)),
        compiler_params=_cparams("parallel"),
        name="rmsnorm",
    )(x, g.reshape(1, d))


def _matmul_kernel(a_ref, w_ref, o_ref, *, act):
    acc = _dot(a_ref[...], w_ref[...])
    if act == "sigmoid":
        acc = jax.nn.sigmoid(acc)
    o_ref[...] = acc.astype(o_ref.dtype)


def _matmul_fp8_sigmoid_kernel(a_ref, w_ref, s_ref, o_ref):
    o_ref[...] = jax.nn.sigmoid(_dot(a_ref[...], w_ref[...]) * s_ref[...]).astype(o_ref.dtype)


def _fp8_quantize(x, axis):
    amax = jnp.max(jnp.abs(x), axis=axis, keepdims=True)
    scale = jnp.where(amax > 0, FP8_MAX / amax, 1.0)
    return (x * scale).astype(FP8), 1.0 / scale


def gate_matmul(a, w, *, tm, tn):
    m, k = a.shape
    n = w.shape[1]
    a8, inv_a = _fp8_quantize(a.astype(F32), None)
    w8, inv_w = _fp8_quantize(w, 0)
    return pl.pallas_call(
        _matmul_fp8_sigmoid_kernel,
        out_shape=jax.ShapeDtypeStruct((m, n), BF16),
        grid=(m // tm, n // tn),
        in_specs=[pl.BlockSpec((tm, k), lambda i, j: (i, 0)), pl.BlockSpec((k, tn), lambda i, j: (0, j)),
                  pl.BlockSpec((1, tn), lambda i, j: (0, j))],
        out_specs=pl.BlockSpec((tm, tn), lambda i, j: (i, j)),
        compiler_params=_cparams("parallel", "arbitrary"),
        name="proj_gates",
    )(a8, w8, (inv_a.reshape(1, 1) * inv_w).astype(F32))


def _matmul_res_kernel(a_ref, w_ref, r_ref, o_ref):
    o_ref[...] = r_ref[...] + _dot(a_ref[...], w_ref[...])


def matmul(a, w, out_dtype, *, tm, tn, act=None, residual=None, name="matmul"):
    m, k = a.shape
    n = w.shape[1]
    assert m % tm == 0 and n % tn == 0, (m, n, tm, tn)
    in_specs = [pl.BlockSpec((tm, k), lambda i, j: (i, 0)), pl.BlockSpec((k, tn), lambda i, j: (0, j))]
    args = [a, w]
    if residual is None:
        body = functools.partial(_matmul_kernel, act=act)
    else:
        body = _matmul_res_kernel
        in_specs.append(pl.BlockSpec((tm, tn), lambda i, j: (i, j)))
        args.append(residual)
    return pl.pallas_call(
        body,
        out_shape=jax.ShapeDtypeStruct((m, n), out_dtype),
        grid=(m // tm, n // tn),
        in_specs=in_specs,
        out_specs=pl.BlockSpec((tm, tn), lambda i, j: (i, j)),
        compiler_params=_cparams("parallel", "arbitrary"),
        name=name,
    )(*args)


def _retention_kernel(q_ref, k_ref, v_ref, g_ref, cos_ref, sin_ref, gn_ref, o_ref, state_ref, *, heads):
    c = pl.program_id(1)

    @pl.when(c == 0)
    def _():
        state_ref[...] = jnp.zeros_like(state_ref)

    d = RET_DIM
    cos = cos_ref[...]
    sin = sin_ref[...]
    row = _iota((CHUNK, CHUNK), 0)
    col = _iota((CHUNK, CHUNK), 1)
    diff = (row - col).astype(F32)
    causal = row >= col
    idx = _iota((CHUNK, 1), 0).astype(F32)
    for h in range(heads):
        lg = math.log1p(-(2.0 ** (-5.0 - h)))
        sl = slice(h * d, (h + 1) * d)
        q = q_ref[0, :, sl].astype(F32)
        k = k_ref[0, :, sl].astype(F32)
        v = v_ref[0, :, sl].astype(BF16)
        g = g_ref[0, :, sl].astype(F32)
        qr = q * cos + pltpu.roll(q, d // 2, 1) * sin
        kr = (k * cos + pltpu.roll(k, d // 2, 1) * sin) * (d ** -0.5)
        decay = jnp.where(causal, jnp.exp(lg * jnp.maximum(diff, 0.0)), 0.0)
        scores = _dot_nt(qr.astype(BF16), kr.astype(BF16)) * decay
        o = _dot(scores.astype(BF16), v)
        xi = jnp.exp(lg * (idx + 1.0))
        st = state_ref[h]
        o = o + _dot((qr * xi).astype(BF16), st.astype(BF16))
        zeta = jnp.exp(lg * (CHUNK - 1.0 - idx))
        kv = _dot((kr * zeta).T.astype(BF16), v)
        state_ref[h] = math.exp(lg * CHUNK) * st + kv
        mu = jnp.mean(o, axis=-1, keepdims=True)
        oc = o - mu
        var = jnp.mean(oc * oc, axis=-1, keepdims=True)
        on = oc * lax.rsqrt(var + RET_EPS) * gn_ref[:, sl]
        o_ref[0, :, sl] = (g * jax.nn.sigmoid(g) * on).astype(o_ref.dtype)


def _rope_tables(tp):
    half = RET_DIM // 2
    inv_freq = ROPE_BASE ** (-jnp.arange(half, dtype=F32) / half)
    ang = jnp.arange(tp, dtype=F32)[:, None] * inv_freq[None, :]
    cos, sin = jnp.cos(ang), jnp.sin(ang)
    return jnp.concatenate([cos, cos], axis=-1), jnp.concatenate([-sin, sin], axis=-1)


def retention(pr, gn_g):
    b, tp, w4 = pr.shape
    w = w4 // 4
    heads = w // RET_DIM
    cos, sin = _rope_tables(tp)
    spec = lambda j: pl.BlockSpec((1, CHUNK, w), lambda bi, c, j=j: (bi, c, j))
    return pl.pallas_call(
        functools.partial(_retention_kernel, heads=heads),
        out_shape=jax.ShapeDtypeStruct((b, tp, w), BF16),
        grid=(b, tp // CHUNK),
        in_specs=[spec(0), spec(1), spec(2), spec(3),
                  pl.BlockSpec((CHUNK, RET_DIM), lambda bi, c: (c, 0)),
                  pl.BlockSpec((CHUNK, RET_DIM), lambda bi, c: (c, 0)),
                  pl.BlockSpec((1, w), lambda bi, c: (0, 0))],
        out_specs=pl.BlockSpec((1, CHUNK, w), lambda bi, c: (bi, c, 0)),
        scratch_shapes=[pltpu.VMEM((heads, RET_DIM, RET_DIM), F32)],
        compiler_params=_cparams("parallel", "arbitrary"),
        name="retention",
    )(pr, pr, pr, pr, cos, sin, gn_g.reshape(1, w))


def _lru_kernel(x_ref, y_ref, cw_ref, cb_ref, wa_ref, ba_ref, wx_ref, bx_ref, lam_ref, o_ref,
                xs_ref, h_ref, *, tc):
    c = pl.program_id(1)
    w = x_ref.shape[-1]

    @pl.when(c == 0)
    def _():
        xs_ref[0:SUBLANES, :] = jnp.zeros((SUBLANES, w), F32)
        h_ref[...] = jnp.zeros_like(h_ref)

    x = x_ref[0].astype(F32)
    xs_ref[SUBLANES:, :] = x
    xc = cb_ref[...] + cw_ref[CONV_W - 1:CONV_W, :] * x
    for j in range(CONV_W - 1):
        off = SUBLANES - (CONV_W - 1) + j
        xc = xc + cw_ref[j:j + 1, :] * xs_ref[off:off + tc, :]
    xs_ref[0:SUBLANES, :] = x[tc - SUBLANES:, :]

    xcb = xc.astype(BF16)
    nb = w // LRU_BLOCK
    ra = jnp.concatenate(
        [_dot(xcb[:, n * LRU_BLOCK:(n + 1) * LRU_BLOCK], wa_ref[n]) for n in range(nb)], axis=-1)
    ri = jnp.concatenate(
        [_dot(xcb[:, n * LRU_BLOCK:(n + 1) * LRU_BLOCK], wx_ref[n]) for n in range(nb)], axis=-1)
    r = jax.nn.sigmoid(ra + ba_ref[...])
    i = jax.nn.sigmoid(ri + bx_ref[...])
    nlam = -lam_ref[...]
    softplus = jnp.maximum(nlam, 0.0) + jnp.log1p(jnp.exp(-jnp.abs(nlam)))
    log_a = -LRU_C * r * softplus
    a = jnp.exp(log_a)
    bb = jnp.sqrt(1.0 - jnp.exp(2.0 * log_a)) * (i * xc)

    row = _iota((tc, w), 0)
    s = 1
    while s < tc:
        a_sh = pltpu.roll(a, s, 0)
        b_sh = pltpu.roll(bb, s, 0)
        m = row >= s
        bb = jnp.where(m, a * b_sh + bb, bb)
        a = jnp.where(m, a * a_sh, a)
        s *= 2
    h = a * h_ref[...] + bb
    h_ref[...] = h[tc - 1:tc, :]
    y = y_ref[0].astype(F32)
    o_ref[0] = (h * jax.nn.gelu(y)).astype(o_ref.dtype)


def rg_lru(pl_xy, conv_w, conv_b, wa, ba, wx, bx, lam, tc=256):
    b, tp, w2 = pl_xy.shape
    w = w2 // 2
    if tp % tc:
        tc = CHUNK
    nb = w // LRU_BLOCK
    vec = lambda: pl.BlockSpec((1, w), lambda bi, c: (0, 0))
    return pl.pallas_call(
        functools.partial(_lru_kernel, tc=tc),
        out_shape=jax.ShapeDtypeStruct((b, tp, w), BF16),
        grid=(b, tp // tc),
        in_specs=[pl.BlockSpec((1, tc, w), lambda bi, c: (bi, c, 0)),
                  pl.BlockSpec((1, tc, w), lambda bi, c: (bi, c, 1)),
                  pl.BlockSpec((CONV_W, w), lambda bi, c: (0, 0)), vec(),
                  pl.BlockSpec((nb, LRU_BLOCK, LRU_BLOCK), lambda bi, c: (0, 0, 0)), vec(),
                  pl.BlockSpec((nb, LRU_BLOCK, LRU_BLOCK), lambda bi, c: (0, 0, 0)), vec(), vec()],
        out_specs=pl.BlockSpec((1, tc, w), lambda bi, c: (bi, c, 0)),
        scratch_shapes=[pltpu.VMEM((tc + SUBLANES, w), F32), pltpu.VMEM((1, w), F32)],
        compiler_params=_cparams("parallel", "arbitrary"),
        name="rg_lru",
    )(pl_xy, pl_xy, conv_w, conv_b.reshape(1, w), wa.astype(BF16), ba.reshape(1, w),
      wx.astype(BF16), bx.reshape(1, w), lam.reshape(1, w))


def _split_bf16(x):
    hi = x.astype(BF16)
    return hi, (x - hi.astype(F32)).astype(BF16)


def _mm3(a, b):
    ah, al = _split_bf16(a)
    bh, bl = _split_bf16(b)
    return _dot(ah, bh) + (_dot(al, bh) + _dot(ah, bl))


def _bmm(a, b):
    return jnp.einsum("pmk,pkn->pmn", a.astype(BF16), b.astype(BF16), preferred_element_type=F32)


def _bmm_nt(a, b):
    return jnp.einsum("pmk,pnk->pmn", a.astype(BF16), b.astype(BF16), preferred_element_type=F32)


def _bt(x):
    return jnp.stack([x[p].T for p in range(x.shape[0])], axis=0)


def _rwkv_scan_kernel(pr_ref, pk_ref, pv_ref, pl_ref, qr_ref, qk_ref, qv_ref, ql_ref,
                      mur_ref, muk_ref, muv_ref, mul_ref, w0_ref, w2_ref, a0_ref, a2_ref, g2_ref,
                      kk_ref, ka_ref, rk_ref, lng_ref, lnb_ref, o_ref, s_ref, *, npair):
    n = pl.program_id(2)

    @pl.when(n == 0)
    def _():
        s_ref[...] = jnp.zeros_like(s_ref)

    P = npair
    tb = pr_ref.shape[1]
    C = RWKV_CHUNK
    nch = tb // C
    hd = RWKV_DIM

    def seq(x):
        return jnp.stack([x[:, p * LANES:(p + 1) * LANES] for p in range(P)], axis=0)

    def shifted(p_ref, q_ref, mu_ref):
        x = p_ref[0]
        prev = jnp.where(n == 0, 0.0, q_ref[0][SUBLANES - 1:SUBLANES, :])
        x_prev = jnp.where(_iota(x.shape, 0) == 0, prev, pltpu.roll(x, 1, 0))
        return x + (x_prev - x) * mu_ref[...]

    def vec(ref):
        x = ref[...]
        return jnp.stack([x[:, p * LANES:(p + 1) * LANES] for p in range(P)], axis=0)

    lane = _iota((1, 1, LANES), 2)
    hmask = [(lane < hd).astype(F32), (lane >= hd).astype(F32)]
    li = _iota((LANES, LANES), 0)
    lj = _iota((LANES, LANES), 1)
    eye = (li == lj).astype(F32)

    def head_sum(x, scale=1.0):
        s0 = jnp.sum(x * hmask[0], axis=-1, keepdims=True) * scale
        s1 = jnp.sum(x * hmask[1], axis=-1, keepdims=True) * scale
        return s0 * hmask[0] + s1 * hmask[1]

    r, k, v = (seq(shifted(p, q, mu)) for p, q, mu in
               ((pr_ref, qr_ref, mur_ref), (pk_ref, qk_ref, muk_ref), (pv_ref, qv_ref, muv_ref)))
    lora = shifted(pl_ref, ql_ref, mul_ref)
    xwa = lora[:, :LANES]
    xg = lora[:, LANES:]
    nw = -(w0_ref[...] + _mm3(jnp.tanh(xwa), w2_ref[...]))
    softplus = jnp.maximum(nw, 0.0) + jnp.log1p(jnp.exp(-jnp.abs(nw)))
    lw = seq(-jnp.exp(-softplus - 0.5))
    a = seq(jax.nn.sigmoid(a0_ref[...] + _mm3(xwa, a2_ref[...])))
    g = seq(_mm3(jax.nn.sigmoid(xg), g2_ref[...]))

    kkr = k * vec(kk_ref)
    kk = kkr * lax.rsqrt(jnp.maximum(head_sum(kkr * kkr), 1e-24))
    kmod = k * (1.0 + (a - 1.0) * vec(ka_ref))
    be = kk * a

    pos = _iota((P * tb, LANES), 0) & (C - 1)
    cw2 = lw.reshape(P * tb, LANES)
    sh = 1
    while sh < C:
        cw2 = cw2 + jnp.where(pos >= sh, pltpu.roll(cw2, sh, 0), 0.0)
        sh *= 2
    cw = cw2.reshape(P, tb, LANES)
    cl = jnp.concatenate(
        [jnp.broadcast_to(cw[:, (c + 1) * C - 1:(c + 1) * C, :], (P, C, LANES)) for c in range(nch)], axis=1)
    e_out = jnp.exp(-cw)
    e_rem = jnp.exp(cl - cw)
    rt = r * jnp.exp(cw)
    at = -kk * jnp.exp(cw - lw)
    kt = kmod * e_out
    bt = be * e_out
    kh = kmod * e_rem
    bh = be * e_rem

    def stack_masked(x):
        return jnp.concatenate(
            [x[:, c * C:(c + 1) * C] * hmask[h] for c in range(nch) for h in range(2)], axis=1)

    def stack_dup(x):
        return jnp.concatenate([x[:, c * C:(c + 1) * C] for c in range(nch) for h in range(2)], axis=1)

    sr = 2 * nch * C
    at_s = stack_masked(at)
    rt_s = stack_masked(rt)
    v_s = stack_masked(v)
    gram = _bmm_nt(jnp.concatenate([at_s, rt_s], axis=1),
                   jnp.concatenate([stack_dup(bt), stack_dup(kt)], axis=1))
    ui = _iota((sr, sr), 0)
    uj = _iota((sr, sr), 1)
    unit = _blk(ui, C) == _blk(uj, C)
    strict = unit & (ui > uj)
    incl = unit & (ui >= uj)
    a_ab = jnp.where(strict, gram[:, :sr, :sr], 0.0)
    a_ak = jnp.where(strict, gram[:, :sr, sr:], 0.0)
    a_rb = jnp.where(incl, gram[:, sr:, :sr], 0.0)
    a_rk = jnp.where(incl, gram[:, sr:, sr:], 0.0)

    x = (ui == uj).astype(F32) + jnp.where(_blk(ui, 2) == _blk(uj, 2), a_ab, 0.0)
    nblk = 2
    while nblk < C:
        e = jnp.where((_blk(ui, 2 * nblk) == _blk(uj, 2 * nblk)) & (_blk(ui, nblk) != _blk(uj, nblk)), a_ab, 0.0)
        x = x + _bmm(_bmm(x, e), x)
        nblk *= 2

    av = _bmm(jnp.concatenate([a_ak, a_rk], axis=1), v_s)
    ah_ul = _bmm(x, jnp.concatenate([at_s, av[:, :sr]], axis=-1))
    r_y = _bmm(a_rb, ah_ul)
    rh = rt_s + r_y[:, :, :LANES]
    yl = r_y[:, :, LANES:] + av[:, sr:]
    bh_s = stack_masked(bh)
    kh_s = stack_masked(kh)
    zero = jnp.zeros((P, 2 * C, LANES), F32)

    s = s_ref[...]
    ys = []
    for c in range(nch):
        u0 = slice(2 * c * C, (2 * c + 1) * C)
        u1 = slice((2 * c + 1) * C, (2 * c + 2) * C)
        uc = slice(2 * c * C, (2 * c + 2) * C)
        lhs_t = jnp.concatenate([_bt(bh_s[:, uc]), _bt(kh_s[:, uc])], axis=-1)
        rhs = jnp.concatenate([ah_ul[:, uc], jnp.concatenate([zero, v_s[:, uc]], axis=-1)], axis=1)
        pz = _bmm(lhs_t, rhs)
        wc = jnp.exp(cl[:, c * C:c * C + 1, :])
        pm = eye * wc + pz[:, :, :LANES]
        ys_s = _bmm(jnp.concatenate([rh[:, u0] + rh[:, u1], pm], axis=1), s)
        ys.append(ys_s[:, :C] + yl[:, u0] + yl[:, u1])
        s = ys_s[:, C:] + pz[:, :, LANES:]
    s_ref[...] = s
    y = jnp.concatenate(ys, axis=1)

    mu = head_sum(y, 1.0 / hd)
    yc = y - mu
    var = head_sum(yc * yc, 1.0 / hd)
    yn = yc * lax.rsqrt(var + RWKV_EPS) * vec(lng_ref) + vec(lnb_ref)
    bonus = head_sum(r * kmod * vec(rk_ref)) * v
    out = (yn + bonus) * g
    o_ref[0] = jnp.concatenate([out[p] for p in range(P)], axis=-1).astype(o_ref.dtype)


def rwkv_mix(pp, mu, w0, w2, a0, a2, g2, k_k, k_a, r_k, ln_g, ln_b, w, npair=4):
    b, tp, n = pp.shape
    tb = RWKV_TBLK
    wl = npair * LANES
    nl = n - 3 * w
    assert w % wl == 0 and (3 * w) % nl == 0 and nl == 3 * LANES
    nj = w // wl
    lblk = 3 * w // nl
    rows8 = tb // SUBLANES
    mu_p = jnp.zeros((1, n), F32).at[0, :mu.shape[0]].set(mu)
    w2_p = jnp.zeros((LANES, w), F32).at[:DECAY_LORA].set(w2)
    a2_p = jnp.zeros((LANES, w), F32).at[DECAY_LORA:DECAY_LORA + AAA_LORA].set(a2)
    g2_p = jnp.zeros((nl - LANES, w), F32).at[:GATE_LORA].set(g2)

    def cur(width, col):
        return pl.BlockSpec((1, tb, width), lambda bi, j, t: (bi, t, col(j)))

    def prev(width, col):
        return pl.BlockSpec((1, SUBLANES, width), lambda bi, j, t: (bi, jnp.maximum(t * rows8 - 1, 0), col(j)))

    def mu_spec(width, col):
        return pl.BlockSpec((1, width), lambda bi, j, t: (0, col(j)))

    cols = [lambda j: j, lambda j: nj + j, lambda j: 2 * nj + j]
    lcol = lambda j: lblk
    vec = pl.BlockSpec((1, wl), lambda bi, j, t: (0, j))
    mat = lambda rows: pl.BlockSpec((rows, wl), lambda bi, j, t: (0, j))
    in_specs = ([cur(wl, c) for c in cols] + [cur(nl, lcol)] + [prev(wl, c) for c in cols] + [prev(nl, lcol)]
                + [mu_spec(wl, c) for c in cols] + [mu_spec(nl, lcol)]
                + [vec, mat(LANES), vec, mat(LANES), mat(nl - LANES)] + [vec] * 5)
    return pl.pallas_call(
        functools.partial(_rwkv_scan_kernel, npair=npair),
        out_shape=jax.ShapeDtypeStruct((b, tp, w), BF16),
        grid=(b, nj, tp // tb),
        in_specs=in_specs,
        out_specs=pl.BlockSpec((1, tb, wl), lambda bi, j, t: (bi, t, j)),
        scratch_shapes=[pltpu.VMEM((npair, LANES, LANES), F32)],
        compiler_params=_cparams("parallel", "parallel", "arbitrary"),
        name="rwkv_mix",
    )(pp, pp, pp, pp, pp, pp, pp, pp, mu_p, mu_p, mu_p, mu_p,
      w0.reshape(1, w), w2_p, a0.reshape(1, w), a2_p, g2_p,
      k_k.reshape(1, w), k_a.reshape(1, w), r_k.reshape(1, w), ln_g.reshape(1, w), ln_b.reshape(1, w))


def _merge_kernel(o1_ref, o2_ref, o3_ref, wb_ref, g1_ref, g2_ref, g3_ref, out_ref):
    acc = g1_ref[...].astype(F32) * _dot(o1_ref[...], wb_ref[0])
    acc = acc + g2_ref[...].astype(F32) * _dot(o2_ref[...], wb_ref[1])
    acc = acc + g3_ref[...].astype(F32) * _dot(o3_ref[...], wb_ref[2])
    out_ref[...] = acc.astype(out_ref.dtype)


def merge(o1, o2, o3, w_branch, gates, tm=ROW_TILE, tn=1024):
    m, w = o1.shape
    d = w_branch.shape[2]
    nj = d // tn
    ospec = pl.BlockSpec((tm, w), lambda i, j: (i, 0))
    gspec = lambda b: pl.BlockSpec((tm, tn), lambda i, j, b=b: (i, b * nj + j))
    return pl.pallas_call(
        _merge_kernel,
        out_shape=jax.ShapeDtypeStruct((m, d), BF16),
        grid=(m // tm, nj),
        in_specs=[ospec, ospec, ospec, pl.BlockSpec((3, w, tn), lambda i, j: (0, 0, j)),
                  gspec(0), gspec(1), gspec(2)],
        out_specs=pl.BlockSpec((tm, tn), lambda i, j: (i, j)),
        compiler_params=_cparams("parallel", "arbitrary"),
        name="merge",
    )(o1, o2, o3, w_branch, gates, gates, gates)


def _pack_bf16_pair(hi, lo):
    uh = lax.bitcast_convert_type(hi.astype(BF16).astype(F32), jnp.uint32)
    ul = lax.bitcast_convert_type(lo.astype(BF16).astype(F32), jnp.uint32)
    return uh | lax.shift_right_logical(ul, jnp.uint32(16))


def _unpack_bf16_pair(w):
    hi = lax.bitcast_convert_type(w & jnp.uint32(0xFFFF0000), F32)
    lo = lax.bitcast_convert_type(lax.shift_left(w, jnp.uint32(16)), F32)
    return hi, lo


def _store_token_rows(ref, val):
    tm, n = val.shape
    seg = n // LANES
    for s in range(seg):
        ref[pl.ds(s, tm, stride=seg), :] = val[:, s * LANES:(s + 1) * LANES]


def _load_token_rows(ref, seg):
    tm = ref.shape[0] // seg
    return jnp.concatenate([ref[pl.ds(s, tm, stride=seg), :] for s in range(seg)], axis=-1)


def _start_token_gather(idx_ref, idx_base, n, src_hbm, dst_vmem, dst_base, sem, seg):
    def body(r, carry):
        t = pl.multiple_of(idx_ref[idx_base + r] * seg, seg)
        o = pl.multiple_of((dst_base + r) * seg, seg)
        pltpu.make_async_copy(src_hbm.at[pl.ds(t, seg), :], dst_vmem.at[pl.ds(o, seg), :], sem).start()
        return carry

    lax.fori_loop(0, n, body, 0, unroll=8)


def _wait_token_gather(n, src_hbm, dst_vmem, sem, seg):
    pltpu.make_async_copy(src_hbm.at[pl.ds(0, n * seg), :], dst_vmem.at[pl.ds(0, n * seg), :], sem).wait()


def _router_kernel(h_ref, g_ref, wr_ref, br_ref, m_ref, eid_ref, wt_ref):
    x = h_ref[...]
    m = x * lax.rsqrt(jnp.mean(x * x, axis=-1, keepdims=True) + RMS_EPS) * g_ref[...]
    half = m.shape[1] // 2
    _store_token_rows(m_ref, _pack_bf16_pair(m[:, :half], m[:, half:]))
    logits = _mm3(m, wr_ref[...]) + br_ref[...]
    lane = _iota(logits.shape, 1)
    lanef = lane.astype(F32)
    neg = -jnp.inf
    big = float(LANES)
    glog = jnp.where(lane < N_GROUPS, logits, neg)
    gmax = jnp.max(glog, axis=-1, keepdims=True)
    gsel = jnp.min(jnp.where(glog == gmax, lanef, big), axis=-1, keepdims=True)
    pg = 1.0 / jnp.sum(jnp.exp(glog - gmax), axis=-1, keepdims=True)
    lo = N_GROUPS + gsel * EXPERTS_PER_GROUP
    el = jnp.where((lanef >= lo) & (lanef < lo + EXPERTS_PER_GROUP), logits, neg)
    v1 = jnp.max(el, axis=-1, keepdims=True)
    i1 = jnp.min(jnp.where(el == v1, lanef, big), axis=-1, keepdims=True)
    el2 = jnp.where(lanef == i1, neg, el)
    v2 = jnp.max(el2, axis=-1, keepdims=True)
    i2 = jnp.min(jnp.where(el2 == v2, lanef, big), axis=-1, keepdims=True)
    e2 = jnp.exp(v2 - v1)
    w1 = pg / (1.0 + e2)
    w2 = pg * e2 / (1.0 + e2)
    eid = jnp.where(lane == 0, i1 - N_GROUPS, jnp.where(lane == 1, i2 - N_GROUPS, 0.0))
    eid_ref[...] = eid.astype(jnp.int32)
    wt_ref[...] = jnp.where(lane == 0, w1, jnp.where(lane == 1, w2, 0.0))


def router(h, g, wg, bg, we, be, tm=POINT_TILE):
    m, d = h.shape
    nr = N_GROUPS + N_EXPERTS
    wr = jnp.zeros((d, LANES), F32).at[:, :N_GROUPS].set(wg).at[:, N_GROUPS:nr].set(we)
    br = jnp.zeros((1, LANES), F32).at[0, :N_GROUPS].set(bg).at[0, N_GROUPS:nr].set(be)
    row = pl.BlockSpec((tm, d), lambda i: (i, 0))
    slab = pl.BlockSpec((tm, LANES), lambda i: (i, 0))
    seg = d // 2 // LANES
    return pl.pallas_call(
        _router_kernel,
        out_shape=[jax.ShapeDtypeStruct((m * seg, LANES), jnp.uint32), jax.ShapeDtypeStruct((m, LANES), jnp.int32),
                   jax.ShapeDtypeStruct((m, LANES), F32)],
        grid=(m // tm,),
        in_specs=[row, pl.BlockSpec((1, d), lambda i: (0, 0)), pl.BlockSpec((d, LANES), lambda i: (0, 0)),
                  pl.BlockSpec((1, LANES), lambda i: (0, 0))],
        out_specs=[pl.BlockSpec((tm * seg, LANES), lambda i: (i, 0)), slab, slab],
        compiler_params=_cparams("parallel"),
        name="moe_router",
    )(h, g.reshape(1, d), wr, br)


def _expert_kernel(te_ref, first_ref, nu_ref, src_ref, x_hbm, wg_ref, wu_ref, wd_ref, y_ref,
                   wgu_s, wd_s, xbuf, sem, *, nk, dk, tm):
    i = pl.program_id(0)
    kc = pl.program_id(1)
    nu = nu_ref[0]
    used = i < nu
    de = nk * dk
    seg = wd_s.shape[1] // 2 // LANES
    slot = lax.rem(i, 2)

    def start_gather(tile, s):
        _start_token_gather(src_ref, tile * tm, tm, x_hbm, xbuf.at[s], 0, sem.at[s], seg)

    @pl.when((i == 0) & (kc == 0) & used)
    def _():
        start_gather(0, 0)

    @pl.when((kc == nk - 1) & (i + 1 < nu))
    def _():
        start_gather(i + 1, 1 - slot)

    @pl.when(used & (first_ref[i] == 1))
    def _():
        for c in range(nk):
            @pl.when(kc == c)
            def _():
                wgu_s[:, c * dk:(c + 1) * dk] = wg_ref[0, 0].astype(BF16)
                wgu_s[:, de + c * dk:de + (c + 1) * dk] = wu_ref[0, 0].astype(BF16)
                wd_s[c * dk:(c + 1) * dk, :] = wd_ref[0, 0].astype(BF16)

    @pl.when(used & (kc == nk - 1))
    def _():
        _wait_token_gather(tm, x_hbm, xbuf.at[slot], sem.at[slot], seg)
        xh, xl = _unpack_bf16_pair(_load_token_rows(xbuf.at[slot], seg))
        x = jnp.concatenate([xh.astype(BF16), xl.astype(BF16)], axis=-1)
        gu = _dot(x, wgu_s[...])
        gt = gu[:, :de]
        act = (gt * jax.nn.sigmoid(gt) * gu[:, de:]).astype(BF16)
        y = _dot(act, wd_s[...])
        half = y.shape[1] // 2
        _store_token_rows(y_ref, _pack_bf16_pair(y[:, :half], y[:, half:]))

    @pl.when(jnp.logical_not(used) & (kc == nk - 1))
    def _():
        y_ref[...] = jnp.zeros_like(y_ref)


def expert_ffn(xm, src, tile_expert, tile_first, n_used, w_gate, w_up, w_down, layer, tm=MOE_TILE, nk=4):
    d = w_down.shape[3]
    seg = d // 2 // LANES
    p = src.shape[0]
    de = w_down.shape[2]
    dk = de // nk
    assert dk * nk == de and dk % LANES == 0

    def chunk(i, kc, first):
        return jnp.where(first[i] == 1, kc, nk - 1)

    grid_spec = pltpu.PrefetchScalarGridSpec(
        num_scalar_prefetch=4,
        grid=(p // tm, nk),
        in_specs=[pl.BlockSpec(memory_space=pl.ANY),
                  pl.BlockSpec((1, 1, d, dk), lambda i, kc, te, first, *_: (layer, te[i], 0, chunk(i, kc, first))),
                  pl.BlockSpec((1, 1, d, dk), lambda i, kc, te, first, *_: (layer, te[i], 0, chunk(i, kc, first))),
                  pl.BlockSpec((1, 1, dk, d), lambda i, kc, te, first, *_: (layer, te[i], chunk(i, kc, first), 0))],
        out_specs=pl.BlockSpec((tm * seg, LANES), lambda i, kc, *_: (i, 0)),
        scratch_shapes=[pltpu.VMEM((d, 2 * de), BF16), pltpu.VMEM((de, d), BF16),
                        pltpu.VMEM((2, tm * seg, LANES), jnp.uint32), pltpu.SemaphoreType.DMA((2,))],
    )
    return pl.pallas_call(
        functools.partial(_expert_kernel, nk=nk, dk=dk, tm=tm),
        out_shape=jax.ShapeDtypeStruct((p * seg, LANES), jnp.uint32),
        grid_spec=grid_spec,
        compiler_params=_cparams("arbitrary", "arbitrary"),
        name="moe_experts",
    )(tile_expert, tile_first, n_used, src, xm, w_gate, w_up, w_down)


def _combine_kernel(dest_ref, h_ref, ys_hbm, wt_ref, g_ref, *rest, emit_h, m):
    out_refs, (ybuf, sem) = rest[:-2], rest[-2:]
    n_ref = out_refs[-1]
    i = pl.program_id(0)
    slot = lax.rem(i, 2)
    tm, d = h_ref.shape
    half = d // 2
    seg = half // LANES

    def start_gather(step, s):
        for k in range(TOP_K):
            _start_token_gather(dest_ref, k * m + step * tm, tm, ys_hbm, ybuf.at[s], k * tm, sem.at[s], seg)

    @pl.when(i == 0)
    def _():
        start_gather(0, 0)

    @pl.when(i + 1 < pl.num_programs(0))
    def _():
        start_gather(i + 1, 1 - slot)

    _wait_token_gather(TOP_K * tm, ys_hbm, ybuf.at[slot], sem.at[slot], seg)
    wt = wt_ref[...]
    w0, w1 = wt[:, 0:1], wt[:, 1:2]
    yb = ybuf.at[slot]
    a0, b0 = _unpack_bf16_pair(_load_token_rows(yb.at[pl.ds(0, tm * seg), :], seg))
    a1, b1 = _unpack_bf16_pair(_load_token_rows(yb.at[pl.ds(tm * seg, tm * seg), :], seg))
    lo = h_ref[:, :half] + w0 * a0 + w1 * a1
    hi = h_ref[:, half:] + w0 * b0 + w1 * b1
    if emit_h:
        out_refs[0][:, :half] = lo
        out_refs[0][:, half:] = hi
    ms = (jnp.sum(lo * lo, axis=-1, keepdims=True) + jnp.sum(hi * hi, axis=-1, keepdims=True)) * (1.0 / d)
    inv = lax.rsqrt(ms + RMS_EPS)
    n_ref[:, :half] = (lo * inv * g_ref[:, :half]).astype(n_ref.dtype)
    n_ref[:, half:] = (hi * inv * g_ref[:, half:]).astype(n_ref.dtype)


def combine(h, ys, dest, wt, g_next, norm_dtype, emit_h, tm=POINT_TILE):
    m, d = h.shape
    seg = d // 2 // LANES
    row = pl.BlockSpec((tm, d), lambda i, dest: (i, 0))
    outs = [jax.ShapeDtypeStruct((m, d), F32)] * emit_h + [jax.ShapeDtypeStruct((m, d), norm_dtype)]
    grid_spec = pltpu.PrefetchScalarGridSpec(
        num_scalar_prefetch=1,
        grid=(m // tm,),
        in_specs=[row, pl.BlockSpec(memory_space=pl.ANY), pl.BlockSpec((tm, LANES), lambda i, dest: (i, 0)),
                  pl.BlockSpec((1, d), lambda i, dest: (0, 0))],
        out_specs=[row] * len(outs),
        scratch_shapes=[pltpu.VMEM((2, TOP_K * tm * seg, LANES), jnp.uint32), pltpu.SemaphoreType.DMA((2,))],
    )
    return pl.pallas_call(
        functools.partial(_combine_kernel, emit_h=emit_h, m=m),
        out_shape=outs,
        grid_spec=grid_spec,
        compiler_params=_cparams("arbitrary"),
        name="moe_combine",
    )(dest, h, ys, wt, g_next.reshape(1, d))


def hier_moe(h, norm_g, wg, bg, we, be, w_gate, w_up, w_down, layer, g_next, norm_dtype, emit_h, tm=MOE_TILE):
    m, d = h.shape
    xm, eid_slab, wt = router(h, norm_g, wg, bg, we, be)
    npair = m * TOP_K
    eid = eid_slab[:, :TOP_K].T.reshape(npair)
    onehot = (eid[:, None] == jnp.arange(N_EXPERTS, dtype=jnp.int32)[None, :]).astype(jnp.int32)
    counts = jnp.sum(onehot, axis=0)
    rank = jnp.sum((jnp.cumsum(onehot, axis=0) - 1) * onehot, axis=1)
    padded = ((counts + tm - 1) // tm) * tm
    ends = jnp.cumsum(padded)
    starts = ends - padded
    dest = jnp.sum(onehot * starts[None, :], axis=1) + rank
    p_rows = ((npair + tm - 1) // tm + N_EXPERTS) * tm
    pair_token = jnp.arange(npair, dtype=jnp.int32) % m
    src = jnp.zeros((p_rows,), jnp.int32).at[dest].set(pair_token, mode="promise_in_bounds",
                                                         unique_indices=True)
    n_tiles = p_rows // tm
    tile_start = jnp.arange(n_tiles, dtype=jnp.int32) * tm
    n_used = (ends[-1] // tm).astype(jnp.int32).reshape(1)
    last_used = jnp.maximum(n_used[0] - 1, 0) * tm
    tile_expert = jnp.sum((ends[None, :] <= jnp.minimum(tile_start, last_used)[:, None]).astype(jnp.int32), axis=1)
    tile_expert = jnp.minimum(tile_expert, N_EXPERTS - 1)
    tile_first = jnp.concatenate([jnp.ones((1,), jnp.int32),
                                  (tile_expert[1:] != tile_expert[:-1]).astype(jnp.int32)])
    ys = expert_ffn(xm, src, tile_expert, tile_first, n_used, w_gate, w_up, w_down, layer, tm)
    return combine(h, ys, dest.astype(jnp.int32), wt, g_next, norm_dtype, emit_h)


def kernel(x, meta_tokens, final_norm_g, norm1_g, w_in, ret_gn_g, lru_conv_w, lru_conv_b, lru_wa, lru_ba,
           lru_wx, lru_bx, lru_lambda, rwkv_mu, rwkv_w0, rwkv_w2, rwkv_a0, rwkv_a2, rwkv_g2, rwkv_k_k,
           rwkv_k_a, rwkv_r_k, rwkv_ln_g, rwkv_ln_b, w_branch, w_out, norm2_g, moe_wg, moe_bg, moe_we,
           moe_be, moe_w_gate, moe_w_up, moe_w_down):
    b, seq, d = x.shape
    depth = w_in.shape[0]
    w = ret_gn_g.shape[1]
    t = N_META + seq
    tp = ((t + CHUNK - 1) // CHUNK) * CHUNK
    m = b * tp
    rwkv_cols = rwkv_mu.shape[1]
    rwkv_pad = 3 * w + LANES + 2 * LANES
    c_ret, c_lru, c_rwkv = 4 * w, 6 * w, 6 * w + rwkv_cols

    meta = jnp.broadcast_to(meta_tokens.astype(x.dtype)[None], (b, N_META, d))
    h = jnp.concatenate([meta, x, jnp.zeros((b, tp - t, d), x.dtype)], axis=1).reshape(m, d)

    a = rmsnorm(h, norm1_g[0], BF16)
    for l in range(depth):
        wl = w_in[l]
        w_ret = wl[:, :c_ret].astype(BF16)
        w_lru = wl[:, c_ret:c_lru].astype(BF16)
        w_rwkv = jnp.pad(wl[:, c_lru:c_rwkv], ((0, 0), (0, rwkv_pad - rwkv_cols))).astype(BF16)

        p_ret = matmul(a, w_ret, BF16, tm=ROW_TILE, tn=1024, name="proj_ret")
        p_lru = matmul(a, w_lru, F32, tm=ROW_TILE, tn=1024, name="proj_lru")
        p_rwkv = matmul(a, w_rwkv, F32, tm=ROW_TILE, tn=rwkv_pad // 3, name="proj_rwkv")
        gates = gate_matmul(a, wl[:, c_rwkv:], tm=ROW_TILE, tn=2048)

        o_ret = retention(p_ret.reshape(b, tp, 4 * w), ret_gn_g[l])
        o_lru = rg_lru(p_lru.reshape(b, tp, 2 * w), lru_conv_w[l], lru_conv_b[l], lru_wa[l], lru_ba[l],
                       lru_wx[l], lru_bx[l], lru_lambda[l])
        o_rwkv = rwkv_mix(p_rwkv.reshape(b, tp, rwkv_pad), rwkv_mu[l], rwkv_w0[l], rwkv_w2[l], rwkv_a0[l],
                          rwkv_a2[l], rwkv_g2[l], rwkv_k_k[l], rwkv_k_a[l], rwkv_r_k[l], rwkv_ln_g[l],
                          rwkv_ln_b[l], w)

        merged = merge(o_ret.reshape(m, w), o_lru.reshape(m, w), o_rwkv.reshape(m, w),
                       w_branch[l].astype(BF16), gates)
        h = matmul(merged, w_out[l].astype(BF16), F32, tm=ROW_TILE, tn=1024, residual=h, name="proj_out")

        moe_args = (h, norm2_g[l], moe_wg[l], moe_bg[l], moe_we[l], moe_be[l], moe_w_gate, moe_w_up, moe_w_down, l)
        if l + 1 < depth:
            h, a = hier_moe(*moe_args, norm1_g[l + 1], BF16, True)
        else:
            (out,) = hier_moe(*moe_args, final_norm_g, F32, False)

    return out.reshape(b, tp, d)[:, N_META:t]
```

```python
import functools
import math

import jax
import jax.numpy as jnp
from jax import lax
from jax.experimental import pallas as pl
from jax.experimental.pallas import tpu as pltpu

F32 = jnp.float32
BF16 = jnp.bfloat16
FP8 = jnp.float8_e4m3fn
FP8_MAX = 448.0

LANES = 128
SUBLANES = 8
VMEM_BYTES_V7X = 64 * 1024 * 1024
VMEM_LIMIT = VMEM_BYTES_V7X - 8 * 1024 * 1024

N_META = 16
CHUNK = 128
RET_DIM = 128
ROPE_BASE = 10000.0
RET_EPS = 1e-5
LRU_BLOCK = 128
CONV_W = 4
LRU_C = 8.0
RWKV_DIM = 64
DECAY_LORA = 64
AAA_LORA = 64
GATE_LORA = 160
RWKV_EPS = 64e-5
N_GROUPS = 4
EXPERTS_PER_GROUP = 8
N_EXPERTS = N_GROUPS * EXPERTS_PER_GROUP
TOP_K = 2
RMS_EPS = 1e-6

RWKV_CHUNK = 64
RWKV_TBLK = 128
ROW_TILE = 512
POINT_TILE = 256
MOE_TILE = 512


def _cparams(*sem):
    return pltpu.CompilerParams(dimension_semantics=sem, vmem_limit_bytes=VMEM_LIMIT)


def _dot(a, b, precision=None):
    return jnp.dot(a, b, preferred_element_type=F32, precision=precision)


def _dot_nt(a, b, precision=None):
    return lax.dot_general(a, b, (((1,), (1,)), ((), ())), preferred_element_type=F32, precision=precision)


def _iota(shape, dim):
    return lax.broadcasted_iota(jnp.int32, shape, dim)


def _blk(i, n):
    shift = n.bit_length() - 1
    assert 1 << shift == n
    return lax.shift_right_logical(i, jnp.int32(shift))


def _embed_kernel(x_ref, meta_ref, g_ref, h_ref, a_ref, *, pad):
    i = pl.program_id(1)
    d = x_ref.shape[-1]
    head = jnp.concatenate([jnp.zeros((pad, d), F32), meta_ref[...]], axis=0)
    h = jnp.where(i == 0, head, x_ref[0])
    h_ref[...] = h
    a = h * lax.rsqrt(jnp.mean(h * h, axis=-1, keepdims=True) + RMS_EPS)
    a_ref[...] = (a * g_ref[...]).astype(a_ref.dtype)


def embed(x, meta, g, pad):
    b, seq, d = x.shape
    assert pad + meta.shape[0] == CHUNK and seq % CHUNK == 0
    nblk = seq // CHUNK + 1
    out = pl.BlockSpec((CHUNK, d), lambda bi, i: (bi * nblk + i, 0))
    return pl.pallas_call(
        functools.partial(_embed_kernel, pad=pad),
        out_shape=[jax.ShapeDtypeStruct((b * nblk * CHUNK, d), F32),
                   jax.ShapeDtypeStruct((b * nblk * CHUNK, d), BF16)],
        grid=(b, nblk),
        in_specs=[pl.BlockSpec((1, CHUNK, d), lambda bi, i: (bi, jnp.maximum(i - 1, 0), 0)),
                  pl.BlockSpec(meta.shape, lambda bi, i: (0, 0)), pl.BlockSpec((1, d), lambda bi, i: (0, 0))],
        out_specs=[out, out],
        compiler_params=_cparams("parallel", "arbitrary"),
        name="embed",
    )(x, meta.astype(F32), g.reshape(1, d))


def _matmul_kernel(a_ref, w_ref, o_ref, *, act):
    acc = _dot(a_ref[...], w_ref[...])
    if act == "sigmoid":
        acc = jax.nn.sigmoid(acc)
    o_ref[...] = acc.astype(o_ref.dtype)


def _matmul_fp8_sigmoid_kernel(a_ref, w_ref, s_ref, o_ref):
    o_ref[...] = jax.nn.sigmoid(_dot(a_ref[...], w_ref[...]) * s_ref[...]).astype(o_ref.dtype)


def _fp8_quantize(x, axis):
    amax = jnp.max(jnp.abs(x), axis=axis, keepdims=True)
    scale = jnp.where(amax > 0, FP8_MAX / amax, 1.0)
    return (x * scale).astype(FP8), 1.0 / scale


def gate_matmul(a, w, *, tm, tn):
    m, k = a.shape
    n = w.shape[1]
    a8, inv_a = _fp8_quantize(a.astype(F32), None)
    w8, inv_w = _fp8_quantize(w, 0)
    return pl.pallas_call(
        _matmul_fp8_sigmoid_kernel,
        out_shape=jax.ShapeDtypeStruct((m, n), BF16),
        grid=(m // tm, n // tn),
        in_specs=[pl.BlockSpec((tm, k), lambda i, j: (i, 0)), pl.BlockSpec((k, tn), lambda i, j: (0, j)),
                  pl.BlockSpec((1, tn), lambda i, j: (0, j))],
        out_specs=pl.BlockSpec((tm, tn), lambda i, j: (i, j)),
        compiler_params=_cparams("parallel", "arbitrary"),
        name="proj_gates",
    )(a8, w8, (inv_a.reshape(1, 1) * inv_w).astype(F32))


def _matmul_res_kernel(a_ref, w_ref, r_ref, o_ref):
    o_ref[...] = r_ref[...] + _dot(a_ref[...], w_ref[...])


def matmul(a, w, out_dtype, *, tm, tn, act=None, residual=None, name="matmul"):
    m, k = a.shape
    n = w.shape[1]
    assert m % tm == 0 and n % tn == 0, (m, n, tm, tn)
    in_specs = [pl.BlockSpec((tm, k), lambda i, j: (i, 0)), pl.BlockSpec((k, tn), lambda i, j: (0, j))]
    args = [a, w]
    if residual is None:
        body = functools.partial(_matmul_kernel, act=act)
    else:
        body = _matmul_res_kernel
        in_specs.append(pl.BlockSpec((tm, tn), lambda i, j: (i, j)))
        args.append(residual)
    return pl.pallas_call(
        body,
        out_shape=jax.ShapeDtypeStruct((m, n), out_dtype),
        grid=(m // tm, n // tn),
        in_specs=in_specs,
        out_specs=pl.BlockSpec((tm, tn), lambda i, j: (i, j)),
        compiler_params=_cparams("parallel", "arbitrary"),
        name=name,
    )(*args)


def _retention_kernel(q_ref, k_ref, v_ref, g_ref, cos_ref, sin_ref, gn_ref, o_ref, state_ref, *, heads):
    c = pl.program_id(1)

    @pl.when(c == 0)
    def _():
        state_ref[...] = jnp.zeros_like(state_ref)

    d = RET_DIM
    cos = cos_ref[...]
    sin = sin_ref[...]
    row = _iota((CHUNK, CHUNK), 0)
    col = _iota((CHUNK, CHUNK), 1)
    diff = (row - col).astype(F32)
    causal = row >= col
    idx = _iota((CHUNK, 1), 0).astype(F32)
    for h in range(heads):
        lg = math.log1p(-(2.0 ** (-5.0 - h)))
        sl = slice(h * d, (h + 1) * d)
        q = q_ref[0, :, sl].astype(F32)
        k = k_ref[0, :, sl].astype(F32)
        v = v_ref[0, :, sl].astype(BF16)
        g = g_ref[0, :, sl].astype(F32)
        qr = q * cos + pltpu.roll(q, d // 2, 1) * sin
        kr = (k * cos + pltpu.roll(k, d // 2, 1) * sin) * (d ** -0.5)
        decay = jnp.where(causal, jnp.exp(lg * jnp.maximum(diff, 0.0)), 0.0)
        scores = _dot_nt(qr.astype(BF16), kr.astype(BF16)) * decay
        o = _dot(scores.astype(BF16), v)
        xi = jnp.exp(lg * (idx + 1.0))
        st = state_ref[h]
        o = o + _dot((qr * xi).astype(BF16), st.astype(BF16))
        zeta = jnp.exp(lg * (CHUNK - 1.0 - idx))
        kv = _dot((kr * zeta).T.astype(BF16), v)
        state_ref[h] = math.exp(lg * CHUNK) * st + kv
        mu = jnp.mean(o, axis=-1, keepdims=True)
        oc = o - mu
        var = jnp.mean(oc * oc, axis=-1, keepdims=True)
        on = oc * lax.rsqrt(var + RET_EPS) * gn_ref[:, sl]
        o_ref[0, :, sl] = (g * jax.nn.sigmoid(g) * on).astype(o_ref.dtype)


def _rope_tables(tp, pad):
    half = RET_DIM // 2
    inv_freq = ROPE_BASE ** (-jnp.arange(half, dtype=F32) / half)
    ang = (jnp.arange(tp, dtype=F32) - pad)[:, None] * inv_freq[None, :]
    cos, sin = jnp.cos(ang), jnp.sin(ang)
    return jnp.concatenate([cos, cos], axis=-1), jnp.concatenate([-sin, sin], axis=-1)


def retention(pr, gn_g, pad):
    b, tp, w4 = pr.shape
    w = w4 // 4
    heads = w // RET_DIM
    cos, sin = _rope_tables(tp, pad)
    spec = lambda j: pl.BlockSpec((1, CHUNK, w), lambda bi, c, j=j: (bi, c, j))
    return pl.pallas_call(
        functools.partial(_retention_kernel, heads=heads),
        out_shape=jax.ShapeDtypeStruct((b, tp, w), BF16),
        grid=(b, tp // CHUNK),
        in_specs=[spec(0), spec(1), spec(2), spec(3),
                  pl.BlockSpec((CHUNK, RET_DIM), lambda bi, c: (c, 0)),
                  pl.BlockSpec((CHUNK, RET_DIM), lambda bi, c: (c, 0)),
                  pl.BlockSpec((1, w), lambda bi, c: (0, 0))],
        out_specs=pl.BlockSpec((1, CHUNK, w), lambda bi, c: (bi, c, 0)),
        scratch_shapes=[pltpu.VMEM((heads, RET_DIM, RET_DIM), F32)],
        compiler_params=_cparams("parallel", "arbitrary"),
        name="retention",
    )(pr, pr, pr, pr, cos, sin, gn_g.reshape(1, w))


def _lru_kernel(x_ref, y_ref, cw_ref, cb_ref, wa_ref, ba_ref, wx_ref, bx_ref, lam_ref, o_ref,
                xs_ref, h_ref, *, tc, pad):
    c = pl.program_id(1)
    w = x_ref.shape[-1]

    @pl.when(c == 0)
    def _():
        xs_ref[0:SUBLANES, :] = jnp.zeros((SUBLANES, w), F32)
        h_ref[...] = jnp.zeros_like(h_ref)

    x = x_ref[0].astype(F32)
    xs_ref[SUBLANES:, :] = x
    xc = cb_ref[...] + cw_ref[CONV_W - 1:CONV_W, :] * x
    for j in range(CONV_W - 1):
        off = SUBLANES - (CONV_W - 1) + j
        xc = xc + cw_ref[j:j + 1, :] * xs_ref[off:off + tc, :]
    xs_ref[0:SUBLANES, :] = x[tc - SUBLANES:, :]

    xcb = xc.astype(BF16)
    nb = w // LRU_BLOCK
    ra = jnp.concatenate(
        [_dot(xcb[:, n * LRU_BLOCK:(n + 1) * LRU_BLOCK], wa_ref[n]) for n in range(nb)], axis=-1)
    ri = jnp.concatenate(
        [_dot(xcb[:, n * LRU_BLOCK:(n + 1) * LRU_BLOCK], wx_ref[n]) for n in range(nb)], axis=-1)
    r = jax.nn.sigmoid(ra + ba_ref[...])
    i = jax.nn.sigmoid(ri + bx_ref[...])
    nlam = -lam_ref[...]
    softplus = jnp.maximum(nlam, 0.0) + jnp.log1p(jnp.exp(-jnp.abs(nlam)))
    log_a = -LRU_C * r * softplus
    a = jnp.exp(log_a)
    bb = jnp.sqrt(1.0 - jnp.exp(2.0 * log_a)) * (i * xc)
    row = _iota((tc, w), 0)
    bb = jnp.where((c > 0) | (row >= pad), bb, 0.0)

    s = 1
    while s < tc:
        a_sh = pltpu.roll(a, s, 0)
        b_sh = pltpu.roll(bb, s, 0)
        m = row >= s
        bb = jnp.where(m, a * b_sh + bb, bb)
        a = jnp.where(m, a * a_sh, a)
        s *= 2
    h = a * h_ref[...] + bb
    h_ref[...] = h[tc - 1:tc, :]
    y = y_ref[0].astype(F32)
    o_ref[0] = (h * jax.nn.gelu(y)).astype(o_ref.dtype)


def rg_lru(pl_xy, conv_w, conv_b, wa, ba, wx, bx, lam, pad, tc=256):
    b, tp, w2 = pl_xy.shape
    w = w2 // 2
    if tp % tc:
        tc = CHUNK
    assert pad <= tc
    nb = w // LRU_BLOCK
    vec = lambda: pl.BlockSpec((1, w), lambda bi, c: (0, 0))
    return pl.pallas_call(
        functools.partial(_lru_kernel, tc=tc, pad=pad),
        out_shape=jax.ShapeDtypeStruct((b, tp, w), BF16),
        grid=(b, tp // tc),
        in_specs=[pl.BlockSpec((1, tc, w), lambda bi, c: (bi, c, 0)),
                  pl.BlockSpec((1, tc, w), lambda bi, c: (bi, c, 1)),
                  pl.BlockSpec((CONV_W, w), lambda bi, c: (0, 0)), vec(),
                  pl.BlockSpec((nb, LRU_BLOCK, LRU_BLOCK), lambda bi, c: (0, 0, 0)), vec(),
                  pl.BlockSpec((nb, LRU_BLOCK, LRU_BLOCK), lambda bi, c: (0, 0, 0)), vec(), vec()],
        out_specs=pl.BlockSpec((1, tc, w), lambda bi, c: (bi, c, 0)),
        scratch_shapes=[pltpu.VMEM((tc + SUBLANES, w), F32), pltpu.VMEM((1, w), F32)],
        compiler_params=_cparams("parallel", "arbitrary"),
        name="rg_lru",
    )(pl_xy, pl_xy, conv_w, conv_b.reshape(1, w), wa.astype(BF16), ba.reshape(1, w),
      wx.astype(BF16), bx.reshape(1, w), lam.reshape(1, w))


def _split_bf16(x):
    hi = x.astype(BF16)
    return hi, (x - hi.astype(F32)).astype(BF16)


def _mm3(a, b):
    ah, al = _split_bf16(a)
    bh, bl = _split_bf16(b)
    return _dot(ah, bh) + (_dot(al, bh) + _dot(ah, bl))


def _bmm(a, b):
    return jnp.einsum("pmk,pkn->pmn", a.astype(BF16), b.astype(BF16), preferred_element_type=F32)


def _bmm_nt(a, b):
    return jnp.einsum("pmk,pnk->pmn", a.astype(BF16), b.astype(BF16), preferred_element_type=F32)


def _bt(x):
    return jnp.stack([x[p].T for p in range(x.shape[0])], axis=0)


def _rwkv_scan_kernel(pr_ref, pk_ref, pv_ref, pl_ref, qr_ref, qk_ref, qv_ref, ql_ref,
                      mur_ref, muk_ref, muv_ref, mul_ref, w0_ref, w2_ref, a0_ref, a2_ref, g2_ref,
                      kk_ref, ka_ref, rk_ref, lng_ref, lnb_ref, o_ref, s_ref, *, npair):
    n = pl.program_id(2)

    @pl.when(n == 0)
    def _():
        s_ref[...] = jnp.zeros_like(s_ref)

    P = npair
    tb = pr_ref.shape[1]
    C = RWKV_CHUNK
    nch = tb // C
    hd = RWKV_DIM

    def seq(x):
        return jnp.stack([x[:, p * LANES:(p + 1) * LANES] for p in range(P)], axis=0)

    def shifted(p_ref, q_ref, mu_ref):
        x = p_ref[0]
        prev = jnp.where(n == 0, 0.0, q_ref[0][SUBLANES - 1:SUBLANES, :])
        x_prev = jnp.where(_iota(x.shape, 0) == 0, prev, pltpu.roll(x, 1, 0))
        return x + (x_prev - x) * mu_ref[...]

    def vec(ref):
        x = ref[...]
        return jnp.stack([x[:, p * LANES:(p + 1) * LANES] for p in range(P)], axis=0)

    lane = _iota((1, 1, LANES), 2)
    hmask = [(lane < hd).astype(F32), (lane >= hd).astype(F32)]
    li = _iota((LANES, LANES), 0)
    lj = _iota((LANES, LANES), 1)
    eye = (li == lj).astype(F32)

    def head_sum(x, scale=1.0):
        s0 = jnp.sum(x * hmask[0], axis=-1, keepdims=True) * scale
        s1 = jnp.sum(x * hmask[1], axis=-1, keepdims=True) * scale
        return s0 * hmask[0] + s1 * hmask[1]

    r, k, v = (seq(shifted(p, q, mu)) for p, q, mu in
               ((pr_ref, qr_ref, mur_ref), (pk_ref, qk_ref, muk_ref), (pv_ref, qv_ref, muv_ref)))
    lora = shifted(pl_ref, ql_ref, mul_ref)
    xwa = lora[:, :LANES]
    xg = lora[:, LANES:]
    nw = -(w0_ref[...] + _mm3(jnp.tanh(xwa), w2_ref[...]))
    softplus = jnp.maximum(nw, 0.0) + jnp.log1p(jnp.exp(-jnp.abs(nw)))
    lw = seq(-jnp.exp(-softplus - 0.5))
    a = seq(jax.nn.sigmoid(a0_ref[...] + _mm3(xwa, a2_ref[...])))
    g = seq(_mm3(jax.nn.sigmoid(xg), g2_ref[...]))

    kkr = k * vec(kk_ref)
    kk = kkr * lax.rsqrt(jnp.maximum(head_sum(kkr * kkr), 1e-24))
    kmod = k * (1.0 + (a - 1.0) * vec(ka_ref))
    be = kk * a

    pos = _iota((P * tb, LANES), 0) & (C - 1)
    cw2 = lw.reshape(P * tb, LANES)
    sh = 1
    while sh < C:
        cw2 = cw2 + jnp.where(pos >= sh, pltpu.roll(cw2, sh, 0), 0.0)
        sh *= 2
    cw = cw2.reshape(P, tb, LANES)
    cl = jnp.concatenate(
        [jnp.broadcast_to(cw[:, (c + 1) * C - 1:(c + 1) * C, :], (P, C, LANES)) for c in range(nch)], axis=1)
    e_out = jnp.exp(-cw)
    e_rem = jnp.exp(cl - cw)
    rt = r * jnp.exp(cw)
    at = -kk * jnp.exp(cw - lw)
    kt = kmod * e_out
    bt = be * e_out
    kh = kmod * e_rem
    bh = be * e_rem

    def stack_masked(x):
        return jnp.concatenate(
            [x[:, c * C:(c + 1) * C] * hmask[h] for c in range(nch) for h in range(2)], axis=1)

    def stack_dup(x):
        return jnp.concatenate([x[:, c * C:(c + 1) * C] for c in range(nch) for h in range(2)], axis=1)

    sr = 2 * nch * C
    at_s = stack_masked(at)
    rt_s = stack_masked(rt)
    v_s = stack_masked(v)
    gram = _bmm_nt(jnp.concatenate([at_s, rt_s], axis=1),
                   jnp.concatenate([stack_dup(bt), stack_dup(kt)], axis=1))
    ui = _iota((sr, sr), 0)
    uj = _iota((sr, sr), 1)
    unit = _blk(ui, C) == _blk(uj, C)
    strict = unit & (ui > uj)
    incl = unit & (ui >= uj)
    a_ab = jnp.where(strict, gram[:, :sr, :sr], 0.0)
    a_ak = jnp.where(strict, gram[:, :sr, sr:], 0.0)
    a_rb = jnp.where(incl, gram[:, sr:, :sr], 0.0)
    a_rk = jnp.where(incl, gram[:, sr:, sr:], 0.0)

    x = (ui == uj).astype(F32) + jnp.where(_blk(ui, 2) == _blk(uj, 2), a_ab, 0.0)
    nblk = 2
    while nblk < C:
        e = jnp.where((_blk(ui, 2 * nblk) == _blk(uj, 2 * nblk)) & (_blk(ui, nblk) != _blk(uj, nblk)), a_ab, 0.0)
        x = x + _bmm(_bmm(x, e), x)
        nblk *= 2

    av = _bmm(jnp.concatenate([a_ak, a_rk], axis=1), v_s)
    ah_ul = _bmm(x, jnp.concatenate([at_s, av[:, :sr]], axis=-1))
    r_y = _bmm(a_rb, ah_ul)
    rh = rt_s + r_y[:, :, :LANES]
    yl = r_y[:, :, LANES:] + av[:, sr:]
    bh_s = stack_masked(bh)
    kh_s = stack_masked(kh)
    zero = jnp.zeros((P, 2 * C, LANES), F32)

    s = s_ref[...]
    ys = []
    for c in range(nch):
        u0 = slice(2 * c * C, (2 * c + 1) * C)
        u1 = slice((2 * c + 1) * C, (2 * c + 2) * C)
        uc = slice(2 * c * C, (2 * c + 2) * C)
        lhs_t = jnp.concatenate([_bt(bh_s[:, uc]), _bt(kh_s[:, uc])], axis=-1)
        rhs = jnp.concatenate([ah_ul[:, uc], jnp.concatenate([zero, v_s[:, uc]], axis=-1)], axis=1)
        pz = _bmm(lhs_t, rhs)
        wc = jnp.exp(cl[:, c * C:c * C + 1, :])
        pm = eye * wc + pz[:, :, :LANES]
        ys_s = _bmm(jnp.concatenate([rh[:, u0] + rh[:, u1], pm], axis=1), s)
        ys.append(ys_s[:, :C] + yl[:, u0] + yl[:, u1])
        s = ys_s[:, C:] + pz[:, :, LANES:]
    s_ref[...] = s
    y = jnp.concatenate(ys, axis=1)

    mu = head_sum(y, 1.0 / hd)
    yc = y - mu
    var = head_sum(yc * yc, 1.0 / hd)
    yn = yc * lax.rsqrt(var + RWKV_EPS) * vec(lng_ref) + vec(lnb_ref)
    bonus = head_sum(r * kmod * vec(rk_ref)) * v
    out = (yn + bonus) * g
    o_ref[0] = jnp.concatenate([out[p] for p in range(P)], axis=-1).astype(o_ref.dtype)


def rwkv_mix(pp, mu, w0, w2, a0, a2, g2, k_k, k_a, r_k, ln_g, ln_b, w, npair=4):
    b, tp, n = pp.shape
    tb = RWKV_TBLK
    wl = npair * LANES
    nl = n - 3 * w
    assert w % wl == 0 and (3 * w) % nl == 0 and nl == 3 * LANES
    nj = w // wl
    lblk = 3 * w // nl
    rows8 = tb // SUBLANES
    mu_p = jnp.zeros((1, n), F32).at[0, :mu.shape[0]].set(mu)
    w2_p = jnp.zeros((LANES, w), F32).at[:DECAY_LORA].set(w2)
    a2_p = jnp.zeros((LANES, w), F32).at[DECAY_LORA:DECAY_LORA + AAA_LORA].set(a2)
    g2_p = jnp.zeros((nl - LANES, w), F32).at[:GATE_LORA].set(g2)

    def cur(width, col):
        return pl.BlockSpec((1, tb, width), lambda bi, j, t: (bi, t, col(j)))

    def prev(width, col):
        return pl.BlockSpec((1, SUBLANES, width), lambda bi, j, t: (bi, jnp.maximum(t * rows8 - 1, 0), col(j)))

    def mu_spec(width, col):
        return pl.BlockSpec((1, width), lambda bi, j, t: (0, col(j)))

    cols = [lambda j: j, lambda j: nj + j, lambda j: 2 * nj + j]
    lcol = lambda j: lblk
    vec = pl.BlockSpec((1, wl), lambda bi, j, t: (0, j))
    mat = lambda rows: pl.BlockSpec((rows, wl), lambda bi, j, t: (0, j))
    in_specs = ([cur(wl, c) for c in cols] + [cur(nl, lcol)] + [prev(wl, c) for c in cols] + [prev(nl, lcol)]
                + [mu_spec(wl, c) for c in cols] + [mu_spec(nl, lcol)]
                + [vec, mat(LANES), vec, mat(LANES), mat(nl - LANES)] + [vec] * 5)
    return pl.pallas_call(
        functools.partial(_rwkv_scan_kernel, npair=npair),
        out_shape=jax.ShapeDtypeStruct((b, tp, w), BF16),
        grid=(b, nj, tp // tb),
        in_specs=in_specs,
        out_specs=pl.BlockSpec((1, tb, wl), lambda bi, j, t: (bi, t, j)),
        scratch_shapes=[pltpu.VMEM((npair, LANES, LANES), F32)],
        compiler_params=_cparams("parallel", "parallel", "arbitrary"),
        name="rwkv_mix",
    )(pp, pp, pp, pp, pp, pp, pp, pp, mu_p, mu_p, mu_p, mu_p,
      w0.reshape(1, w), w2_p, a0.reshape(1, w), a2_p, g2_p,
      k_k.reshape(1, w), k_a.reshape(1, w), r_k.reshape(1, w), ln_g.reshape(1, w), ln_b.reshape(1, w))


def _merge_kernel(o1_ref, o2_ref, o3_ref, wb_ref, g1_ref, g2_ref, g3_ref, out_ref):
    acc = g1_ref[...].astype(F32) * _dot(o1_ref[...], wb_ref[0])
    acc = acc + g2_ref[...].astype(F32) * _dot(o2_ref[...], wb_ref[1])
    acc = acc + g3_ref[...].astype(F32) * _dot(o3_ref[...], wb_ref[2])
    out_ref[...] = acc.astype(out_ref.dtype)


def merge(o1, o2, o3, w_branch, gates, tm=ROW_TILE, tn=1024):
    m, w = o1.shape
    d = w_branch.shape[2]
    nj = d // tn
    ospec = pl.BlockSpec((tm, w), lambda i, j: (i, 0))
    gspec = lambda b: pl.BlockSpec((tm, tn), lambda i, j, b=b: (i, b * nj + j))
    return pl.pallas_call(
        _merge_kernel,
        out_shape=jax.ShapeDtypeStruct((m, d), BF16),
        grid=(m // tm, nj),
        in_specs=[ospec, ospec, ospec, pl.BlockSpec((3, w, tn), lambda i, j: (0, 0, j)),
                  gspec(0), gspec(1), gspec(2)],
        out_specs=pl.BlockSpec((tm, tn), lambda i, j: (i, j)),
        compiler_params=_cparams("parallel", "arbitrary"),
        name="merge",
    )(o1, o2, o3, w_branch, gates, gates, gates)


def _pack_bf16_pair(hi, lo):
    uh = lax.bitcast_convert_type(hi.astype(BF16).astype(F32), jnp.uint32)
    ul = lax.bitcast_convert_type(lo.astype(BF16).astype(F32), jnp.uint32)
    return uh | lax.shift_right_logical(ul, jnp.uint32(16))


def _unpack_bf16_pair(w):
    hi = lax.bitcast_convert_type(w & jnp.uint32(0xFFFF0000), F32)
    lo = lax.bitcast_convert_type(lax.shift_left(w, jnp.uint32(16)), F32)
    return hi, lo


def _store_token_rows(ref, val):
    tm, n = val.shape
    seg = n // LANES
    for s in range(seg):
        ref[pl.ds(s, tm, stride=seg), :] = val[:, s * LANES:(s + 1) * LANES]


def _load_token_rows(ref, seg):
    tm = ref.shape[0] // seg
    return jnp.concatenate([ref[pl.ds(s, tm, stride=seg), :] for s in range(seg)], axis=-1)


def _start_token_gather(idx_ref, idx_base, n, src_hbm, dst_vmem, dst_base, sem, seg):
    def body(r, carry):
        t = pl.multiple_of(idx_ref[idx_base + r] * seg, seg)
        o = pl.multiple_of((dst_base + r) * seg, seg)
        pltpu.make_async_copy(src_hbm.at[pl.ds(t, seg), :], dst_vmem.at[pl.ds(o, seg), :], sem).start()
        return carry

    lax.fori_loop(0, n, body, 0, unroll=8)


def _wait_token_gather(n, src_hbm, dst_vmem, sem, seg):
    pltpu.make_async_copy(src_hbm.at[pl.ds(0, n * seg), :], dst_vmem.at[pl.ds(0, n * seg), :], sem).wait()


def _router_kernel(h_ref, g_ref, wr_ref, br_ref, m_ref, eid_ref, wt_ref):
    x = h_ref[...]
    m = x * lax.rsqrt(jnp.mean(x * x, axis=-1, keepdims=True) + RMS_EPS) * g_ref[...]
    half = m.shape[1] // 2
    _store_token_rows(m_ref, _pack_bf16_pair(m[:, :half], m[:, half:]))
    logits = _mm3(m, wr_ref[...]) + br_ref[...]
    lane = _iota(logits.shape, 1)
    lanef = lane.astype(F32)
    neg = -jnp.inf
    big = float(LANES)
    glog = jnp.where(lane < N_GROUPS, logits, neg)
    gmax = jnp.max(glog, axis=-1, keepdims=True)
    gsel = jnp.min(jnp.where(glog == gmax, lanef, big), axis=-1, keepdims=True)
    pg = 1.0 / jnp.sum(jnp.exp(glog - gmax), axis=-1, keepdims=True)
    lo = N_GROUPS + gsel * EXPERTS_PER_GROUP
    el = jnp.where((lanef >= lo) & (lanef < lo + EXPERTS_PER_GROUP), logits, neg)
    v1 = jnp.max(el, axis=-1, keepdims=True)
    i1 = jnp.min(jnp.where(el == v1, lanef, big), axis=-1, keepdims=True)
    el2 = jnp.where(lanef == i1, neg, el)
    v2 = jnp.max(el2, axis=-1, keepdims=True)
    i2 = jnp.min(jnp.where(el2 == v2, lanef, big), axis=-1, keepdims=True)
    e2 = jnp.exp(v2 - v1)
    w1 = pg / (1.0 + e2)
    w2 = pg * e2 / (1.0 + e2)
    eid = jnp.where(lane == 0, i1 - N_GROUPS, jnp.where(lane == 1, i2 - N_GROUPS, 0.0))
    eid_ref[...] = eid.astype(jnp.int32)
    wt_ref[...] = jnp.where(lane == 0, w1, jnp.where(lane == 1, w2, 0.0))


def router(h, g, wg, bg, we, be, tm=POINT_TILE):
    m, d = h.shape
    nr = N_GROUPS + N_EXPERTS
    wr = jnp.zeros((d, LANES), F32).at[:, :N_GROUPS].set(wg).at[:, N_GROUPS:nr].set(we)
    br = jnp.zeros((1, LANES), F32).at[0, :N_GROUPS].set(bg).at[0, N_GROUPS:nr].set(be)
    row = pl.BlockSpec((tm, d), lambda i: (i, 0))
    slab = pl.BlockSpec((tm, LANES), lambda i: (i, 0))
    seg = d // 2 // LANES
    return pl.pallas_call(
        _router_kernel,
        out_shape=[jax.ShapeDtypeStruct((m * seg, LANES), jnp.uint32), jax.ShapeDtypeStruct((m, LANES), jnp.int32),
                   jax.ShapeDtypeStruct((m, LANES), F32)],
        grid=(m // tm,),
        in_specs=[row, pl.BlockSpec((1, d), lambda i: (0, 0)), pl.BlockSpec((d, LANES), lambda i: (0, 0)),
                  pl.BlockSpec((1, LANES), lambda i: (0, 0))],
        out_specs=[pl.BlockSpec((tm * seg, LANES), lambda i: (i, 0)), slab, slab],
        compiler_params=_cparams("parallel"),
        name="moe_router",
    )(h, g.reshape(1, d), wr, br)


def _expert_kernel(te_ref, first_ref, nu_ref, src_ref, x_hbm, wg_ref, wu_ref, wd_ref, y_ref,
                   wgu_s, wd_s, xbuf, sem, *, nk, dk, tm):
    i = pl.program_id(0)
    kc = pl.program_id(1)
    nu = nu_ref[0]
    used = i < nu
    de = nk * dk
    seg = wd_s.shape[1] // 2 // LANES
    slot = lax.rem(i, 2)

    def start_gather(tile, s):
        _start_token_gather(src_ref, tile * tm, tm, x_hbm, xbuf.at[s], 0, sem.at[s], seg)

    @pl.when((i == 0) & (kc == 0) & used)
    def _():
        start_gather(0, 0)

    @pl.when((kc == nk - 1) & (i + 1 < nu))
    def _():
        start_gather(i + 1, 1 - slot)

    @pl.when(used & (first_ref[i] == 1))
    def _():
        for c in range(nk):
            @pl.when(kc == c)
            def _():
                wgu_s[:, c * dk:(c + 1) * dk] = wg_ref[0, 0].astype(BF16)
                wgu_s[:, de + c * dk:de + (c + 1) * dk] = wu_ref[0, 0].astype(BF16)
                wd_s[c * dk:(c + 1) * dk, :] = wd_ref[0, 0].astype(BF16)

    @pl.when(used & (kc == nk - 1))
    def _():
        _wait_token_gather(tm, x_hbm, xbuf.at[slot], sem.at[slot], seg)
        xh, xl = _unpack_bf16_pair(_load_token_rows(xbuf.at[slot], seg))
        x = jnp.concatenate([xh.astype(BF16), xl.astype(BF16)], axis=-1)
        gu = _dot(x, wgu_s[...])
        gt = gu[:, :de]
        act = (gt * jax.nn.sigmoid(gt) * gu[:, de:]).astype(BF16)
        y = _dot(act, wd_s[...])
        half = y.shape[1] // 2
        _store_token_rows(y_ref, _pack_bf16_pair(y[:, :half], y[:, half:]))

    @pl.when(jnp.logical_not(used) & (kc == nk - 1))
    def _():
        y_ref[...] = jnp.zeros_like(y_ref)


def expert_ffn(xm, src, tile_expert, tile_first, n_used, w_gate, w_up, w_down, layer, tm=MOE_TILE, nk=4):
    d = w_down.shape[3]
    seg = d // 2 // LANES
    p = src.shape[0]
    de = w_down.shape[2]
    dk = de // nk
    assert dk * nk == de and dk % LANES == 0

    def chunk(i, kc, first):
        return jnp.where(first[i] == 1, kc, nk - 1)

    grid_spec = pltpu.PrefetchScalarGridSpec(
        num_scalar_prefetch=4,
        grid=(p // tm, nk),
        in_specs=[pl.BlockSpec(memory_space=pl.ANY),
                  pl.BlockSpec((1, 1, d, dk), lambda i, kc, te, first, *_: (layer, te[i], 0, chunk(i, kc, first))),
                  pl.BlockSpec((1, 1, d, dk), lambda i, kc, te, first, *_: (layer, te[i], 0, chunk(i, kc, first))),
                  pl.BlockSpec((1, 1, dk, d), lambda i, kc, te, first, *_: (layer, te[i], chunk(i, kc, first), 0))],
        out_specs=pl.BlockSpec((tm * seg, LANES), lambda i, kc, *_: (i, 0)),
        scratch_shapes=[pltpu.VMEM((d, 2 * de), BF16), pltpu.VMEM((de, d), BF16),
                        pltpu.VMEM((2, tm * seg, LANES), jnp.uint32), pltpu.SemaphoreType.DMA((2,))],
    )
    return pl.pallas_call(
        functools.partial(_expert_kernel, nk=nk, dk=dk, tm=tm),
        out_shape=jax.ShapeDtypeStruct((p * seg, LANES), jnp.uint32),
        grid_spec=grid_spec,
        compiler_params=_cparams("arbitrary", "arbitrary"),
        name="moe_experts",
    )(tile_expert, tile_first, n_used, src, xm, w_gate, w_up, w_down)


def _combine_kernel(dest_ref, h_ref, ys_hbm, wt_ref, g_ref, *rest, emit_h, m):
    out_refs, (ybuf, sem) = rest[:-2], rest[-2:]
    n_ref = out_refs[-1]
    i = pl.program_id(0)
    slot = lax.rem(i, 2)
    tm, d = h_ref.shape
    half = d // 2
    seg = half // LANES

    def start_gather(step, s):
        for k in range(TOP_K):
            _start_token_gather(dest_ref, k * m + step * tm, tm, ys_hbm, ybuf.at[s], k * tm, sem.at[s], seg)

    @pl.when(i == 0)
    def _():
        start_gather(0, 0)

    @pl.when(i + 1 < pl.num_programs(0))
    def _():
        start_gather(i + 1, 1 - slot)

    _wait_token_gather(TOP_K * tm, ys_hbm, ybuf.at[slot], sem.at[slot], seg)
    wt = wt_ref[...]
    w0, w1 = wt[:, 0:1], wt[:, 1:2]
    yb = ybuf.at[slot]
    a0, b0 = _unpack_bf16_pair(_load_token_rows(yb.at[pl.ds(0, tm * seg), :], seg))
    a1, b1 = _unpack_bf16_pair(_load_token_rows(yb.at[pl.ds(tm * seg, tm * seg), :], seg))
    lo = h_ref[:, :half] + w0 * a0 + w1 * a1
    hi = h_ref[:, half:] + w0 * b0 + w1 * b1
    if emit_h:
        out_refs[0][:, :half] = lo
        out_refs[0][:, half:] = hi
    ms = (jnp.sum(lo * lo, axis=-1, keepdims=True) + jnp.sum(hi * hi, axis=-1, keepdims=True)) * (1.0 / d)
    scale = lax.rsqrt(ms + RMS_EPS) * wt[:, TOP_K:TOP_K + 1]
    n_ref[:, :half] = (lo * scale * g_ref[:, :half]).astype(n_ref.dtype)
    n_ref[:, half:] = (hi * scale * g_ref[:, half:]).astype(n_ref.dtype)


def combine(h, ys, dest, wt, g_next, norm_dtype, emit_h, blocks_per_seq, tm=POINT_TILE):
    m, d = h.shape
    seg = d // 2 // LANES
    if emit_h:
        row = pl.BlockSpec((tm, d), lambda i, dest: (i, 0))
        outs = [jax.ShapeDtypeStruct((m, d), F32), jax.ShapeDtypeStruct((m, d), norm_dtype)]
        out_specs = [row, row]
    else:
        tm = CHUNK
        row = pl.BlockSpec((tm, d), lambda i, dest: (i, 0))
        nb = blocks_per_seq
        outs = [jax.ShapeDtypeStruct((m // nb * (nb - 1), d), norm_dtype)]
        out_specs = [pl.BlockSpec((tm, d), lambda i, dest: ((i // nb) * (nb - 1) + jnp.maximum(i % nb - 1, 0), 0))]
    grid_spec = pltpu.PrefetchScalarGridSpec(
        num_scalar_prefetch=1,
        grid=(m // tm,),
        in_specs=[row, pl.BlockSpec(memory_space=pl.ANY), pl.BlockSpec((tm, LANES), lambda i, dest: (i, 0)),
                  pl.BlockSpec((1, d), lambda i, dest: (0, 0))],
        out_specs=out_specs,
        scratch_shapes=[pltpu.VMEM((2, TOP_K * tm * seg, LANES), jnp.uint32), pltpu.SemaphoreType.DMA((2,))],
    )
    return pl.pallas_call(
        functools.partial(_combine_kernel, emit_h=emit_h, m=m),
        out_shape=outs,
        grid_spec=grid_spec,
        compiler_params=_cparams("arbitrary"),
        name="moe_combine",
    )(dest, h, ys, wt, g_next.reshape(1, d))


def hier_moe(h, norm_g, wg, bg, we, be, w_gate, w_up, w_down, layer, g_next, norm_dtype, emit_h, row_valid,
             blocks_per_seq, tm=MOE_TILE):
    m, d = h.shape
    xm, eid_slab, wt = router(h, norm_g, wg, bg, we, be)
    wt = wt.at[:, TOP_K].set(row_valid)
    npair = m * TOP_K
    eid = eid_slab[:, :TOP_K].T.reshape(npair)
    onehot = (eid[:, None] == jnp.arange(N_EXPERTS, dtype=jnp.int32)[None, :]).astype(jnp.int32)
    counts = jnp.sum(onehot, axis=0)
    rank = jnp.sum((jnp.cumsum(onehot, axis=0) - 1) * onehot, axis=1)
    padded = ((counts + tm - 1) // tm) * tm
    ends = jnp.cumsum(padded)
    starts = ends - padded
    dest = jnp.sum(onehot * starts[None, :], axis=1) + rank
    p_rows = ((npair + tm - 1) // tm + N_EXPERTS) * tm
    pair_token = jnp.arange(npair, dtype=jnp.int32) % m
    src = jnp.zeros((p_rows,), jnp.int32).at[dest].set(pair_token, mode="promise_in_bounds",
                                                         unique_indices=True)
    n_tiles = p_rows // tm
    tile_start = jnp.arange(n_tiles, dtype=jnp.int32) * tm
    n_used = (ends[-1] // tm).astype(jnp.int32).reshape(1)
    last_used = jnp.maximum(n_used[0] - 1, 0) * tm
    tile_expert = jnp.sum((ends[None, :] <= jnp.minimum(tile_start, last_used)[:, None]).astype(jnp.int32), axis=1)
    tile_expert = jnp.minimum(tile_expert, N_EXPERTS - 1)
    tile_first = jnp.concatenate([jnp.ones((1,), jnp.int32),
                                  (tile_expert[1:] != tile_expert[:-1]).astype(jnp.int32)])
    ys = expert_ffn(xm, src, tile_expert, tile_first, n_used, w_gate, w_up, w_down, layer, tm)
    return combine(h, ys, dest.astype(jnp.int32), wt, g_next, norm_dtype, emit_h, blocks_per_seq)


def kernel(x, meta_tokens, final_norm_g, norm1_g, w_in, ret_gn_g, lru_conv_w, lru_conv_b, lru_wa, lru_ba,
           lru_wx, lru_bx, lru_lambda, rwkv_mu, rwkv_w0, rwkv_w2, rwkv_a0, rwkv_a2, rwkv_g2, rwkv_k_k,
           rwkv_k_a, rwkv_r_k, rwkv_ln_g, rwkv_ln_b, w_branch, w_out, norm2_g, moe_wg, moe_bg, moe_we,
           moe_be, moe_w_gate, moe_w_up, moe_w_down):
    b, seq, d = x.shape
    depth = w_in.shape[0]
    w = ret_gn_g.shape[1]
    t = N_META + seq
    tp = ((t + CHUNK - 1) // CHUNK) * CHUNK
    m = b * tp
    rwkv_cols = rwkv_mu.shape[1]
    rwkv_pad = 3 * w + LANES + 2 * LANES
    c_ret, c_lru, c_rwkv = 4 * w, 6 * w, 6 * w + rwkv_cols
    pad = tp - t
    row_valid = (jnp.arange(m, dtype=jnp.int32) % tp >= pad).astype(F32)

    h, a = embed(x, meta_tokens, norm1_g[0], pad)
    for l in range(depth):
        wl = w_in[l]
        w_ret = wl[:, :c_ret].astype(BF16)
        w_lru = wl[:, c_ret:c_lru].astype(BF16)
        w_rwkv = jnp.pad(wl[:, c_lru:c_rwkv], ((0, 0), (0, rwkv_pad - rwkv_cols))).astype(BF16)

        p_ret = matmul(a, w_ret, BF16, tm=ROW_TILE, tn=1024, name="proj_ret")
        p_lru = matmul(a, w_lru, F32, tm=ROW_TILE, tn=1024, name="proj_lru")
        p_rwkv = matmul(a, w_rwkv, F32, tm=ROW_TILE, tn=rwkv_pad // 3, name="proj_rwkv")
        gates = gate_matmul(a, wl[:, c_rwkv:], tm=ROW_TILE, tn=2048)

        o_ret = retention(p_ret.reshape(b, tp, 4 * w), ret_gn_g[l], pad)
        o_lru = rg_lru(p_lru.reshape(b, tp, 2 * w), lru_conv_w[l], lru_conv_b[l], lru_wa[l], lru_ba[l],
                       lru_wx[l], lru_bx[l], lru_lambda[l], pad)
        o_rwkv = rwkv_mix(p_rwkv.reshape(b, tp, rwkv_pad), rwkv_mu[l], rwkv_w0[l], rwkv_w2[l], rwkv_a0[l],
                          rwkv_a2[l], rwkv_g2[l], rwkv_k_k[l], rwkv_k_a[l], rwkv_r_k[l], rwkv_ln_g[l],
                          rwkv_ln_b[l], w)

        merged = merge(o_ret.reshape(m, w), o_lru.reshape(m, w), o_rwkv.reshape(m, w),
                       w_branch[l].astype(BF16), gates)
        h = matmul(merged, w_out[l].astype(BF16), F32, tm=ROW_TILE, tn=1024, residual=h, name="proj_out")

        moe_args = (h, norm2_g[l], moe_wg[l], moe_bg[l], moe_we[l], moe_be[l], moe_w_gate, moe_w_up, moe_w_down, l)
        if l + 1 < depth:
            h, a = hier_moe(*moe_args, norm1_g[l + 1], BF16, True, row_valid, tp // CHUNK)
        else:
            (out,) = hier_moe(*moe_args, final_norm_g, F32, False, row_valid, tp // CHUNK)

    return out.reshape(b, seq, d)
```

```python
import functools
import math

import jax
import jax.numpy as jnp
from jax import lax
from jax.experimental import pallas as pl
from jax.experimental.pallas import tpu as pltpu

F32 = jnp.float32
BF16 = jnp.bfloat16
FP8 = jnp.float8_e4m3fn
FP8_MAX = 448.0

LANES = 128
SUBLANES = 8
VMEM_BYTES_V7X = 64 * 1024 * 1024
VMEM_LIMIT = VMEM_BYTES_V7X - 8 * 1024 * 1024

N_META = 16
CHUNK = 128
RET_DIM = 128
ROPE_BASE = 10000.0
RET_EPS = 1e-5
LRU_BLOCK = 128
CONV_W = 4
LRU_C = 8.0
RWKV_DIM = 64
DECAY_LORA = 64
AAA_LORA = 64
GATE_LORA = 160
RWKV_EPS = 64e-5
N_GROUPS = 4
EXPERTS_PER_GROUP = 8
N_EXPERTS = N_GROUPS * EXPERTS_PER_GROUP
TOP_K = 2
RMS_EPS = 1e-6

RWKV_CHUNK = 64
RWKV_TBLK = 128
ROW_TILE = 512
RWKV_PROJ_TN = 768
POINT_TILE = 256
MOE_TILE = 512


def _cparams(*sem):
    return pltpu.CompilerParams(dimension_semantics=sem, vmem_limit_bytes=VMEM_LIMIT)


def _dot(a, b, precision=None):
    return jnp.dot(a, b, preferred_element_type=F32, precision=precision)


def _dot_nt(a, b, precision=None):
    return lax.dot_general(a, b, (((1,), (1,)), ((), ())), preferred_element_type=F32, precision=precision)


def _iota(shape, dim):
    return lax.broadcasted_iota(jnp.int32, shape, dim)


def _blk(i, n):
    shift = n.bit_length() - 1
    assert 1 << shift == n
    return lax.shift_right_logical(i, jnp.int32(shift))


def _embed_kernel(x_ref, meta_ref, g_ref, h_ref, a_ref, *, pad):
    i = pl.program_id(1)
    d = x_ref.shape[-1]
    head = jnp.concatenate([jnp.zeros((pad, d), F32), meta_ref[...]], axis=0)
    h = jnp.where(i == 0, head, x_ref[0])
    h_ref[...] = h
    a = h * lax.rsqrt(jnp.mean(h * h, axis=-1, keepdims=True) + RMS_EPS)
    a_ref[...] = (a * g_ref[...]).astype(a_ref.dtype)


def embed(x, meta, g, pad):
    b, seq, d = x.shape
    assert pad + meta.shape[0] == CHUNK and seq % CHUNK == 0
    nblk = seq // CHUNK + 1
    out = pl.BlockSpec((CHUNK, d), lambda bi, i: (bi * nblk + i, 0))
    return pl.pallas_call(
        functools.partial(_embed_kernel, pad=pad),
        out_shape=[jax.ShapeDtypeStruct((b * nblk * CHUNK, d), F32),
                   jax.ShapeDtypeStruct((b * nblk * CHUNK, d), BF16)],
        grid=(b, nblk),
        in_specs=[pl.BlockSpec((1, CHUNK, d), lambda bi, i: (bi, jnp.maximum(i - 1, 0), 0)),
                  pl.BlockSpec(meta.shape, lambda bi, i: (0, 0)), pl.BlockSpec((1, d), lambda bi, i: (0, 0))],
        out_specs=[out, out],
        compiler_params=_cparams("parallel", "arbitrary"),
        name="embed",
    )(x, meta.astype(F32), g.reshape(1, d))


def _matmul_kernel(a_ref, w_ref, *rest, has_res):
    o_ref, wb_ref = rest[-2:]

    @pl.when(pl.program_id(1) == 0)
    def _():
        wb_ref[...] = w_ref[0].astype(BF16)

    acc = _dot(a_ref[...], wb_ref[...])
    if has_res:
        acc = acc + rest[0][...]
    o_ref[...] = acc.astype(o_ref.dtype)


def matmul(a, w, layer, col0, n, out_dtype, *, tm, tn, residual=None, name="matmul"):
    m, k = a.shape
    assert m % tm == 0 and n % tn == 0 and col0 % tn == 0, (m, n, tm, tn, col0)
    jb = col0 // tn
    in_specs = [pl.BlockSpec((tm, k), lambda j, i: (i, 0)),
                pl.BlockSpec((1, k, tn), lambda j, i: (layer, 0, jb + j))]
    args = [a, w]
    if residual is not None:
        in_specs.append(pl.BlockSpec((tm, tn), lambda j, i: (i, j)))
        args.append(residual)
    return pl.pallas_call(
        functools.partial(_matmul_kernel, has_res=residual is not None),
        out_shape=jax.ShapeDtypeStruct((m, n), out_dtype),
        grid=(n // tn, m // tm),
        in_specs=in_specs,
        out_specs=pl.BlockSpec((tm, tn), lambda j, i: (i, j)),
        scratch_shapes=[pltpu.VMEM((k, tn), BF16)],
        compiler_params=_cparams("arbitrary", "arbitrary"),
        name=name,
    )(*args)


def _gate_matmul_kernel(a_ref, w_ref, sa_ref, o_ref, w8_ref, s_ref):
    @pl.when(pl.program_id(1) == 0)
    def _():
        wf = w_ref[...]
        amax = jnp.max(jnp.abs(wf), axis=0, keepdims=True)
        scale = jnp.where(amax > 0, FP8_MAX / amax, 1.0)
        w8_ref[...] = (wf * scale).astype(FP8)
        s_ref[...] = sa_ref[:, 0:1] / scale

    o_ref[...] = jax.nn.sigmoid(_dot(a_ref[...], w8_ref[...]) * s_ref[...]).astype(o_ref.dtype)


def gate_matmul(a, w, *, tm, tn):
    m, k = a.shape
    n = w.shape[1]
    af = a.astype(F32)
    amax = jnp.max(jnp.abs(af))
    scale = jnp.where(amax > 0, FP8_MAX / amax, 1.0)
    a8 = (af * scale).astype(FP8)
    inv_a = jnp.full((1, LANES), 1.0, F32) / scale
    return pl.pallas_call(
        _gate_matmul_kernel,
        out_shape=jax.ShapeDtypeStruct((m, n), BF16),
        grid=(n // tn, m // tm),
        in_specs=[pl.BlockSpec((tm, k), lambda j, i: (i, 0)), pl.BlockSpec((k, tn), lambda j, i: (0, j)),
                  pl.BlockSpec((1, LANES), lambda j, i: (0, 0))],
        out_specs=pl.BlockSpec((tm, tn), lambda j, i: (i, j)),
        scratch_shapes=[pltpu.VMEM((k, tn), FP8), pltpu.VMEM((1, tn), F32)],
        compiler_params=_cparams("arbitrary", "arbitrary"),
        name="proj_gates",
    )(a8, w, inv_a)


def _retention_kernel(q_ref, k_ref, v_ref, g_ref, cos_ref, sin_ref, gn_ref, o_ref, state_ref, *, heads):
    c = pl.program_id(1)

    @pl.when(c == 0)
    def _():
        state_ref[...] = jnp.zeros_like(state_ref)

    d = RET_DIM
    cos = cos_ref[...]
    sin = sin_ref[...]
    row = _iota((CHUNK, CHUNK), 0)
    col = _iota((CHUNK, CHUNK), 1)
    diff = (row - col).astype(F32)
    causal = row >= col
    idx = _iota((CHUNK, 1), 0).astype(F32)
    for h in range(heads):
        lg = math.log1p(-(2.0 ** (-5.0 - h)))
        sl = slice(h * d, (h + 1) * d)
        q = q_ref[0, :, sl].astype(F32)
        k = k_ref[0, :, sl].astype(F32)
        v = v_ref[0, :, sl].astype(BF16)
        g = g_ref[0, :, sl].astype(F32)
        qr = q * cos + pltpu.roll(q, d // 2, 1) * sin
        kr = (k * cos + pltpu.roll(k, d // 2, 1) * sin) * (d ** -0.5)
        decay = jnp.where(causal, jnp.exp(lg * jnp.maximum(diff, 0.0)), 0.0)
        scores = _dot_nt(qr.astype(BF16), kr.astype(BF16)) * decay
        o = _dot(scores.astype(BF16), v)
        xi = jnp.exp(lg * (idx + 1.0))
        st = state_ref[h]
        o = o + _dot((qr * xi).astype(BF16), st.astype(BF16))
        zeta = jnp.exp(lg * (CHUNK - 1.0 - idx))
        kv = _dot((kr * zeta).T.astype(BF16), v)
        state_ref[h] = math.exp(lg * CHUNK) * st + kv
        mu = jnp.mean(o, axis=-1, keepdims=True)
        oc = o - mu
        var = jnp.mean(oc * oc, axis=-1, keepdims=True)
        on = oc * lax.rsqrt(var + RET_EPS) * gn_ref[:, sl]
        o_ref[0, :, sl] = (g * jax.nn.sigmoid(g) * on).astype(o_ref.dtype)


def _rope_tables(tp, pad):
    half = RET_DIM // 2
    inv_freq = ROPE_BASE ** (-jnp.arange(half, dtype=F32) / half)
    ang = (jnp.arange(tp, dtype=F32) - pad)[:, None] * inv_freq[None, :]
    cos, sin = jnp.cos(ang), jnp.sin(ang)
    return jnp.concatenate([cos, cos], axis=-1), jnp.concatenate([-sin, sin], axis=-1)


def retention(pr, gn_g, pad):
    b, tp, w4 = pr.shape
    w = w4 // 4
    heads = w // RET_DIM
    cos, sin = _rope_tables(tp, pad)
    spec = lambda j: pl.BlockSpec((1, CHUNK, w), lambda bi, c, j=j: (bi, c, j))
    return pl.pallas_call(
        functools.partial(_retention_kernel, heads=heads),
        out_shape=jax.ShapeDtypeStruct((b, tp, w), BF16),
        grid=(b, tp // CHUNK),
        in_specs=[spec(0), spec(1), spec(2), spec(3),
                  pl.BlockSpec((CHUNK, RET_DIM), lambda bi, c: (c, 0)),
                  pl.BlockSpec((CHUNK, RET_DIM), lambda bi, c: (c, 0)),
                  pl.BlockSpec((1, w), lambda bi, c: (0, 0))],
        out_specs=pl.BlockSpec((1, CHUNK, w), lambda bi, c: (bi, c, 0)),
        scratch_shapes=[pltpu.VMEM((heads, RET_DIM, RET_DIM), F32)],
        compiler_params=_cparams("parallel", "arbitrary"),
        name="retention",
    )(pr, pr, pr, pr, cos, sin, gn_g.reshape(1, w))


def _lru_kernel(x_ref, y_ref, cw_ref, cb_ref, wa_ref, ba_ref, wx_ref, bx_ref, lam_ref, o_ref,
                xs_ref, h_ref, *, tc, pad):
    c = pl.program_id(1)
    w = x_ref.shape[-1]

    @pl.when(c == 0)
    def _():
        xs_ref[0:SUBLANES, :] = jnp.zeros((SUBLANES, w), F32)
        h_ref[...] = jnp.zeros_like(h_ref)

    x = x_ref[0].astype(F32)
    xs_ref[SUBLANES:, :] = x
    xc = cb_ref[...] + cw_ref[CONV_W - 1:CONV_W, :] * x
    for j in range(CONV_W - 1):
        off = SUBLANES - (CONV_W - 1) + j
        xc = xc + cw_ref[j:j + 1, :] * xs_ref[off:off + tc, :]
    xs_ref[0:SUBLANES, :] = x[tc - SUBLANES:, :]

    xcb = xc.astype(BF16)
    nb = w // LRU_BLOCK
    ra = jnp.concatenate(
        [_dot(xcb[:, n * LRU_BLOCK:(n + 1) * LRU_BLOCK], wa_ref[n]) for n in range(nb)], axis=-1)
    ri = jnp.concatenate(
        [_dot(xcb[:, n * LRU_BLOCK:(n + 1) * LRU_BLOCK], wx_ref[n]) for n in range(nb)], axis=-1)
    r = jax.nn.sigmoid(ra + ba_ref[...])
    i = jax.nn.sigmoid(ri + bx_ref[...])
    nlam = -lam_ref[...]
    softplus = jnp.maximum(nlam, 0.0) + jnp.log1p(jnp.exp(-jnp.abs(nlam)))
    log_a = -LRU_C * r * softplus
    a = jnp.exp(log_a)
    bb = jnp.sqrt(1.0 - jnp.exp(2.0 * log_a)) * (i * xc)
    row = _iota((tc, w), 0)
    bb = jnp.where((c > 0) | (row >= pad), bb, 0.0)

    s = 1
    while s < tc:
        a_sh = pltpu.roll(a, s, 0)
        b_sh = pltpu.roll(bb, s, 0)
        m = row >= s
        bb = jnp.where(m, a * b_sh + bb, bb)
        a = jnp.where(m, a * a_sh, a)
        s *= 2
    h = a * h_ref[...] + bb
    h_ref[...] = h[tc - 1:tc, :]
    y = y_ref[0].astype(F32)
    o_ref[0] = (h * jax.nn.gelu(y)).astype(o_ref.dtype)


def rg_lru(pl_xy, conv_w, conv_b, wa, ba, wx, bx, lam, pad, tc=256):
    b, tp, w2 = pl_xy.shape
    w = w2 // 2
    if tp % tc:
        tc = CHUNK
    assert pad <= tc
    nb = w // LRU_BLOCK
    vec = lambda: pl.BlockSpec((1, w), lambda bi, c: (0, 0))
    return pl.pallas_call(
        functools.partial(_lru_kernel, tc=tc, pad=pad),
        out_shape=jax.ShapeDtypeStruct((b, tp, w), BF16),
        grid=(b, tp // tc),
        in_specs=[pl.BlockSpec((1, tc, w), lambda bi, c: (bi, c, 0)),
                  pl.BlockSpec((1, tc, w), lambda bi, c: (bi, c, 1)),
                  pl.BlockSpec((CONV_W, w), lambda bi, c: (0, 0)), vec(),
                  pl.BlockSpec((nb, LRU_BLOCK, LRU_BLOCK), lambda bi, c: (0, 0, 0)), vec(),
                  pl.BlockSpec((nb, LRU_BLOCK, LRU_BLOCK), lambda bi, c: (0, 0, 0)), vec(), vec()],
        out_specs=pl.BlockSpec((1, tc, w), lambda bi, c: (bi, c, 0)),
        scratch_shapes=[pltpu.VMEM((tc + SUBLANES, w), F32), pltpu.VMEM((1, w), F32)],
        compiler_params=_cparams("parallel", "arbitrary"),
        name="rg_lru",
    )(pl_xy, pl_xy, conv_w, conv_b.reshape(1, w), wa.astype(BF16), ba.reshape(1, w),
      wx.astype(BF16), bx.reshape(1, w), lam.reshape(1, w))


def _split_bf16(x):
    hi = x.astype(BF16)
    return hi, (x - hi.astype(F32)).astype(BF16)


def _mm3(a, b):
    ah, al = _split_bf16(a)
    bh, bl = _split_bf16(b)
    return _dot(ah, bh) + (_dot(al, bh) + _dot(ah, bl))


def _bmm(a, b):
    return jnp.einsum("pmk,pkn->pmn", a.astype(BF16), b.astype(BF16), preferred_element_type=F32)


def _bmm_nt(a, b):
    return jnp.einsum("pmk,pnk->pmn", a.astype(BF16), b.astype(BF16), preferred_element_type=F32)


def _bt(x):
    return jnp.stack([x[p].T for p in range(x.shape[0])], axis=0)


def _rwkv_scan_kernel(pr_ref, pk_ref, pv_ref, pl_ref, qr_ref, qk_ref, qv_ref, ql_ref,
                      mur_ref, muk_ref, muv_ref, mul_ref, w0_ref, w2_ref, a0_ref, a2_ref, g2_ref,
                      kk_ref, ka_ref, rk_ref, lng_ref, lnb_ref, o_ref, s_ref, *, npair):
    n = pl.program_id(2)

    @pl.when(n == 0)
    def _():
        s_ref[...] = jnp.zeros_like(s_ref)

    P = npair
    tb = pr_ref.shape[1]
    C = RWKV_CHUNK
    nch = tb // C
    hd = RWKV_DIM

    def seq(x):
        return jnp.stack([x[:, p * LANES:(p + 1) * LANES] for p in range(P)], axis=0)

    def shifted(p_ref, q_ref, mu_ref):
        x = p_ref[0]
        prev = jnp.where(n == 0, 0.0, q_ref[0][SUBLANES - 1:SUBLANES, :])
        x_prev = jnp.where(_iota(x.shape, 0) == 0, prev, pltpu.roll(x, 1, 0))
        return x + (x_prev - x) * mu_ref[...]

    def vec(ref):
        x = ref[...]
        return jnp.stack([x[:, p * LANES:(p + 1) * LANES] for p in range(P)], axis=0)

    lane = _iota((1, 1, LANES), 2)
    hmask = [(lane < hd).astype(F32), (lane >= hd).astype(F32)]
    li = _iota((LANES, LANES), 0)
    lj = _iota((LANES, LANES), 1)
    eye = (li == lj).astype(F32)

    def head_sum(x, scale=1.0):
        s0 = jnp.sum(x * hmask[0], axis=-1, keepdims=True) * scale
        s1 = jnp.sum(x * hmask[1], axis=-1, keepdims=True) * scale
        return s0 * hmask[0] + s1 * hmask[1]

    r, k, v = (seq(shifted(p, q, mu)) for p, q, mu in
               ((pr_ref, qr_ref, mur_ref), (pk_ref, qk_ref, muk_ref), (pv_ref, qv_ref, muv_ref)))
    lora = shifted(pl_ref, ql_ref, mul_ref)
    xwa = lora[:, :LANES]
    xg = lora[:, LANES:]
    nw = -(w0_ref[...] + _mm3(jnp.tanh(xwa), w2_ref[...]))
    softplus = jnp.maximum(nw, 0.0) + jnp.log1p(jnp.exp(-jnp.abs(nw)))
    lw = seq(-jnp.exp(-softplus - 0.5))
    a = seq(jax.nn.sigmoid(a0_ref[...] + _mm3(xwa, a2_ref[...])))
    g = seq(_mm3(jax.nn.sigmoid(xg), g2_ref[...]))

    kkr = k * vec(kk_ref)
    kk = kkr * lax.rsqrt(jnp.maximum(head_sum(kkr * kkr), 1e-24))
    kmod = k * (1.0 + (a - 1.0) * vec(ka_ref))
    be = kk * a

    pos = _iota((P * tb, LANES), 0) & (C - 1)
    cw2 = lw.reshape(P * tb, LANES)
    sh = 1
    while sh < C:
        cw2 = cw2 + jnp.where(pos >= sh, pltpu.roll(cw2, sh, 0), 0.0)
        sh *= 2
    cw = cw2.reshape(P, tb, LANES)
    cl = jnp.concatenate(
        [jnp.broadcast_to(cw[:, (c + 1) * C - 1:(c + 1) * C, :], (P, C, LANES)) for c in range(nch)], axis=1)
    e_out = jnp.exp(-cw)
    e_rem = jnp.exp(cl - cw)
    rt = r * jnp.exp(cw)
    at = -kk * jnp.exp(cw - lw)
    kt = kmod * e_out
    bt = be * e_out
    kh = kmod * e_rem
    bh = be * e_rem

    def stack_masked(x):
        return jnp.concatenate(
            [x[:, c * C:(c + 1) * C] * hmask[h] for c in range(nch) for h in range(2)], axis=1)

    def stack_dup(x):
        return jnp.concatenate([x[:, c * C:(c + 1) * C] for c in range(nch) for h in range(2)], axis=1)

    sr = 2 * nch * C
    at_s = stack_masked(at)
    rt_s = stack_masked(rt)
    v_s = stack_masked(v)
    gram = _bmm_nt(jnp.concatenate([at_s, rt_s], axis=1),
                   jnp.concatenate([stack_dup(bt), stack_dup(kt)], axis=1))
    ui = _iota((sr, sr), 0)
    uj = _iota((sr, sr), 1)
    unit = _blk(ui, C) == _blk(uj, C)
    strict = unit & (ui > uj)
    incl = unit & (ui >= uj)
    a_ab = jnp.where(strict, gram[:, :sr, :sr], 0.0)
    a_ak = jnp.where(strict, gram[:, :sr, sr:], 0.0)
    a_rb = jnp.where(incl, gram[:, sr:, :sr], 0.0)
    a_rk = jnp.where(incl, gram[:, sr:, sr:], 0.0)

    x = (ui == uj).astype(F32) + jnp.where(_blk(ui, 2) == _blk(uj, 2), a_ab, 0.0)
    nblk = 2
    while nblk < C:
        e = jnp.where((_blk(ui, 2 * nblk) == _blk(uj, 2 * nblk)) & (_blk(ui, nblk) != _blk(uj, nblk)), a_ab, 0.0)
        x = x + _bmm(_bmm(x, e), x)
        nblk *= 2

    av = _bmm(jnp.concatenate([a_ak, a_rk], axis=1), v_s)
    ah_ul = _bmm(x, jnp.concatenate([at_s, av[:, :sr]], axis=-1))
    r_y = _bmm(a_rb, ah_ul)
    rh = rt_s + r_y[:, :, :LANES]
    yl = r_y[:, :, LANES:] + av[:, sr:]
    bh_s = stack_masked(bh)
    kh_s = stack_masked(kh)
    zero = jnp.zeros((P, 2 * C, LANES), F32)

    s = s_ref[...]
    ys = []
    for c in range(nch):
        u0 = slice(2 * c * C, (2 * c + 1) * C)
        u1 = slice((2 * c + 1) * C, (2 * c + 2) * C)
        uc = slice(2 * c * C, (2 * c + 2) * C)
        lhs_t = jnp.concatenate([_bt(bh_s[:, uc]), _bt(kh_s[:, uc])], axis=-1)
        rhs = jnp.concatenate([ah_ul[:, uc], jnp.concatenate([zero, v_s[:, uc]], axis=-1)], axis=1)
        pz = _bmm(lhs_t, rhs)
        wc = jnp.exp(cl[:, c * C:c * C + 1, :])
        pm = eye * wc + pz[:, :, :LANES]
        ys_s = _bmm(jnp.concatenate([rh[:, u0] + rh[:, u1], pm], axis=1), s)
        ys.append(ys_s[:, :C] + yl[:, u0] + yl[:, u1])
        s = ys_s[:, C:] + pz[:, :, LANES:]
    s_ref[...] = s
    y = jnp.concatenate(ys, axis=1)

    mu = head_sum(y, 1.0 / hd)
    yc = y - mu
    var = head_sum(yc * yc, 1.0 / hd)
    yn = yc * lax.rsqrt(var + RWKV_EPS) * vec(lng_ref) + vec(lnb_ref)
    bonus = head_sum(r * kmod * vec(rk_ref)) * v
    out = (yn + bonus) * g
    o_ref[0] = jnp.concatenate([out[p] for p in range(P)], axis=-1).astype(o_ref.dtype)


def rwkv_mix(pp, mu, w0, w2, a0, a2, g2, k_k, k_a, r_k, ln_g, ln_b, w, npair=4):
    b, tp, n = pp.shape
    tb = RWKV_TBLK
    wl = npair * LANES
    nl = 3 * LANES
    assert w % wl == 0 and (3 * w) % nl == 0 and n >= 3 * w + nl and mu.shape[0] <= 3 * w + nl
    nj = w // wl
    lblk = 3 * w // nl
    rows8 = tb // SUBLANES
    mu_p = jnp.zeros((1, n), F32).at[0, :mu.shape[0]].set(mu)
    w2_p = jnp.zeros((LANES, w), F32).at[:DECAY_LORA].set(w2)
    a2_p = jnp.zeros((LANES, w), F32).at[DECAY_LORA:DECAY_LORA + AAA_LORA].set(a2)
    g2_p = jnp.zeros((nl - LANES, w), F32).at[:GATE_LORA].set(g2)

    def cur(width, col):
        return pl.BlockSpec((1, tb, width), lambda bi, j, t: (bi, t, col(j)))

    def prev(width, col):
        return pl.BlockSpec((1, SUBLANES, width), lambda bi, j, t: (bi, jnp.maximum(t * rows8 - 1, 0), col(j)))

    def mu_spec(width, col):
        return pl.BlockSpec((1, width), lambda bi, j, t: (0, col(j)))

    cols = [lambda j: j, lambda j: nj + j, lambda j: 2 * nj + j]
    lcol = lambda j: lblk
    vec = pl.BlockSpec((1, wl), lambda bi, j, t: (0, j))
    mat = lambda rows: pl.BlockSpec((rows, wl), lambda bi, j, t: (0, j))
    in_specs = ([cur(wl, c) for c in cols] + [cur(nl, lcol)] + [prev(wl, c) for c in cols] + [prev(nl, lcol)]
                + [mu_spec(wl, c) for c in cols] + [mu_spec(nl, lcol)]
                + [vec, mat(LANES), vec, mat(LANES), mat(nl - LANES)] + [vec] * 5)
    return pl.pallas_call(
        functools.partial(_rwkv_scan_kernel, npair=npair),
        out_shape=jax.ShapeDtypeStruct((b, tp, w), BF16),
        grid=(b, nj, tp // tb),
        in_specs=in_specs,
        out_specs=pl.BlockSpec((1, tb, wl), lambda bi, j, t: (bi, t, j)),
        scratch_shapes=[pltpu.VMEM((npair, LANES, LANES), F32)],
        compiler_params=_cparams("parallel", "parallel", "arbitrary"),
        name="rwkv_mix",
    )(pp, pp, pp, pp, pp, pp, pp, pp, mu_p, mu_p, mu_p, mu_p,
      w0.reshape(1, w), w2_p, a0.reshape(1, w), a2_p, g2_p,
      k_k.reshape(1, w), k_a.reshape(1, w), r_k.reshape(1, w), ln_g.reshape(1, w), ln_b.reshape(1, w))


def _merge_kernel(o1_ref, o2_ref, o3_ref, w_ref, g1_ref, g2_ref, g3_ref, out_ref, wb_ref):
    @pl.when(pl.program_id(1) == 0)
    def _():
        wb_ref[...] = w_ref[0].astype(BF16)

    acc = g1_ref[...].astype(F32) * _dot(o1_ref[...], wb_ref[0])
    acc = acc + g2_ref[...].astype(F32) * _dot(o2_ref[...], wb_ref[1])
    acc = acc + g3_ref[...].astype(F32) * _dot(o3_ref[...], wb_ref[2])
    out_ref[...] = acc.astype(out_ref.dtype)


def merge(o1, o2, o3, w_branch, layer, gates, tm=ROW_TILE, tn=512):
    m, w = o1.shape
    nb, d = w_branch.shape[1], w_branch.shape[3]
    nj = d // tn
    ospec = pl.BlockSpec((tm, w), lambda j, i: (i, 0))
    gspec = lambda b: pl.BlockSpec((tm, tn), lambda j, i, b=b: (i, b * nj + j))
    return pl.pallas_call(
        _merge_kernel,
        out_shape=jax.ShapeDtypeStruct((m, d), BF16),
        grid=(nj, m // tm),
        in_specs=[ospec, ospec, ospec, pl.BlockSpec((1, nb, w, tn), lambda j, i: (layer, 0, 0, j)),
                  gspec(0), gspec(1), gspec(2)],
        out_specs=pl.BlockSpec((tm, tn), lambda j, i: (i, j)),
        scratch_shapes=[pltpu.VMEM((nb, w, tn), BF16)],
        compiler_params=_cparams("arbitrary", "arbitrary"),
        name="merge",
    )(o1, o2, o3, w_branch, gates, gates, gates)


def _pack_bf16_pair(hi, lo):
    uh = lax.bitcast_convert_type(hi.astype(BF16).astype(F32), jnp.uint32)
    ul = lax.bitcast_convert_type(lo.astype(BF16).astype(F32), jnp.uint32)
    return uh | lax.shift_right_logical(ul, jnp.uint32(16))


def _unpack_bf16_pair(w):
    hi = lax.bitcast_convert_type(w & jnp.uint32(0xFFFF0000), F32)
    lo = lax.bitcast_convert_type(lax.shift_left(w, jnp.uint32(16)), F32)
    return hi, lo


def _store_token_rows(ref, val):
    tm, n = val.shape
    seg = n // LANES
    for s in range(seg):
        ref[pl.ds(s, tm, stride=seg), :] = val[:, s * LANES:(s + 1) * LANES]


def _load_token_rows(ref, seg):
    tm = ref.shape[0] // seg
    return jnp.concatenate([ref[pl.ds(s, tm, stride=seg), :] for s in range(seg)], axis=-1)


def _start_token_gather(idx_ref, idx_base, n, src_hbm, dst_vmem, dst_base, sem, seg):
    def body(r, carry):
        t = pl.multiple_of(idx_ref[idx_base + r] * seg, seg)
        o = pl.multiple_of((dst_base + r) * seg, seg)
        pltpu.make_async_copy(src_hbm.at[pl.ds(t, seg), :], dst_vmem.at[pl.ds(o, seg), :], sem).start()
        return carry

    lax.fori_loop(0, n, body, 0, unroll=8)


def _wait_token_gather(n, src_hbm, dst_vmem, sem, seg):
    pltpu.make_async_copy(src_hbm.at[pl.ds(0, n * seg), :], dst_vmem.at[pl.ds(0, n * seg), :], sem).wait()


def _router_kernel(h_ref, g_ref, wr_ref, br_ref, m_ref, eid_ref, wt_ref):
    x = h_ref[...]
    m = x * lax.rsqrt(jnp.mean(x * x, axis=-1, keepdims=True) + RMS_EPS) * g_ref[...]
    half = m.shape[1] // 2
    _store_token_rows(m_ref, _pack_bf16_pair(m[:, :half], m[:, half:]))
    logits = _mm3(m, wr_ref[...]) + br_ref[...]
    lane = _iota(logits.shape, 1)
    lanef = lane.astype(F32)
    neg = -jnp.inf
    big = float(LANES)
    glog = jnp.where(lane < N_GROUPS, logits, neg)
    gmax = jnp.max(glog, axis=-1, keepdims=True)
    gsel = jnp.min(jnp.where(glog == gmax, lanef, big), axis=-1, keepdims=True)
    pg = 1.0 / jnp.sum(jnp.exp(glog - gmax), axis=-1, keepdims=True)
    lo = N_GROUPS + gsel * EXPERTS_PER_GROUP
    el = jnp.where((lanef >= lo) & (lanef < lo + EXPERTS_PER_GROUP), logits, neg)
    v1 = jnp.max(el, axis=-1, keepdims=True)
    i1 = jnp.min(jnp.where(el == v1, lanef, big), axis=-1, keepdims=True)
    el2 = jnp.where(lanef == i1, neg, el)
    v2 = jnp.max(el2, axis=-1, keepdims=True)
    i2 = jnp.min(jnp.where(el2 == v2, lanef, big), axis=-1, keepdims=True)
    e2 = jnp.exp(v2 - v1)
    w1 = pg / (1.0 + e2)
    w2 = pg * e2 / (1.0 + e2)
    eid = jnp.where(lane == 0, i1 - N_GROUPS, jnp.where(lane == 1, i2 - N_GROUPS, 0.0))
    eid_ref[...] = eid.astype(jnp.int32)
    wt_ref[...] = jnp.where(lane == 0, w1, jnp.where(lane == 1, w2, 0.0))


def router(h, g, wg, bg, we, be, tm=POINT_TILE):
    m, d = h.shape
    nr = N_GROUPS + N_EXPERTS
    wr = jnp.zeros((d, LANES), F32).at[:, :N_GROUPS].set(wg).at[:, N_GROUPS:nr].set(we)
    br = jnp.zeros((1, LANES), F32).at[0, :N_GROUPS].set(bg).at[0, N_GROUPS:nr].set(be)
    row = pl.BlockSpec((tm, d), lambda i: (i, 0))
    slab = pl.BlockSpec((tm, LANES), lambda i: (i, 0))
    seg = d // 2 // LANES
    return pl.pallas_call(
        _router_kernel,
        out_shape=[jax.ShapeDtypeStruct((m * seg, LANES), jnp.uint32), jax.ShapeDtypeStruct((m, LANES), jnp.int32),
                   jax.ShapeDtypeStruct((m, LANES), F32)],
        grid=(m // tm,),
        in_specs=[row, pl.BlockSpec((1, d), lambda i: (0, 0)), pl.BlockSpec((d, LANES), lambda i: (0, 0)),
                  pl.BlockSpec((1, LANES), lambda i: (0, 0))],
        out_specs=[pl.BlockSpec((tm * seg, LANES), lambda i: (i, 0)), slab, slab],
        compiler_params=_cparams("parallel"),
        name="moe_router",
    )(h, g.reshape(1, d), wr, br)


def _expert_kernel(te_ref, first_ref, nu_ref, src_ref, x_hbm, wg_ref, wu_ref, wd_ref, y_ref,
                   wgu_s, wd_s, xbuf, sem, *, nk, dk, tm):
    i = pl.program_id(0)
    kc = pl.program_id(1)
    nu = nu_ref[0]
    used = i < nu
    de = nk * dk
    seg = wd_s.shape[1] // 2 // LANES
    slot = lax.rem(i, 2)

    def start_gather(tile, s):
        _start_token_gather(src_ref, tile * tm, tm, x_hbm, xbuf.at[s], 0, sem.at[s], seg)

    @pl.when((i == 0) & (kc == 0) & used)
    def _():
        start_gather(0, 0)

    @pl.when((kc == nk - 1) & (i + 1 < nu))
    def _():
        start_gather(i + 1, 1 - slot)

    @pl.when(used & (first_ref[i] == 1))
    def _():
        for c in range(nk):
            @pl.when(kc == c)
            def _():
                wgu_s[:, c * dk:(c + 1) * dk] = wg_ref[0, 0].astype(BF16)
                wgu_s[:, de + c * dk:de + (c + 1) * dk] = wu_ref[0, 0].astype(BF16)
                wd_s[c * dk:(c + 1) * dk, :] = wd_ref[0, 0].astype(BF16)

    @pl.when(used & (kc == nk - 1))
    def _():
        _wait_token_gather(tm, x_hbm, xbuf.at[slot], sem.at[slot], seg)
        xh, xl = _unpack_bf16_pair(_load_token_rows(xbuf.at[slot], seg))
        x = jnp.concatenate([xh.astype(BF16), xl.astype(BF16)], axis=-1)
        gu = _dot(x, wgu_s[...])
        gt = gu[:, :de]
        act = (gt * jax.nn.sigmoid(gt) * gu[:, de:]).astype(BF16)
        y = _dot(act, wd_s[...])
        half = y.shape[1] // 2
        _store_token_rows(y_ref, _pack_bf16_pair(y[:, :half], y[:, half:]))

    @pl.when(jnp.logical_not(used) & (kc == nk - 1))
    def _():
        y_ref[...] = jnp.zeros_like(y_ref)


def expert_ffn(xm, src, tile_expert, tile_first, n_used, w_gate, w_up, w_down, layer, tm=MOE_TILE, nk=4):
    d = w_down.shape[3]
    seg = d // 2 // LANES
    p = src.shape[0]
    de = w_down.shape[2]
    dk = de // nk
    assert dk * nk == de and dk % LANES == 0

    def chunk(i, kc, first):
        return jnp.where(first[i] == 1, kc, nk - 1)

    grid_spec = pltpu.PrefetchScalarGridSpec(
        num_scalar_prefetch=4,
        grid=(p // tm, nk),
        in_specs=[pl.BlockSpec(memory_space=pl.ANY),
                  pl.BlockSpec((1, 1, d, dk), lambda i, kc, te, first, *_: (layer, te[i], 0, chunk(i, kc, first))),
                  pl.BlockSpec((1, 1, d, dk), lambda i, kc, te, first, *_: (layer, te[i], 0, chunk(i, kc, first))),
                  pl.BlockSpec((1, 1, dk, d), lambda i, kc, te, first, *_: (layer, te[i], chunk(i, kc, first), 0))],
        out_specs=pl.BlockSpec((tm * seg, LANES), lambda i, kc, *_: (i, 0)),
        scratch_shapes=[pltpu.VMEM((d, 2 * de), BF16), pltpu.VMEM((de, d), BF16),
                        pltpu.VMEM((2, tm * seg, LANES), jnp.uint32), pltpu.SemaphoreType.DMA((2,))],
    )
    return pl.pallas_call(
        functools.partial(_expert_kernel, nk=nk, dk=dk, tm=tm),
        out_shape=jax.ShapeDtypeStruct((p * seg, LANES), jnp.uint32),
        grid_spec=grid_spec,
        compiler_params=_cparams("arbitrary", "arbitrary"),
        name="moe_experts",
    )(tile_expert, tile_first, n_used, src, xm, w_gate, w_up, w_down)


def _combine_kernel(dest_ref, h_ref, ys_hbm, wt_ref, g_ref, *rest, emit_h, m):
    out_refs, (ybuf, sem) = rest[:-2], rest[-2:]
    n_ref = out_refs[-1]
    i = pl.program_id(0)
    slot = lax.rem(i, 2)
    tm, d = h_ref.shape
    half = d // 2
    seg = half // LANES

    def start_gather(step, s):
        for k in range(TOP_K):
            _start_token_gather(dest_ref, k * m + step * tm, tm, ys_hbm, ybuf.at[s], k * tm, sem.at[s], seg)

    @pl.when(i == 0)
    def _():
        start_gather(0, 0)

    @pl.when(i + 1 < pl.num_programs(0))
    def _():
        start_gather(i + 1, 1 - slot)

    _wait_token_gather(TOP_K * tm, ys_hbm, ybuf.at[slot], sem.at[slot], seg)
    wt = wt_ref[...]
    w0, w1 = wt[:, 0:1], wt[:, 1:2]
    yb = ybuf.at[slot]
    a0, b0 = _unpack_bf16_pair(_load_token_rows(yb.at[pl.ds(0, tm * seg), :], seg))
    a1, b1 = _unpack_bf16_pair(_load_token_rows(yb.at[pl.ds(tm * seg, tm * seg), :], seg))
    lo = h_ref[:, :half] + w0 * a0 + w1 * a1
    hi = h_ref[:, half:] + w0 * b0 + w1 * b1
    if emit_h:
        out_refs[0][:, :half] = lo
        out_refs[0][:, half:] = hi
    ms = (jnp.sum(lo * lo, axis=-1, keepdims=True) + jnp.sum(hi * hi, axis=-1, keepdims=True)) * (1.0 / d)
    scale = lax.rsqrt(ms + RMS_EPS) * wt[:, TOP_K:TOP_K + 1]
    n_ref[:, :half] = (lo * scale * g_ref[:, :half]).astype(n_ref.dtype)
    n_ref[:, half:] = (hi * scale * g_ref[:, half:]).astype(n_ref.dtype)


def combine(h, ys, dest, wt, g_next, norm_dtype, emit_h, blocks_per_seq, tm=POINT_TILE):
    m, d = h.shape
    seg = d // 2 // LANES
    if emit_h:
        row = pl.BlockSpec((tm, d), lambda i, dest: (i, 0))
        outs = [jax.ShapeDtypeStruct((m, d), F32), jax.ShapeDtypeStruct((m, d), norm_dtype)]
        out_specs = [row, row]
    else:
        tm = CHUNK
        row = pl.BlockSpec((tm, d), lambda i, dest: (i, 0))
        nb = blocks_per_seq
        outs = [jax.ShapeDtypeStruct((m // nb * (nb - 1), d), norm_dtype)]
        out_specs = [pl.BlockSpec((tm, d), lambda i, dest: ((i // nb) * (nb - 1) + jnp.maximum(i % nb - 1, 0), 0))]
    grid_spec = pltpu.PrefetchScalarGridSpec(
        num_scalar_prefetch=1,
        grid=(m // tm,),
        in_specs=[row, pl.BlockSpec(memory_space=pl.ANY), pl.BlockSpec((tm, LANES), lambda i, dest: (i, 0)),
                  pl.BlockSpec((1, d), lambda i, dest: (0, 0))],
        out_specs=out_specs,
        scratch_shapes=[pltpu.VMEM((2, TOP_K * tm * seg, LANES), jnp.uint32), pltpu.SemaphoreType.DMA((2,))],
    )
    return pl.pallas_call(
        functools.partial(_combine_kernel, emit_h=emit_h, m=m),
        out_shape=outs,
        grid_spec=grid_spec,
        compiler_params=_cparams("arbitrary"),
        name="moe_combine",
    )(dest, h, ys, wt, g_next.reshape(1, d))


def hier_moe(h, norm_g, wg, bg, we, be, w_gate, w_up, w_down, layer, g_next, norm_dtype, emit_h, row_valid,
             blocks_per_seq, tm=MOE_TILE):
    m, d = h.shape
    xm, eid_slab, wt = router(h, norm_g, wg, bg, we, be)
    wt = wt.at[:, TOP_K].set(row_valid)
    npair = m * TOP_K
    eid = eid_slab[:, :TOP_K].T.reshape(npair)
    onehot = (eid[:, None] == jnp.arange(N_EXPERTS, dtype=jnp.int32)[None, :]).astype(jnp.int32)
    counts = jnp.sum(onehot, axis=0)
    rank = jnp.sum((jnp.cumsum(onehot, axis=0) - 1) * onehot, axis=1)
    padded = ((counts + tm - 1) // tm) * tm
    ends = jnp.cumsum(padded)
    starts = ends - padded
    dest = jnp.sum(onehot * starts[None, :], axis=1) + rank
    p_rows = ((npair + tm - 1) // tm + N_EXPERTS) * tm
    pair_token = jnp.arange(npair, dtype=jnp.int32) % m
    src = jnp.zeros((p_rows,), jnp.int32).at[dest].set(pair_token, mode="promise_in_bounds",
                                                         unique_indices=True)
    n_tiles = p_rows // tm
    tile_start = jnp.arange(n_tiles, dtype=jnp.int32) * tm
    n_used = (ends[-1] // tm).astype(jnp.int32).reshape(1)
    last_used = jnp.maximum(n_used[0] - 1, 0) * tm
    tile_expert = jnp.sum((ends[None, :] <= jnp.minimum(tile_start, last_used)[:, None]).astype(jnp.int32), axis=1)
    tile_expert = jnp.minimum(tile_expert, N_EXPERTS - 1)
    tile_first = jnp.concatenate([jnp.ones((1,), jnp.int32),
                                  (tile_expert[1:] != tile_expert[:-1]).astype(jnp.int32)])
    ys = expert_ffn(xm, src, tile_expert, tile_first, n_used, w_gate, w_up, w_down, layer, tm)
    return combine(h, ys, dest.astype(jnp.int32), wt, g_next, norm_dtype, emit_h, blocks_per_seq)


def kernel(x, meta_tokens, final_norm_g, norm1_g, w_in, ret_gn_g, lru_conv_w, lru_conv_b, lru_wa, lru_ba,
           lru_wx, lru_bx, lru_lambda, rwkv_mu, rwkv_w0, rwkv_w2, rwkv_a0, rwkv_a2, rwkv_g2, rwkv_k_k,
           rwkv_k_a, rwkv_r_k, rwkv_ln_g, rwkv_ln_b, w_branch, w_out, norm2_g, moe_wg, moe_bg, moe_we,
           moe_be, moe_w_gate, moe_w_up, moe_w_down):
    b, seq, d = x.shape
    depth = w_in.shape[0]
    w = ret_gn_g.shape[1]
    t = N_META + seq
    tp = ((t + CHUNK - 1) // CHUNK) * CHUNK
    m = b * tp
    rwkv_cols = rwkv_mu.shape[1]
    rwkv_pad = -(-(3 * w + 3 * LANES) // RWKV_PROJ_TN) * RWKV_PROJ_TN
    c_ret, c_lru, c_rwkv = 4 * w, 6 * w, 6 * w + rwkv_cols
    pad = tp - t
    row_valid = (jnp.arange(m, dtype=jnp.int32) % tp >= pad).astype(F32)

    h, a = embed(x, meta_tokens, norm1_g[0], pad)
    for l in range(depth):
        p_ret = matmul(a, w_in, l, 0, c_ret, BF16, tm=ROW_TILE, tn=512, name="proj_ret")
        p_lru = matmul(a, w_in, l, c_ret, c_lru - c_ret, F32, tm=ROW_TILE, tn=512, name="proj_lru")
        p_rwkv = matmul(a, w_in, l, c_lru, rwkv_pad, F32, tm=ROW_TILE, tn=RWKV_PROJ_TN, name="proj_rwkv")
        gates = gate_matmul(a, w_in[l][:, c_rwkv:], tm=ROW_TILE, tn=1024)

        o_ret = retention(p_ret.reshape(b, tp, 4 * w), ret_gn_g[l], pad)
        o_lru = rg_lru(p_lru.reshape(b, tp, 2 * w), lru_conv_w[l], lru_conv_b[l], lru_wa[l], lru_ba[l],
                       lru_wx[l], lru_bx[l], lru_lambda[l], pad)
        o_rwkv = rwkv_mix(p_rwkv.reshape(b, tp, rwkv_pad), rwkv_mu[l], rwkv_w0[l], rwkv_w2[l], rwkv_a0[l],
                          rwkv_a2[l], rwkv_g2[l], rwkv_k_k[l], rwkv_k_a[l], rwkv_r_k[l], rwkv_ln_g[l],
                          rwkv_ln_b[l], w)

        merged = merge(o_ret.reshape(m, w), o_lru.reshape(m, w), o_rwkv.reshape(m, w), w_branch, l, gates)
        h = matmul(merged, w_out, l, 0, d, F32, tm=ROW_TILE, tn=512, residual=h, name="proj_out")

        moe_args = (h, norm2_g[l], moe_wg[l], moe_bg[l], moe_we[l], moe_be[l], moe_w_gate, moe_w_up, moe_w_down, l)
        if l + 1 < depth:
            h, a = hier_moe(*moe_args, norm1_g[l + 1], BF16, True, row_valid, tp // CHUNK)
        else:
            (out,) = hier_moe(*moe_args, final_norm_g, F32, False, row_valid, tp // CHUNK)

    return out.reshape(b, seq, d)
```

```python
import functools
import math

import jax
import jax.numpy as jnp
from jax import lax
from jax.experimental import pallas as pl
from jax.experimental.pallas import tpu as pltpu

F32 = jnp.float32
BF16 = jnp.bfloat16
FP8 = jnp.float8_e4m3fn
FP8_MAX = 448.0

LANES = 128
SUBLANES = 8
VMEM_BYTES_V7X = 64 * 1024 * 1024
VMEM_LIMIT = VMEM_BYTES_V7X - 8 * 1024 * 1024

N_META = 16
CHUNK = 128
RET_DIM = 128
ROPE_BASE = 10000.0
RET_EPS = 1e-5
LRU_BLOCK = 128
CONV_W = 4
LRU_C = 8.0
RWKV_DIM = 64
DECAY_LORA = 64
AAA_LORA = 64
GATE_LORA = 160
RWKV_EPS = 64e-5
N_GROUPS = 4
EXPERTS_PER_GROUP = 8
N_EXPERTS = N_GROUPS * EXPERTS_PER_GROUP
TOP_K = 2
RMS_EPS = 1e-6

RWKV_CHUNK = 64
RWKV_TBLK = 128
ROW_TILE = 512
POINT_TILE = 256
MOE_TILE = 512


def _cparams(*sem):
    return pltpu.CompilerParams(dimension_semantics=sem, vmem_limit_bytes=VMEM_LIMIT)


def _dot(a, b, precision=None):
    return jnp.dot(a, b, preferred_element_type=F32, precision=precision)


def _dot_nt(a, b, precision=None):
    return lax.dot_general(a, b, (((1,), (1,)), ((), ())), preferred_element_type=F32, precision=precision)


def _iota(shape, dim):
    return lax.broadcasted_iota(jnp.int32, shape, dim)


def _blk(i, n):
    shift = n.bit_length() - 1
    assert 1 << shift == n
    return lax.shift_right_logical(i, jnp.int32(shift))


def _embed_kernel(x_ref, meta_ref, g_ref, h_ref, a_ref, *, pad):
    i = pl.program_id(1)
    d = x_ref.shape[-1]
    head = jnp.concatenate([jnp.zeros((pad, d), F32), meta_ref[...]], axis=0)
    h = jnp.where(i == 0, head, x_ref[0])
    h_ref[...] = h
    a = h * lax.rsqrt(jnp.mean(h * h, axis=-1, keepdims=True) + RMS_EPS)
    a_ref[...] = (a * g_ref[...]).astype(a_ref.dtype)


def embed(x, meta, g, pad):
    b, seq, d = x.shape
    assert pad + meta.shape[0] == CHUNK and seq % CHUNK == 0
    nblk = seq // CHUNK + 1
    out = pl.BlockSpec((CHUNK, d), lambda bi, i: (bi * nblk + i, 0))
    return pl.pallas_call(
        functools.partial(_embed_kernel, pad=pad),
        out_shape=[jax.ShapeDtypeStruct((b * nblk * CHUNK, d), F32),
                   jax.ShapeDtypeStruct((b * nblk * CHUNK, d), BF16)],
        grid=(b, nblk),
        in_specs=[pl.BlockSpec((1, CHUNK, d), lambda bi, i: (bi, jnp.maximum(i - 1, 0), 0)),
                  pl.BlockSpec(meta.shape, lambda bi, i: (0, 0)), pl.BlockSpec((1, d), lambda bi, i: (0, 0))],
        out_specs=[out, out],
        compiler_params=_cparams("parallel", "arbitrary"),
        name="embed",
    )(x, meta.astype(F32), g.reshape(1, d))


def _matmul_kernel(a_ref, w_ref, o_ref, *, act):
    acc = _dot(a_ref[...], w_ref[...])
    if act == "sigmoid":
        acc = jax.nn.sigmoid(acc)
    o_ref[...] = acc.astype(o_ref.dtype)


def _matmul_fp8_sigmoid_kernel(a_ref, w_ref, s_ref, o_ref):
    o_ref[...] = jax.nn.sigmoid(_dot(a_ref[...], w_ref[...]) * s_ref[...]).astype(o_ref.dtype)


def _fp8_quantize(x, axis):
    amax = jnp.max(jnp.abs(x), axis=axis, keepdims=True)
    scale = jnp.where(amax > 0, FP8_MAX / amax, 1.0)
    return (x * scale).astype(FP8), 1.0 / scale


def gate_matmul(a, w, *, tm, tn):
    m, k = a.shape
    n = w.shape[1]
    a8, inv_a = _fp8_quantize(a.astype(F32), None)
    w8, inv_w = _fp8_quantize(w, 0)
    return pl.pallas_call(
        _matmul_fp8_sigmoid_kernel,
        out_shape=jax.ShapeDtypeStruct((m, n), BF16),
        grid=(m // tm, n // tn),
        in_specs=[pl.BlockSpec((tm, k), lambda i, j: (i, 0)), pl.BlockSpec((k, tn), lambda i, j: (0, j)),
                  pl.BlockSpec((1, tn), lambda i, j: (0, j))],
        out_specs=pl.BlockSpec((tm, tn), lambda i, j: (i, j)),
        compiler_params=_cparams("parallel", "arbitrary"),
        name="proj_gates",
    )(a8, w8, (inv_a.reshape(1, 1) * inv_w).astype(F32))


def _matmul_res_kernel(a_ref, w_ref, r_ref, o_ref):
    o_ref[...] = r_ref[...] + _dot(a_ref[...], w_ref[...])


def matmul(a, w, out_dtype, *, tm, tn, act=None, residual=None, name="matmul"):
    m, k = a.shape
    n = w.shape[1]
    assert m % tm == 0 and n % tn == 0, (m, n, tm, tn)
    in_specs = [pl.BlockSpec((tm, k), lambda i, j: (i, 0)), pl.BlockSpec((k, tn), lambda i, j: (0, j))]
    args = [a, w]
    if residual is None:
        body = functools.partial(_matmul_kernel, act=act)
    else:
        body = _matmul_res_kernel
        in_specs.append(pl.BlockSpec((tm, tn), lambda i, j: (i, j)))
        args.append(residual)
    return pl.pallas_call(
        body,
        out_shape=jax.ShapeDtypeStruct((m, n), out_dtype),
        grid=(m // tm, n // tn),
        in_specs=in_specs,
        out_specs=pl.BlockSpec((tm, tn), lambda i, j: (i, j)),
        compiler_params=_cparams("parallel", "arbitrary"),
        name=name,
    )(*args)


def _retention_kernel(q_ref, k_ref, v_ref, g_ref, cos_ref, sin_ref, gn_ref, o_ref, state_ref, *, heads):
    c = pl.program_id(1)

    @pl.when(c == 0)
    def _():
        state_ref[...] = jnp.zeros_like(state_ref)

    d = RET_DIM
    cos = cos_ref[...]
    sin = sin_ref[...]
    row = _iota((CHUNK, CHUNK), 0)
    col = _iota((CHUNK, CHUNK), 1)
    diff = (row - col).astype(F32)
    causal = row >= col
    idx = _iota((CHUNK, 1), 0).astype(F32)
    for h in range(heads):
        lg = math.log1p(-(2.0 ** (-5.0 - h)))
        sl = slice(h * d, (h + 1) * d)
        q = q_ref[0, :, sl].astype(F32)
        k = k_ref[0, :, sl].astype(F32)
        v = v_ref[0, :, sl].astype(BF16)
        g = g_ref[0, :, sl].astype(F32)
        qr = q * cos + pltpu.roll(q, d // 2, 1) * sin
        kr = (k * cos + pltpu.roll(k, d // 2, 1) * sin) * (d ** -0.5)
        decay = jnp.where(causal, jnp.exp(lg * jnp.maximum(diff, 0.0)), 0.0)
        scores = _dot_nt(qr.astype(BF16), kr.astype(BF16)) * decay
        o = _dot(scores.astype(BF16), v)
        xi = jnp.exp(lg * (idx + 1.0))
        st = state_ref[h]
        o = o + _dot((qr * xi).astype(BF16), st.astype(BF16))
        zeta = jnp.exp(lg * (CHUNK - 1.0 - idx))
        kv = _dot((kr * zeta).T.astype(BF16), v)
        state_ref[h] = math.exp(lg * CHUNK) * st + kv
        mu = jnp.mean(o, axis=-1, keepdims=True)
        oc = o - mu
        var = jnp.mean(oc * oc, axis=-1, keepdims=True)
        on = oc * lax.rsqrt(var + RET_EPS) * gn_ref[:, sl]
        o_ref[0, :, sl] = (g * jax.nn.sigmoid(g) * on).astype(o_ref.dtype)


def _rope_tables(tp, pad):
    half = RET_DIM // 2
    inv_freq = ROPE_BASE ** (-jnp.arange(half, dtype=F32) / half)
    ang = (jnp.arange(tp, dtype=F32) - pad)[:, None] * inv_freq[None, :]
    cos, sin = jnp.cos(ang), jnp.sin(ang)
    return jnp.concatenate([cos, cos], axis=-1), jnp.concatenate([-sin, sin], axis=-1)


def retention(pr, gn_g, pad):
    b, tp, w4 = pr.shape
    w = w4 // 4
    heads = w // RET_DIM
    cos, sin = _rope_tables(tp, pad)
    spec = lambda j: pl.BlockSpec((1, CHUNK, w), lambda bi, c, j=j: (bi, c, j))
    return pl.pallas_call(
        functools.partial(_retention_kernel, heads=heads),
        out_shape=jax.ShapeDtypeStruct((b, tp, w), BF16),
        grid=(b, tp // CHUNK),
        in_specs=[spec(0), spec(1), spec(2), spec(3),
                  pl.BlockSpec((CHUNK, RET_DIM), lambda bi, c: (c, 0)),
                  pl.BlockSpec((CHUNK, RET_DIM), lambda bi, c: (c, 0)),
                  pl.BlockSpec((1, w), lambda bi, c: (0, 0))],
        out_specs=pl.BlockSpec((1, CHUNK, w), lambda bi, c: (bi, c, 0)),
        scratch_shapes=[pltpu.VMEM((heads, RET_DIM, RET_DIM), F32)],
        compiler_params=_cparams("parallel", "arbitrary"),
        name="retention",
    )(pr, pr, pr, pr, cos, sin, gn_g.reshape(1, w))


def _lru_kernel(x_ref, y_ref, cw_ref, cb_ref, wa_ref, ba_ref, wx_ref, bx_ref, lam_ref, o_ref,
                xs_ref, h_ref, *, tc, pad):
    c = pl.program_id(1)
    w = x_ref.shape[-1]

    @pl.when(c == 0)
    def _():
        xs_ref[0:SUBLANES, :] = jnp.zeros((SUBLANES, w), F32)
        h_ref[...] = jnp.zeros_like(h_ref)

    x = x_ref[0].astype(F32)
    xs_ref[SUBLANES:, :] = x
    xc = cb_ref[...] + cw_ref[CONV_W - 1:CONV_W, :] * x
    for j in range(CONV_W - 1):
        off = SUBLANES - (CONV_W - 1) + j
        xc = xc + cw_ref[j:j + 1, :] * xs_ref[off:off + tc, :]
    xs_ref[0:SUBLANES, :] = x[tc - SUBLANES:, :]

    xcb = xc.astype(BF16)
    nb = w // LRU_BLOCK
    ra = jnp.concatenate(
        [_dot(xcb[:, n * LRU_BLOCK:(n + 1) * LRU_BLOCK], wa_ref[n]) for n in range(nb)], axis=-1)
    ri = jnp.concatenate(
        [_dot(xcb[:, n * LRU_BLOCK:(n + 1) * LRU_BLOCK], wx_ref[n]) for n in range(nb)], axis=-1)
    r = jax.nn.sigmoid(ra + ba_ref[...])
    i = jax.nn.sigmoid(ri + bx_ref[...])
    nlam = -lam_ref[...]
    softplus = jnp.maximum(nlam, 0.0) + jnp.log1p(jnp.exp(-jnp.abs(nlam)))
    log_a = -LRU_C * r * softplus
    a = jnp.exp(log_a)
    bb = jnp.sqrt(1.0 - jnp.exp(2.0 * log_a)) * (i * xc)
    row = _iota((tc, w), 0)
    bb = jnp.where((c > 0) | (row >= pad), bb, 0.0)

    s = 1
    while s < tc:
        a_sh = pltpu.roll(a, s, 0)
        b_sh = pltpu.roll(bb, s, 0)
        m = row >= s
        bb = jnp.where(m, a * b_sh + bb, bb)
        a = jnp.where(m, a * a_sh, a)
        s *= 2
    h = a * h_ref[...] + bb
    h_ref[...] = h[tc - 1:tc, :]
    y = y_ref[0].astype(F32)
    o_ref[0] = (h * jax.nn.gelu(y)).astype(o_ref.dtype)


def rg_lru(pl_xy, conv_w, conv_b, wa, ba, wx, bx, lam, pad, tc=256):
    b, tp, w2 = pl_xy.shape
    w = w2 // 2
    if tp % tc:
        tc = CHUNK
    assert pad <= tc
    nb = w // LRU_BLOCK
    vec = lambda: pl.BlockSpec((1, w), lambda bi, c: (0, 0))
    return pl.pallas_call(
        functools.partial(_lru_kernel, tc=tc, pad=pad),
        out_shape=jax.ShapeDtypeStruct((b, tp, w), BF16),
        grid=(b, tp // tc),
        in_specs=[pl.BlockSpec((1, tc, w), lambda bi, c: (bi, c, 0)),
                  pl.BlockSpec((1, tc, w), lambda bi, c: (bi, c, 1)),
                  pl.BlockSpec((CONV_W, w), lambda bi, c: (0, 0)), vec(),
                  pl.BlockSpec((nb, LRU_BLOCK, LRU_BLOCK), lambda bi, c: (0, 0, 0)), vec(),
                  pl.BlockSpec((nb, LRU_BLOCK, LRU_BLOCK), lambda bi, c: (0, 0, 0)), vec(), vec()],
        out_specs=pl.BlockSpec((1, tc, w), lambda bi, c: (bi, c, 0)),
        scratch_shapes=[pltpu.VMEM((tc + SUBLANES, w), F32), pltpu.VMEM((1, w), F32)],
        compiler_params=_cparams("parallel", "arbitrary"),
        name="rg_lru",
    )(pl_xy, pl_xy, conv_w, conv_b.reshape(1, w), wa.astype(BF16), ba.reshape(1, w),
      wx.astype(BF16), bx.reshape(1, w), lam.reshape(1, w))


def _split_bf16(x):
    hi = x.astype(BF16)
    return hi, (x - hi.astype(F32)).astype(BF16)


def _mm3(a, b):
    ah, al = _split_bf16(a)
    bh, bl = _split_bf16(b)
    return _dot(ah, bh) + (_dot(al, bh) + _dot(ah, bl))


def _bmm(a, b):
    return jnp.einsum("pmk,pkn->pmn", a.astype(BF16), b.astype(BF16), preferred_element_type=F32)


def _bmm_nt(a, b):
    return jnp.einsum("pmk,pnk->pmn", a.astype(BF16), b.astype(BF16), preferred_element_type=F32)


def _bt(x):
    return jnp.stack([x[p].T for p in range(x.shape[0])], axis=0)


def _rwkv_scan_kernel(pr_ref, pk_ref, pv_ref, pl_ref, qr_ref, qk_ref, qv_ref, ql_ref,
                      mur_ref, muk_ref, muv_ref, mul_ref, w0_ref, w2_ref, a0_ref, a2_ref, g2_ref,
                      kk_ref, ka_ref, rk_ref, lng_ref, lnb_ref, o_ref, s_ref, *, npair):
    n = pl.program_id(2)

    @pl.when(n == 0)
    def _():
        s_ref[...] = jnp.zeros_like(s_ref)

    P = npair
    tb = pr_ref.shape[1]
    C = RWKV_CHUNK
    nch = tb // C
    hd = RWKV_DIM

    def seq(x):
        return jnp.stack([x[:, p * LANES:(p + 1) * LANES] for p in range(P)], axis=0)

    def shifted(p_ref, q_ref, mu_ref):
        x = p_ref[0]
        prev = jnp.where(n == 0, 0.0, q_ref[0][SUBLANES - 1:SUBLANES, :])
        x_prev = jnp.where(_iota(x.shape, 0) == 0, prev, pltpu.roll(x, 1, 0))
        return x + (x_prev - x) * mu_ref[...]

    def vec(ref):
        x = ref[...]
        return jnp.stack([x[:, p * LANES:(p + 1) * LANES] for p in range(P)], axis=0)

    lane = _iota((1, 1, LANES), 2)
    hmask = [(lane < hd).astype(F32), (lane >= hd).astype(F32)]
    li = _iota((LANES, LANES), 0)
    lj = _iota((LANES, LANES), 1)
    eye = (li == lj).astype(F32)

    def head_sum(x, scale=1.0):
        s0 = jnp.sum(x * hmask[0], axis=-1, keepdims=True) * scale
        s1 = jnp.sum(x * hmask[1], axis=-1, keepdims=True) * scale
        return s0 * hmask[0] + s1 * hmask[1]

    r, k, v = (seq(shifted(p, q, mu)) for p, q, mu in
               ((pr_ref, qr_ref, mur_ref), (pk_ref, qk_ref, muk_ref), (pv_ref, qv_ref, muv_ref)))
    lora = shifted(pl_ref, ql_ref, mul_ref)
    xwa = lora[:, :LANES]
    xg = lora[:, LANES:]
    nw = -(w0_ref[...] + _mm3(jnp.tanh(xwa), w2_ref[...]))
    softplus = jnp.maximum(nw, 0.0) + jnp.log1p(jnp.exp(-jnp.abs(nw)))
    lw = seq(-jnp.exp(-softplus - 0.5))
    a = seq(jax.nn.sigmoid(a0_ref[...] + _mm3(xwa, a2_ref[...])))
    g = seq(_mm3(jax.nn.sigmoid(xg), g2_ref[...]))

    kkr = k * vec(kk_ref)
    kk = kkr * lax.rsqrt(jnp.maximum(head_sum(kkr * kkr), 1e-24))
    kmod = k * (1.0 + (a - 1.0) * vec(ka_ref))
    be = kk * a

    pos = _iota((P * tb, LANES), 0) & (C - 1)
    cw2 = lw.reshape(P * tb, LANES)
    sh = 1
    while sh < C:
        cw2 = cw2 + jnp.where(pos >= sh, pltpu.roll(cw2, sh, 0), 0.0)
        sh *= 2
    cw = cw2.reshape(P, tb, LANES)
    cl = jnp.concatenate(
        [jnp.broadcast_to(cw[:, (c + 1) * C - 1:(c + 1) * C, :], (P, C, LANES)) for c in range(nch)], axis=1)
    e_out = jnp.exp(-cw)
    e_rem = jnp.exp(cl - cw)
    rt = r * jnp.exp(cw)
    at = -kk * jnp.exp(cw - lw)
    kt = kmod * e_out
    bt = be * e_out
    kh = kmod * e_rem
    bh = be * e_rem

    def stack_masked(x):
        return jnp.concatenate(
            [x[:, c * C:(c + 1) * C] * hmask[h] for c in range(nch) for h in range(2)], axis=1)

    def stack_dup(x):
        return jnp.concatenate([x[:, c * C:(c + 1) * C] for c in range(nch) for h in range(2)], axis=1)

    sr = 2 * nch * C
    at_s = stack_masked(at)
    rt_s = stack_masked(rt)
    v_s = stack_masked(v)
    gram = _bmm_nt(jnp.concatenate([at_s, rt_s], axis=1),
                   jnp.concatenate([stack_dup(bt), stack_dup(kt)], axis=1))
    ui = _iota((sr, sr), 0)
    uj = _iota((sr, sr), 1)
    unit = _blk(ui, C) == _blk(uj, C)
    strict = unit & (ui > uj)
    incl = unit & (ui >= uj)
    a_ab = jnp.where(strict, gram[:, :sr, :sr], 0.0)
    a_ak = jnp.where(strict, gram[:, :sr, sr:], 0.0)
    a_rb = jnp.where(incl, gram[:, sr:, :sr], 0.0)
    a_rk = jnp.where(incl, gram[:, sr:, sr:], 0.0)

    x = (ui == uj).astype(F32) + jnp.where(_blk(ui, 2) == _blk(uj, 2), a_ab, 0.0)
    nblk = 2
    while nblk < C:
        e = jnp.where((_blk(ui, 2 * nblk) == _blk(uj, 2 * nblk)) & (_blk(ui, nblk) != _blk(uj, nblk)), a_ab, 0.0)
        x = x + _bmm(_bmm(x, e), x)
        nblk *= 2

    av = _bmm(jnp.concatenate([a_ak, a_rk], axis=1), v_s)
    ah_ul = _bmm(x, jnp.concatenate([at_s, av[:, :sr]], axis=-1))
    r_y = _bmm(a_rb, ah_ul)
    rh = rt_s + r_y[:, :, :LANES]
    yl = r_y[:, :, LANES:] + av[:, sr:]
    bh_s = stack_masked(bh)
    kh_s = stack_masked(kh)
    zero = jnp.zeros((P, 2 * C, LANES), F32)

    s = s_ref[...]
    ys = []
    for c in range(nch):
        u0 = slice(2 * c * C, (2 * c + 1) * C)
        u1 = slice((2 * c + 1) * C, (2 * c + 2) * C)
        uc = slice(2 * c * C, (2 * c + 2) * C)
        lhs_t = jnp.concatenate([_bt(bh_s[:, uc]), _bt(kh_s[:, uc])], axis=-1)
        rhs = jnp.concatenate([ah_ul[:, uc], jnp.concatenate([zero, v_s[:, uc]], axis=-1)], axis=1)
        pz = _bmm(lhs_t, rhs)
        wc = jnp.exp(cl[:, c * C:c * C + 1, :])
        pm = eye * wc + pz[:, :, :LANES]
        ys_s = _bmm(jnp.concatenate([rh[:, u0] + rh[:, u1], pm], axis=1), s)
        ys.append(ys_s[:, :C] + yl[:, u0] + yl[:, u1])
        s = ys_s[:, C:] + pz[:, :, LANES:]
    s_ref[...] = s
    y = jnp.concatenate(ys, axis=1)

    mu = head_sum(y, 1.0 / hd)
    yc = y - mu
    var = head_sum(yc * yc, 1.0 / hd)
    yn = yc * lax.rsqrt(var + RWKV_EPS) * vec(lng_ref) + vec(lnb_ref)
    bonus = head_sum(r * kmod * vec(rk_ref)) * v
    out = (yn + bonus) * g
    o_ref[0] = jnp.concatenate([out[p] for p in range(P)], axis=-1).astype(o_ref.dtype)


def rwkv_mix(pp, mu, w0, w2, a0, a2, g2, k_k, k_a, r_k, ln_g, ln_b, w, npair=8):
    b, tp, n = pp.shape
    tb = RWKV_TBLK
    wl = npair * LANES
    nl = n - 3 * w
    assert w % wl == 0 and (3 * w) % nl == 0 and nl == 3 * LANES
    nj = w // wl
    lblk = 3 * w // nl
    rows8 = tb // SUBLANES
    mu_p = jnp.zeros((1, n), F32).at[0, :mu.shape[0]].set(mu)
    w2_p = jnp.zeros((LANES, w), F32).at[:DECAY_LORA].set(w2)
    a2_p = jnp.zeros((LANES, w), F32).at[DECAY_LORA:DECAY_LORA + AAA_LORA].set(a2)
    g2_p = jnp.zeros((nl - LANES, w), F32).at[:GATE_LORA].set(g2)

    def cur(width, col):
        return pl.BlockSpec((1, tb, width), lambda bi, j, t: (bi, t, col(j)))

    def prev(width, col):
        return pl.BlockSpec((1, SUBLANES, width), lambda bi, j, t: (bi, jnp.maximum(t * rows8 - 1, 0), col(j)))

    def mu_spec(width, col):
        return pl.BlockSpec((1, width), lambda bi, j, t: (0, col(j)))

    cols = [lambda j: j, lambda j: nj + j, lambda j: 2 * nj + j]
    lcol = lambda j: lblk
    vec = pl.BlockSpec((1, wl), lambda bi, j, t: (0, j))
    mat = lambda rows: pl.BlockSpec((rows, wl), lambda bi, j, t: (0, j))
    in_specs = ([cur(wl, c) for c in cols] + [cur(nl, lcol)] + [prev(wl, c) for c in cols] + [prev(nl, lcol)]
                + [mu_spec(wl, c) for c in cols] + [mu_spec(nl, lcol)]
                + [vec, mat(LANES), vec, mat(LANES), mat(nl - LANES)] + [vec] * 5)
    return pl.pallas_call(
        functools.partial(_rwkv_scan_kernel, npair=npair),
        out_shape=jax.ShapeDtypeStruct((b, tp, w), BF16),
        grid=(b, nj, tp // tb),
        in_specs=in_specs,
        out_specs=pl.BlockSpec((1, tb, wl), lambda bi, j, t: (bi, t, j)),
        scratch_shapes=[pltpu.VMEM((npair, LANES, LANES), F32)],
        compiler_params=_cparams("parallel", "parallel", "arbitrary"),
        name="rwkv_mix",
    )(pp, pp, pp, pp, pp, pp, pp, pp, mu_p, mu_p, mu_p, mu_p,
      w0.reshape(1, w), w2_p, a0.reshape(1, w), a2_p, g2_p,
      k_k.reshape(1, w), k_a.reshape(1, w), r_k.reshape(1, w), ln_g.reshape(1, w), ln_b.reshape(1, w))


def _merge_kernel(o1_ref, o2_ref, o3_ref, wb_ref, g1_ref, g2_ref, g3_ref, out_ref):
    acc = g1_ref[...].astype(F32) * _dot(o1_ref[...], wb_ref[0])
    acc = acc + g2_ref[...].astype(F32) * _dot(o2_ref[...], wb_ref[1])
    acc = acc + g3_ref[...].astype(F32) * _dot(o3_ref[...], wb_ref[2])
    out_ref[...] = acc.astype(out_ref.dtype)


def merge(o1, o2, o3, w_branch, gates, tm=ROW_TILE, tn=1024):
    m, w = o1.shape
    d = w_branch.shape[2]
    nj = d // tn
    ospec = pl.BlockSpec((tm, w), lambda i, j: (i, 0))
    gspec = lambda b: pl.BlockSpec((tm, tn), lambda i, j, b=b: (i, b * nj + j))
    return pl.pallas_call(
        _merge_kernel,
        out_shape=jax.ShapeDtypeStruct((m, d), BF16),
        grid=(m // tm, nj),
        in_specs=[ospec, ospec, ospec, pl.BlockSpec((3, w, tn), lambda i, j: (0, 0, j)),
                  gspec(0), gspec(1), gspec(2)],
        out_specs=pl.BlockSpec((tm, tn), lambda i, j: (i, j)),
        compiler_params=_cparams("parallel", "arbitrary"),
        name="merge",
    )(o1, o2, o3, w_branch, gates, gates, gates)


def _pack_bf16_pair(hi, lo):
    uh = lax.bitcast_convert_type(hi.astype(BF16).astype(F32), jnp.uint32)
    ul = lax.bitcast_convert_type(lo.astype(BF16).astype(F32), jnp.uint32)
    return uh | lax.shift_right_logical(ul, jnp.uint32(16))


def _unpack_bf16_pair(w):
    hi = lax.bitcast_convert_type(w & jnp.uint32(0xFFFF0000), F32)
    lo = lax.bitcast_convert_type(lax.shift_left(w, jnp.uint32(16)), F32)
    return hi, lo


def _store_token_rows(ref, val):
    tm, n = val.shape
    seg = n // LANES
    for s in range(seg):
        ref[pl.ds(s, tm, stride=seg), :] = val[:, s * LANES:(s + 1) * LANES]


def _load_token_rows(ref, seg):
    tm = ref.shape[0] // seg
    return jnp.concatenate([ref[pl.ds(s, tm, stride=seg), :] for s in range(seg)], axis=-1)


def _start_token_gather(idx_ref, idx_base, n, src_hbm, dst_vmem, dst_base, sem, seg):
    def body(r, carry):
        t = pl.multiple_of(idx_ref[idx_base + r] * seg, seg)
        o = pl.multiple_of((dst_base + r) * seg, seg)
        pltpu.make_async_copy(src_hbm.at[pl.ds(t, seg), :], dst_vmem.at[pl.ds(o, seg), :], sem).start()
        return carry

    lax.fori_loop(0, n, body, 0, unroll=8)


def _wait_token_gather(n, src_hbm, dst_vmem, sem, seg):
    pltpu.make_async_copy(src_hbm.at[pl.ds(0, n * seg), :], dst_vmem.at[pl.ds(0, n * seg), :], sem).wait()


def _router_kernel(h_ref, g_ref, wr_ref, br_ref, m_ref, eid_ref, wt_ref):
    x = h_ref[...]
    m = x * lax.rsqrt(jnp.mean(x * x, axis=-1, keepdims=True) + RMS_EPS) * g_ref[...]
    half = m.shape[1] // 2
    _store_token_rows(m_ref, _pack_bf16_pair(m[:, :half], m[:, half:]))
    logits = _mm3(m, wr_ref[...]) + br_ref[...]
    lane = _iota(logits.shape, 1)
    lanef = lane.astype(F32)
    neg = -jnp.inf
    big = float(LANES)
    glog = jnp.where(lane < N_GROUPS, logits, neg)
    gmax = jnp.max(glog, axis=-1, keepdims=True)
    gsel = jnp.min(jnp.where(glog == gmax, lanef, big), axis=-1, keepdims=True)
    pg = 1.0 / jnp.sum(jnp.exp(glog - gmax), axis=-1, keepdims=True)
    lo = N_GROUPS + gsel * EXPERTS_PER_GROUP
    el = jnp.where((lanef >= lo) & (lanef < lo + EXPERTS_PER_GROUP), logits, neg)
    v1 = jnp.max(el, axis=-1, keepdims=True)
    i1 = jnp.min(jnp.where(el == v1, lanef, big), axis=-1, keepdims=True)
    el2 = jnp.where(lanef == i1, neg, el)
    v2 = jnp.max(el2, axis=-1, keepdims=True)
    i2 = jnp.min(jnp.where(el2 == v2, lanef, big), axis=-1, keepdims=True)
    e2 = jnp.exp(v2 - v1)
    w1 = pg / (1.0 + e2)
    w2 = pg * e2 / (1.0 + e2)
    eid = jnp.where(lane == 0, i1 - N_GROUPS, jnp.where(lane == 1, i2 - N_GROUPS, 0.0))
    eid_ref[...] = eid.astype(jnp.int32)
    wt_ref[...] = jnp.where(lane == 0, w1, jnp.where(lane == 1, w2, 0.0))


def router(h, g, wg, bg, we, be, tm=POINT_TILE):
    m, d = h.shape
    nr = N_GROUPS + N_EXPERTS
    wr = jnp.zeros((d, LANES), F32).at[:, :N_GROUPS].set(wg).at[:, N_GROUPS:nr].set(we)
    br = jnp.zeros((1, LANES), F32).at[0, :N_GROUPS].set(bg).at[0, N_GROUPS:nr].set(be)
    row = pl.BlockSpec((tm, d), lambda i: (i, 0))
    slab = pl.BlockSpec((tm, LANES), lambda i: (i, 0))
    seg = d // 2 // LANES
    return pl.pallas_call(
        _router_kernel,
        out_shape=[jax.ShapeDtypeStruct((m * seg, LANES), jnp.uint32), jax.ShapeDtypeStruct((m, LANES), jnp.int32),
                   jax.ShapeDtypeStruct((m, LANES), F32)],
        grid=(m // tm,),
        in_specs=[row, pl.BlockSpec((1, d), lambda i: (0, 0)), pl.BlockSpec((d, LANES), lambda i: (0, 0)),
                  pl.BlockSpec((1, LANES), lambda i: (0, 0))],
        out_specs=[pl.BlockSpec((tm * seg, LANES), lambda i: (i, 0)), slab, slab],
        compiler_params=_cparams("parallel"),
        name="moe_router",
    )(h, g.reshape(1, d), wr, br)


def _expert_kernel(te_ref, first_ref, nu_ref, src_ref, x_hbm, wg_ref, wu_ref, wd_ref, y_ref,
                   wgu_s, wd_s, xbuf, sem, *, nk, dk, tm):
    i = pl.program_id(0)
    kc = pl.program_id(1)
    nu = nu_ref[0]
    used = i < nu
    de = nk * dk
    seg = wd_s.shape[1] // 2 // LANES
    slot = lax.rem(i, 2)

    def start_gather(tile, s):
        _start_token_gather(src_ref, tile * tm, tm, x_hbm, xbuf.at[s], 0, sem.at[s], seg)

    @pl.when((i == 0) & (kc == 0) & used)
    def _():
        start_gather(0, 0)

    @pl.when((kc == nk - 1) & (i + 1 < nu))
    def _():
        start_gather(i + 1, 1 - slot)

    @pl.when(used & (first_ref[i] == 1))
    def _():
        for c in range(nk):
            @pl.when(kc == c)
            def _():
                wgu_s[:, c * dk:(c + 1) * dk] = wg_ref[0, 0].astype(BF16)
                wgu_s[:, de + c * dk:de + (c + 1) * dk] = wu_ref[0, 0].astype(BF16)
                wd_s[c * dk:(c + 1) * dk, :] = wd_ref[0, 0].astype(BF16)

    @pl.when(used & (kc == nk - 1))
    def _():
        _wait_token_gather(tm, x_hbm, xbuf.at[slot], sem.at[slot], seg)
        xh, xl = _unpack_bf16_pair(_load_token_rows(xbuf.at[slot], seg))
        x = jnp.concatenate([xh.astype(BF16), xl.astype(BF16)], axis=-1)
        gu = _dot(x, wgu_s[...])
        gt = gu[:, :de]
        act = (gt * jax.nn.sigmoid(gt) * gu[:, de:]).astype(BF16)
        y = _dot(act, wd_s[...])
        half = y.shape[1] // 2
        _store_token_rows(y_ref, _pack_bf16_pair(y[:, :half], y[:, half:]))

    @pl.when(jnp.logical_not(used) & (kc == nk - 1))
    def _():
        y_ref[...] = jnp.zeros_like(y_ref)


def expert_ffn(xm, src, tile_expert, tile_first, n_used, w_gate, w_up, w_down, layer, tm=MOE_TILE, nk=4):
    d = w_down.shape[3]
    seg = d // 2 // LANES
    p = src.shape[0]
    de = w_down.shape[2]
    dk = de // nk
    assert dk * nk == de and dk % LANES == 0

    def chunk(i, kc, first):
        return jnp.where(first[i] == 1, kc, nk - 1)

    grid_spec = pltpu.PrefetchScalarGridSpec(
        num_scalar_prefetch=4,
        grid=(p // tm, nk),
        in_specs=[pl.BlockSpec(memory_space=pl.ANY),
                  pl.BlockSpec((1, 1, d, dk), lambda i, kc, te, first, *_: (layer, te[i], 0, chunk(i, kc, first))),
                  pl.BlockSpec((1, 1, d, dk), lambda i, kc, te, first, *_: (layer, te[i], 0, chunk(i, kc, first))),
                  pl.BlockSpec((1, 1, dk, d), lambda i, kc, te, first, *_: (layer, te[i], chunk(i, kc, first), 0))],
        out_specs=pl.BlockSpec((tm * seg, LANES), lambda i, kc, *_: (i, 0)),
        scratch_shapes=[pltpu.VMEM((d, 2 * de), BF16), pltpu.VMEM((de, d), BF16),
                        pltpu.VMEM((2, tm * seg, LANES), jnp.uint32), pltpu.SemaphoreType.DMA((2,))],
    )
    return pl.pallas_call(
        functools.partial(_expert_kernel, nk=nk, dk=dk, tm=tm),
        out_shape=jax.ShapeDtypeStruct((p * seg, LANES), jnp.uint32),
        grid_spec=grid_spec,
        compiler_params=_cparams("arbitrary", "arbitrary"),
        name="moe_experts",
    )(tile_expert, tile_first, n_used, src, xm, w_gate, w_up, w_down)


def _combine_kernel(dest_ref, h_ref, ys_hbm, wt_ref, g_ref, *rest, emit_h, m):
    out_refs, (ybuf, sem) = rest[:-2], rest[-2:]
    n_ref = out_refs[-1]
    i = pl.program_id(0)
    slot = lax.rem(i, 2)
    tm, d = h_ref.shape
    half = d // 2
    seg = half // LANES

    def start_gather(step, s):
        for k in range(TOP_K):
            _start_token_gather(dest_ref, k * m + step * tm, tm, ys_hbm, ybuf.at[s], k * tm, sem.at[s], seg)

    @pl.when(i == 0)
    def _():
        start_gather(0, 0)

    @pl.when(i + 1 < pl.num_programs(0))
    def _():
        start_gather(i + 1, 1 - slot)

    _wait_token_gather(TOP_K * tm, ys_hbm, ybuf.at[slot], sem.at[slot], seg)
    wt = wt_ref[...]
    w0, w1 = wt[:, 0:1], wt[:, 1:2]
    yb = ybuf.at[slot]
    a0, b0 = _unpack_bf16_pair(_load_token_rows(yb.at[pl.ds(0, tm * seg), :], seg))
    a1, b1 = _unpack_bf16_pair(_load_token_rows(yb.at[pl.ds(tm * seg, tm * seg), :], seg))
    lo = h_ref[:, :half] + w0 * a0 + w1 * a1
    hi = h_ref[:, half:] + w0 * b0 + w1 * b1
    if emit_h:
        out_refs[0][:, :half] = lo
        out_refs[0][:, half:] = hi
    ms = (jnp.sum(lo * lo, axis=-1, keepdims=True) + jnp.sum(hi * hi, axis=-1, keepdims=True)) * (1.0 / d)
    scale = lax.rsqrt(ms + RMS_EPS) * wt[:, TOP_K:TOP_K + 1]
    n_ref[:, :half] = (lo * scale * g_ref[:, :half]).astype(n_ref.dtype)
    n_ref[:, half:] = (hi * scale * g_ref[:, half:]).astype(n_ref.dtype)


def combine(h, ys, dest, wt, g_next, norm_dtype, emit_h, blocks_per_seq, tm=POINT_TILE):
    m, d = h.shape
    seg = d // 2 // LANES
    if emit_h:
        row = pl.BlockSpec((tm, d), lambda i, dest: (i, 0))
        outs = [jax.ShapeDtypeStruct((m, d), F32), jax.ShapeDtypeStruct((m, d), norm_dtype)]
        out_specs = [row, row]
    else:
        tm = CHUNK
        row = pl.BlockSpec((tm, d), lambda i, dest: (i, 0))
        nb = blocks_per_seq
        outs = [jax.ShapeDtypeStruct((m // nb * (nb - 1), d), norm_dtype)]
        out_specs = [pl.BlockSpec((tm, d), lambda i, dest: ((i // nb) * (nb - 1) + jnp.maximum(i % nb - 1, 0), 0))]
    grid_spec = pltpu.PrefetchScalarGridSpec(
        num_scalar_prefetch=1,
        grid=(m // tm,),
        in_specs=[row, pl.BlockSpec(memory_space=pl.ANY), pl.BlockSpec((tm, LANES), lambda i, dest: (i, 0)),
                  pl.BlockSpec((1, d), lambda i, dest: (0, 0))],
        out_specs=out_specs,
        scratch_shapes=[pltpu.VMEM((2, TOP_K * tm * seg, LANES), jnp.uint32), pltpu.SemaphoreType.DMA((2,))],
    )
    return pl.pallas_call(
        functools.partial(_combine_kernel, emit_h=emit_h, m=m),
        out_shape=outs,
        grid_spec=grid_spec,
        compiler_params=_cparams("arbitrary"),
        name="moe_combine",
    )(dest, h, ys, wt, g_next.reshape(1, d))


def hier_moe(h, norm_g, wg, bg, we, be, w_gate, w_up, w_down, layer, g_next, norm_dtype, emit_h, row_valid,
             blocks_per_seq, tm=MOE_TILE):
    m, d = h.shape
    xm, eid_slab, wt = router(h, norm_g, wg, bg, we, be)
    wt = wt.at[:, TOP_K].set(row_valid)
    npair = m * TOP_K
    eid = eid_slab[:, :TOP_K].T.reshape(npair)
    onehot = (eid[:, None] == jnp.arange(N_EXPERTS, dtype=jnp.int32)[None, :]).astype(jnp.int32)
    counts = jnp.sum(onehot, axis=0)
    rank = jnp.sum((jnp.cumsum(onehot, axis=0) - 1) * onehot, axis=1)
    padded = ((counts + tm - 1) // tm) * tm
    ends = jnp.cumsum(padded)
    starts = ends - padded
    dest = jnp.sum(onehot * starts[None, :], axis=1) + rank
    p_rows = ((npair + tm - 1) // tm + N_EXPERTS) * tm
    pair_token = jnp.arange(npair, dtype=jnp.int32) % m
    src = jnp.zeros((p_rows,), jnp.int32).at[dest].set(pair_token, mode="promise_in_bounds",
                                                         unique_indices=True)
    n_tiles = p_rows // tm
    tile_start = jnp.arange(n_tiles, dtype=jnp.int32) * tm
    n_used = (ends[-1] // tm).astype(jnp.int32).reshape(1)
    last_used = jnp.maximum(n_used[0] - 1, 0) * tm
    tile_expert = jnp.sum((ends[None, :] <= jnp.minimum(tile_start, last_used)[:, None]).astype(jnp.int32), axis=1)
    tile_expert = jnp.minimum(tile_expert, N_EXPERTS - 1)
    tile_first = jnp.concatenate([jnp.ones((1,), jnp.int32),
                                  (tile_expert[1:] != tile_expert[:-1]).astype(jnp.int32)])
    ys = expert_ffn(xm, src, tile_expert, tile_first, n_used, w_gate, w_up, w_down, layer, tm)
    return combine(h, ys, dest.astype(jnp.int32), wt, g_next, norm_dtype, emit_h, blocks_per_seq)


def kernel(x, meta_tokens, final_norm_g, norm1_g, w_in, ret_gn_g, lru_conv_w, lru_conv_b, lru_wa, lru_ba,
           lru_wx, lru_bx, lru_lambda, rwkv_mu, rwkv_w0, rwkv_w2, rwkv_a0, rwkv_a2, rwkv_g2, rwkv_k_k,
           rwkv_k_a, rwkv_r_k, rwkv_ln_g, rwkv_ln_b, w_branch, w_out, norm2_g, moe_wg, moe_bg, moe_we,
           moe_be, moe_w_gate, moe_w_up, moe_w_down):
    b, seq, d = x.shape
    depth = w_in.shape[0]
    w = ret_gn_g.shape[1]
    t = N_META + seq
    tp = ((t + CHUNK - 1) // CHUNK) * CHUNK
    m = b * tp
    rwkv_cols = rwkv_mu.shape[1]
    rwkv_pad = 3 * w + LANES + 2 * LANES
    c_ret, c_lru, c_rwkv = 4 * w, 6 * w, 6 * w + rwkv_cols
    pad = tp - t
    row_valid = (jnp.arange(m, dtype=jnp.int32) % tp >= pad).astype(F32)

    h, a = embed(x, meta_tokens, norm1_g[0], pad)
    for l in range(depth):
        wl = w_in[l]
        w_ret = wl[:, :c_ret].astype(BF16)
        w_lru = wl[:, c_ret:c_lru].astype(BF16)
        w_rwkv = jnp.pad(wl[:, c_lru:c_rwkv], ((0, 0), (0, rwkv_pad - rwkv_cols))).astype(BF16)

        p_ret = matmul(a, w_ret, BF16, tm=ROW_TILE, tn=1024, name="proj_ret")
        p_lru = matmul(a, w_lru, F32, tm=ROW_TILE, tn=1024, name="proj_lru")
        p_rwkv = matmul(a, w_rwkv, F32, tm=ROW_TILE, tn=rwkv_pad // 3, name="proj_rwkv")
        gates = gate_matmul(a, wl[:, c_rwkv:], tm=ROW_TILE, tn=2048)

        o_ret = retention(p_ret.reshape(b, tp, 4 * w), ret_gn_g[l], pad)
        o_lru = rg_lru(p_lru.reshape(b, tp, 2 * w), lru_conv_w[l], lru_conv_b[l], lru_wa[l], lru_ba[l],
                       lru_wx[l], lru_bx[l], lru_lambda[l], pad)
        o_rwkv = rwkv_mix(p_rwkv.reshape(b, tp, rwkv_pad), rwkv_mu[l], rwkv_w0[l], rwkv_w2[l], rwkv_a0[l],
                          rwkv_a2[l], rwkv_g2[l], rwkv_k_k[l], rwkv_k_a[l], rwkv_r_k[l], rwkv_ln_g[l],
                          rwkv_ln_b[l], w)

        merged = merge(o_ret.reshape(m, w), o_lru.reshape(m, w), o_rwkv.reshape(m, w),
                       w_branch[l].astype(BF16), gates)
        h = matmul(merged, w_out[l].astype(BF16), F32, tm=ROW_TILE, tn=1024, residual=h, name="proj_out")

        moe_args = (h, norm2_g[l], moe_wg[l], moe_bg[l], moe_we[l], moe_be[l], moe_w_gate, moe_w_up, moe_w_down, l)
        if l + 1 < depth:
            h, a = hier_moe(*moe_args, norm1_g[l + 1], BF16, True, row_valid, tp // CHUNK)
        else:
            (out,) = hier_moe(*moe_args, final_norm_g, F32, False, row_valid, tp // CHUNK)

    return out.reshape(b, seq, d)
```

```python
import functools
import math

import jax
import jax.numpy as jnp
from jax import lax
from jax.experimental import pallas as pl
from jax.experimental.pallas import tpu as pltpu

F32 = jnp.float32
BF16 = jnp.bfloat16
FP8 = jnp.float8_e4m3fn
FP8_MAX = 448.0

LANES = 128
SUBLANES = 8
VMEM_BYTES_V7X = 64 * 1024 * 1024
VMEM_LIMIT = VMEM_BYTES_V7X - 8 * 1024 * 1024
DMA_PRIORITIES = 2

N_META = 16
CHUNK = 128
RET_DIM = 128
ROPE_BASE = 10000.0
RET_EPS = 1e-5
LRU_BLOCK = 128
CONV_W = 4
LRU_C = 8.0
RWKV_DIM = 64
DECAY_LORA = 64
AAA_LORA = 64
GATE_LORA = 160
RWKV_EPS = 64e-5
N_GROUPS = 4
EXPERTS_PER_GROUP = 8
N_EXPERTS = N_GROUPS * EXPERTS_PER_GROUP
TOP_K = 2
RMS_EPS = 1e-6

RWKV_CHUNK = 64
RWKV_TBLK = 128
ROW_TILE = 512
POINT_TILE = 256
MOE_TILE = 512


def _cparams(*sem):
    return pltpu.CompilerParams(dimension_semantics=sem, vmem_limit_bytes=VMEM_LIMIT)


def _dot(a, b, precision=None):
    return jnp.dot(a, b, preferred_element_type=F32, precision=precision)


def _dot_nt(a, b, precision=None):
    return lax.dot_general(a, b, (((1,), (1,)), ((), ())), preferred_element_type=F32, precision=precision)


def _iota(shape, dim):
    return lax.broadcasted_iota(jnp.int32, shape, dim)


def _blk(i, n):
    shift = n.bit_length() - 1
    assert 1 << shift == n
    return lax.shift_right_logical(i, jnp.int32(shift))


def _embed_kernel(x_ref, meta_ref, g_ref, h_ref, a_ref, *, pad):
    i = pl.program_id(1)
    d = x_ref.shape[-1]
    head = jnp.concatenate([jnp.zeros((pad, d), F32), meta_ref[...]], axis=0)
    h = jnp.where(i == 0, head, x_ref[0])
    h_ref[...] = h
    a = h * lax.rsqrt(jnp.mean(h * h, axis=-1, keepdims=True) + RMS_EPS)
    a_ref[...] = (a * g_ref[...]).astype(a_ref.dtype)


def embed(x, meta, g, pad):
    b, seq, d = x.shape
    assert pad + meta.shape[0] == CHUNK and seq % CHUNK == 0
    nblk = seq // CHUNK + 1
    out = pl.BlockSpec((CHUNK, d), lambda bi, i: (bi * nblk + i, 0))
    return pl.pallas_call(
        functools.partial(_embed_kernel, pad=pad),
        out_shape=[jax.ShapeDtypeStruct((b * nblk * CHUNK, d), F32),
                   jax.ShapeDtypeStruct((b * nblk * CHUNK, d), BF16)],
        grid=(b, nblk),
        in_specs=[pl.BlockSpec((1, CHUNK, d), lambda bi, i: (bi, jnp.maximum(i - 1, 0), 0)),
                  pl.BlockSpec(meta.shape, lambda bi, i: (0, 0)), pl.BlockSpec((1, d), lambda bi, i: (0, 0))],
        out_specs=[out, out],
        compiler_params=_cparams("parallel", "arbitrary"),
        name="embed",
    )(x, meta.astype(F32), g.reshape(1, d))


def _matmul_kernel(a_ref, w_ref, o_ref, *, act):
    acc = _dot(a_ref[...], w_ref[...])
    if act == "sigmoid":
        acc = jax.nn.sigmoid(acc)
    o_ref[...] = acc.astype(o_ref.dtype)


def _matmul_fp8_sigmoid_kernel(a_ref, w_ref, s_ref, o_ref):
    o_ref[...] = jax.nn.sigmoid(_dot(a_ref[...], w_ref[...]) * s_ref[...]).astype(o_ref.dtype)


def _fp8_quantize(x, axis):
    amax = jnp.max(jnp.abs(x), axis=axis, keepdims=True)
    scale = jnp.where(amax > 0, FP8_MAX / amax, 1.0)
    return (x * scale).astype(FP8), 1.0 / scale


def gate_matmul(a, w, *, tm, tn):
    m, k = a.shape
    n = w.shape[1]
    a8, inv_a = _fp8_quantize(a.astype(F32), None)
    w8, inv_w = _fp8_quantize(w, 0)
    return pl.pallas_call(
        _matmul_fp8_sigmoid_kernel,
        out_shape=jax.ShapeDtypeStruct((m, n), BF16),
        grid=(m // tm, n // tn),
        in_specs=[pl.BlockSpec((tm, k), lambda i, j: (i, 0)), pl.BlockSpec((k, tn), lambda i, j: (0, j)),
                  pl.BlockSpec((1, tn), lambda i, j: (0, j))],
        out_specs=pl.BlockSpec((tm, tn), lambda i, j: (i, j)),
        compiler_params=_cparams("parallel", "arbitrary"),
        name="proj_gates",
    )(a8, w8, (inv_a.reshape(1, 1) * inv_w).astype(F32))


def _matmul_res_kernel(a_ref, w_ref, r_ref, o_ref):
    o_ref[...] = r_ref[...] + _dot(a_ref[...], w_ref[...])


def matmul(a, w, out_dtype, *, tm, tn, act=None, residual=None, name="matmul"):
    m, k = a.shape
    n = w.shape[1]
    assert m % tm == 0 and n % tn == 0, (m, n, tm, tn)
    in_specs = [pl.BlockSpec((tm, k), lambda i, j: (i, 0)), pl.BlockSpec((k, tn), lambda i, j: (0, j))]
    args = [a, w]
    if residual is None:
        body = functools.partial(_matmul_kernel, act=act)
    else:
        body = _matmul_res_kernel
        in_specs.append(pl.BlockSpec((tm, tn), lambda i, j: (i, j)))
        args.append(residual)
    return pl.pallas_call(
        body,
        out_shape=jax.ShapeDtypeStruct((m, n), out_dtype),
        grid=(m // tm, n // tn),
        in_specs=in_specs,
        out_specs=pl.BlockSpec((tm, tn), lambda i, j: (i, j)),
        compiler_params=_cparams("parallel", "arbitrary"),
        name=name,
    )(*args)


def _retention_kernel(q_ref, k_ref, v_ref, g_ref, cos_ref, sin_ref, gn_ref, o_ref, state_ref, *, heads):
    c = pl.program_id(1)

    @pl.when(c == 0)
    def _():
        state_ref[...] = jnp.zeros_like(state_ref)

    d = RET_DIM
    cos = cos_ref[...]
    sin = sin_ref[...]
    row = _iota((CHUNK, CHUNK), 0)
    col = _iota((CHUNK, CHUNK), 1)
    diff = (row - col).astype(F32)
    causal = row >= col
    idx = _iota((CHUNK, 1), 0).astype(F32)
    for h in range(heads):
        lg = math.log1p(-(2.0 ** (-5.0 - h)))
        sl = slice(h * d, (h + 1) * d)
        q = q_ref[0, :, sl].astype(F32)
        k = k_ref[0, :, sl].astype(F32)
        v = v_ref[0, :, sl].astype(BF16)
        g = g_ref[0, :, sl].astype(F32)
        qr = q * cos + pltpu.roll(q, d // 2, 1) * sin
        kr = (k * cos + pltpu.roll(k, d // 2, 1) * sin) * (d ** -0.5)
        decay = jnp.where(causal, jnp.exp(lg * jnp.maximum(diff, 0.0)), 0.0)
        scores = _dot_nt(qr.astype(BF16), kr.astype(BF16)) * decay
        o = _dot(scores.astype(BF16), v)
        xi = jnp.exp(lg * (idx + 1.0))
        st = state_ref[h]
        o = o + _dot((qr * xi).astype(BF16), st.astype(BF16))
        zeta = jnp.exp(lg * (CHUNK - 1.0 - idx))
        kv = _dot((kr * zeta).T.astype(BF16), v)
        state_ref[h] = math.exp(lg * CHUNK) * st + kv
        mu = jnp.mean(o, axis=-1, keepdims=True)
        oc = o - mu
        var = jnp.mean(oc * oc, axis=-1, keepdims=True)
        on = oc * lax.rsqrt(var + RET_EPS) * gn_ref[:, sl]
        o_ref[0, :, sl] = (g * jax.nn.sigmoid(g) * on).astype(o_ref.dtype)


def _rope_tables(tp, pad):
    half = RET_DIM // 2
    inv_freq = ROPE_BASE ** (-jnp.arange(half, dtype=F32) / half)
    ang = (jnp.arange(tp, dtype=F32) - pad)[:, None] * inv_freq[None, :]
    cos, sin = jnp.cos(ang), jnp.sin(ang)
    return jnp.concatenate([cos, cos], axis=-1), jnp.concatenate([-sin, sin], axis=-1)


def retention(pr, gn_g, pad):
    b, tp, w4 = pr.shape
    w = w4 // 4
    heads = w // RET_DIM
    cos, sin = _rope_tables(tp, pad)
    spec = lambda j: pl.BlockSpec((1, CHUNK, w), lambda bi, c, j=j: (bi, c, j))
    return pl.pallas_call(
        functools.partial(_retention_kernel, heads=heads),
        out_shape=jax.ShapeDtypeStruct((b, tp, w), BF16),
        grid=(b, tp // CHUNK),
        in_specs=[spec(0), spec(1), spec(2), spec(3),
                  pl.BlockSpec((CHUNK, RET_DIM), lambda bi, c: (c, 0)),
                  pl.BlockSpec((CHUNK, RET_DIM), lambda bi, c: (c, 0)),
                  pl.BlockSpec((1, w), lambda bi, c: (0, 0))],
        out_specs=pl.BlockSpec((1, CHUNK, w), lambda bi, c: (bi, c, 0)),
        scratch_shapes=[pltpu.VMEM((heads, RET_DIM, RET_DIM), F32)],
        compiler_params=_cparams("parallel", "arbitrary"),
        name="retention",
    )(pr, pr, pr, pr, cos, sin, gn_g.reshape(1, w))


def _lru_kernel(x_ref, y_ref, cw_ref, cb_ref, wa_ref, ba_ref, wx_ref, bx_ref, lam_ref, o_ref,
                xs_ref, h_ref, *, tc, pad):
    c = pl.program_id(1)
    w = x_ref.shape[-1]

    @pl.when(c == 0)
    def _():
        xs_ref[0:SUBLANES, :] = jnp.zeros((SUBLANES, w), F32)
        h_ref[...] = jnp.zeros_like(h_ref)

    x = x_ref[0].astype(F32)
    xs_ref[SUBLANES:, :] = x
    xc = cb_ref[...] + cw_ref[CONV_W - 1:CONV_W, :] * x
    for j in range(CONV_W - 1):
        off = SUBLANES - (CONV_W - 1) + j
        xc = xc + cw_ref[j:j + 1, :] * xs_ref[off:off + tc, :]
    xs_ref[0:SUBLANES, :] = x[tc - SUBLANES:, :]

    xcb = xc.astype(BF16)
    nb = w // LRU_BLOCK
    ra = jnp.concatenate(
        [_dot(xcb[:, n * LRU_BLOCK:(n + 1) * LRU_BLOCK], wa_ref[n]) for n in range(nb)], axis=-1)
    ri = jnp.concatenate(
        [_dot(xcb[:, n * LRU_BLOCK:(n + 1) * LRU_BLOCK], wx_ref[n]) for n in range(nb)], axis=-1)
    r = jax.nn.sigmoid(ra + ba_ref[...])
    i = jax.nn.sigmoid(ri + bx_ref[...])
    nlam = -lam_ref[...]
    softplus = jnp.maximum(nlam, 0.0) + jnp.log1p(jnp.exp(-jnp.abs(nlam)))
    log_a = -LRU_C * r * softplus
    a = jnp.exp(log_a)
    bb = jnp.sqrt(1.0 - jnp.exp(2.0 * log_a)) * (i * xc)
    row = _iota((tc, w), 0)
    bb = jnp.where((c > 0) | (row >= pad), bb, 0.0)

    s = 1
    while s < tc:
        a_sh = pltpu.roll(a, s, 0)
        b_sh = pltpu.roll(bb, s, 0)
        m = row >= s
        bb = jnp.where(m, a * b_sh + bb, bb)
        a = jnp.where(m, a * a_sh, a)
        s *= 2
    h = a * h_ref[...] + bb
    h_ref[...] = h[tc - 1:tc, :]
    y = y_ref[0].astype(F32)
    o_ref[0] = (h * jax.nn.gelu(y)).astype(o_ref.dtype)


def rg_lru(pl_xy, conv_w, conv_b, wa, ba, wx, bx, lam, pad, tc=256):
    b, tp, w2 = pl_xy.shape
    w = w2 // 2
    if tp % tc:
        tc = CHUNK
    assert pad <= tc
    nb = w // LRU_BLOCK
    vec = lambda: pl.BlockSpec((1, w), lambda bi, c: (0, 0))
    return pl.pallas_call(
        functools.partial(_lru_kernel, tc=tc, pad=pad),
        out_shape=jax.ShapeDtypeStruct((b, tp, w), BF16),
        grid=(b, tp // tc),
        in_specs=[pl.BlockSpec((1, tc, w), lambda bi, c: (bi, c, 0)),
                  pl.BlockSpec((1, tc, w), lambda bi, c: (bi, c, 1)),
                  pl.BlockSpec((CONV_W, w), lambda bi, c: (0, 0)), vec(),
                  pl.BlockSpec((nb, LRU_BLOCK, LRU_BLOCK), lambda bi, c: (0, 0, 0)), vec(),
                  pl.BlockSpec((nb, LRU_BLOCK, LRU_BLOCK), lambda bi, c: (0, 0, 0)), vec(), vec()],
        out_specs=pl.BlockSpec((1, tc, w), lambda bi, c: (bi, c, 0)),
        scratch_shapes=[pltpu.VMEM((tc + SUBLANES, w), F32), pltpu.VMEM((1, w), F32)],
        compiler_params=_cparams("parallel", "arbitrary"),
        name="rg_lru",
    )(pl_xy, pl_xy, conv_w, conv_b.reshape(1, w), wa.astype(BF16), ba.reshape(1, w),
      wx.astype(BF16), bx.reshape(1, w), lam.reshape(1, w))


def _split_bf16(x):
    hi = x.astype(BF16)
    return hi, (x - hi.astype(F32)).astype(BF16)


def _mm3(a, b):
    ah, al = _split_bf16(a)
    bh, bl = _split_bf16(b)
    return _dot(ah, bh) + (_dot(al, bh) + _dot(ah, bl))


def _bmm(a, b):
    return jnp.einsum("pmk,pkn->pmn", a.astype(BF16), b.astype(BF16), preferred_element_type=F32)


def _bmm_nt(a, b):
    return jnp.einsum("pmk,pnk->pmn", a.astype(BF16), b.astype(BF16), preferred_element_type=F32)


def _bt(x):
    return jnp.stack([x[p].T for p in range(x.shape[0])], axis=0)


def _rwkv_scan_kernel(pr_ref, pk_ref, pv_ref, pl_ref, qr_ref, qk_ref, qv_ref, ql_ref,
                      mur_ref, muk_ref, muv_ref, mul_ref, w0_ref, w2_ref, a0_ref, a2_ref, g2_ref,
                      kk_ref, ka_ref, rk_ref, lng_ref, lnb_ref, o_ref, s_ref, *, npair):
    n = pl.program_id(2)

    @pl.when(n == 0)
    def _():
        s_ref[...] = jnp.zeros_like(s_ref)

    P = npair
    tb = pr_ref.shape[1]
    C = RWKV_CHUNK
    nch = tb // C
    hd = RWKV_DIM

    def seq(x):
        return jnp.stack([x[:, p * LANES:(p + 1) * LANES] for p in range(P)], axis=0)

    def shifted(p_ref, q_ref, mu_ref):
        x = p_ref[0]
        prev = jnp.where(n == 0, 0.0, q_ref[0][SUBLANES - 1:SUBLANES, :])
        x_prev = jnp.where(_iota(x.shape, 0) == 0, prev, pltpu.roll(x, 1, 0))
        return x + (x_prev - x) * mu_ref[...]

    def vec(ref):
        x = ref[...]
        return jnp.stack([x[:, p * LANES:(p + 1) * LANES] for p in range(P)], axis=0)

    lane = _iota((1, 1, LANES), 2)
    hmask = [(lane < hd).astype(F32), (lane >= hd).astype(F32)]
    li = _iota((LANES, LANES), 0)
    lj = _iota((LANES, LANES), 1)
    eye = (li == lj).astype(F32)

    def head_sum(x, scale=1.0):
        s0 = jnp.sum(x * hmask[0], axis=-1, keepdims=True) * scale
        s1 = jnp.sum(x * hmask[1], axis=-1, keepdims=True) * scale
        return s0 * hmask[0] + s1 * hmask[1]

    r, k, v = (seq(shifted(p, q, mu)) for p, q, mu in
               ((pr_ref, qr_ref, mur_ref), (pk_ref, qk_ref, muk_ref), (pv_ref, qv_ref, muv_ref)))
    lora = shifted(pl_ref, ql_ref, mul_ref)
    xwa = lora[:, :LANES]
    xg = lora[:, LANES:]
    nw = -(w0_ref[...] + _mm3(jnp.tanh(xwa), w2_ref[...]))
    softplus = jnp.maximum(nw, 0.0) + jnp.log1p(jnp.exp(-jnp.abs(nw)))
    lw = seq(-jnp.exp(-softplus - 0.5))
    a = seq(jax.nn.sigmoid(a0_ref[...] + _mm3(xwa, a2_ref[...])))
    g = seq(_mm3(jax.nn.sigmoid(xg), g2_ref[...]))

    kkr = k * vec(kk_ref)
    kk = kkr * lax.rsqrt(jnp.maximum(head_sum(kkr * kkr), 1e-24))
    kmod = k * (1.0 + (a - 1.0) * vec(ka_ref))
    be = kk * a

    pos = _iota((P * tb, LANES), 0) & (C - 1)
    cw2 = lw.reshape(P * tb, LANES)
    sh = 1
    while sh < C:
        cw2 = cw2 + jnp.where(pos >= sh, pltpu.roll(cw2, sh, 0), 0.0)
        sh *= 2
    cw = cw2.reshape(P, tb, LANES)
    cl = jnp.concatenate(
        [jnp.broadcast_to(cw[:, (c + 1) * C - 1:(c + 1) * C, :], (P, C, LANES)) for c in range(nch)], axis=1)
    e_out = jnp.exp(-cw)
    e_rem = jnp.exp(cl - cw)
    rt = r * jnp.exp(cw)
    at = -kk * jnp.exp(cw - lw)
    kt = kmod * e_out
    bt = be * e_out
    kh = kmod * e_rem
    bh = be * e_rem

    def stack_masked(x):
        return jnp.concatenate(
            [x[:, c * C:(c + 1) * C] * hmask[h] for c in range(nch) for h in range(2)], axis=1)

    def stack_dup(x):
        return jnp.concatenate([x[:, c * C:(c + 1) * C] for c in range(nch) for h in range(2)], axis=1)

    sr = 2 * nch * C
    at_s = stack_masked(at)
    rt_s = stack_masked(rt)
    v_s = stack_masked(v)
    gram = _bmm_nt(jnp.concatenate([at_s, rt_s], axis=1),
                   jnp.concatenate([stack_dup(bt), stack_dup(kt)], axis=1))
    ui = _iota((sr, sr), 0)
    uj = _iota((sr, sr), 1)
    unit = _blk(ui, C) == _blk(uj, C)
    strict = unit & (ui > uj)
    incl = unit & (ui >= uj)
    a_ab = jnp.where(strict, gram[:, :sr, :sr], 0.0)
    a_ak = jnp.where(strict, gram[:, :sr, sr:], 0.0)
    a_rb = jnp.where(incl, gram[:, sr:, :sr], 0.0)
    a_rk = jnp.where(incl, gram[:, sr:, sr:], 0.0)

    x = (ui == uj).astype(F32) + jnp.where(_blk(ui, 2) == _blk(uj, 2), a_ab, 0.0)
    nblk = 2
    while nblk < C:
        e = jnp.where((_blk(ui, 2 * nblk) == _blk(uj, 2 * nblk)) & (_blk(ui, nblk) != _blk(uj, nblk)), a_ab, 0.0)
        x = x + _bmm(_bmm(x, e), x)
        nblk *= 2

    av = _bmm(jnp.concatenate([a_ak, a_rk], axis=1), v_s)
    ah_ul = _bmm(x, jnp.concatenate([at_s, av[:, :sr]], axis=-1))
    r_y = _bmm(a_rb, ah_ul)
    rh = rt_s + r_y[:, :, :LANES]
    yl = r_y[:, :, LANES:] + av[:, sr:]
    bh_s = stack_masked(bh)
    kh_s = stack_masked(kh)
    zero = jnp.zeros((P, 2 * C, LANES), F32)

    s = s_ref[...]
    ys = []
    for c in range(nch):
        u0 = slice(2 * c * C, (2 * c + 1) * C)
        u1 = slice((2 * c + 1) * C, (2 * c + 2) * C)
        uc = slice(2 * c * C, (2 * c + 2) * C)
        lhs_t = jnp.concatenate([_bt(bh_s[:, uc]), _bt(kh_s[:, uc])], axis=-1)
        rhs = jnp.concatenate([ah_ul[:, uc], jnp.concatenate([zero, v_s[:, uc]], axis=-1)], axis=1)
        pz = _bmm(lhs_t, rhs)
        wc = jnp.exp(cl[:, c * C:c * C + 1, :])
        pm = eye * wc + pz[:, :, :LANES]
        ys_s = _bmm(jnp.concatenate([rh[:, u0] + rh[:, u1], pm], axis=1), s)
        ys.append(ys_s[:, :C] + yl[:, u0] + yl[:, u1])
        s = ys_s[:, C:] + pz[:, :, LANES:]
    s_ref[...] = s
    y = jnp.concatenate(ys, axis=1)

    mu = head_sum(y, 1.0 / hd)
    yc = y - mu
    var = head_sum(yc * yc, 1.0 / hd)
    yn = yc * lax.rsqrt(var + RWKV_EPS) * vec(lng_ref) + vec(lnb_ref)
    bonus = head_sum(r * kmod * vec(rk_ref)) * v
    out = (yn + bonus) * g
    o_ref[0] = jnp.concatenate([out[p] for p in range(P)], axis=-1).astype(o_ref.dtype)


def rwkv_mix(pp, mu, w0, w2, a0, a2, g2, k_k, k_a, r_k, ln_g, ln_b, w, npair=8):
    b, tp, n = pp.shape
    tb = RWKV_TBLK
    wl = npair * LANES
    nl = n - 3 * w
    assert w % wl == 0 and (3 * w) % nl == 0 and nl == 3 * LANES
    nj = w // wl
    lblk = 3 * w // nl
    rows8 = tb // SUBLANES
    mu_p = jnp.zeros((1, n), F32).at[0, :mu.shape[0]].set(mu)
    w2_p = jnp.zeros((LANES, w), F32).at[:DECAY_LORA].set(w2)
    a2_p = jnp.zeros((LANES, w), F32).at[DECAY_LORA:DECAY_LORA + AAA_LORA].set(a2)
    g2_p = jnp.zeros((nl - LANES, w), F32).at[:GATE_LORA].set(g2)

    def cur(width, col):
        return pl.BlockSpec((1, tb, width), lambda bi, j, t: (bi, t, col(j)))

    def prev(width, col):
        return pl.BlockSpec((1, SUBLANES, width), lambda bi, j, t: (bi, jnp.maximum(t * rows8 - 1, 0), col(j)))

    def mu_spec(width, col):
        return pl.BlockSpec((1, width), lambda bi, j, t: (0, col(j)))

    cols = [lambda j: j, lambda j: nj + j, lambda j: 2 * nj + j]
    lcol = lambda j: lblk
    vec = pl.BlockSpec((1, wl), lambda bi, j, t: (0, j))
    mat = lambda rows: pl.BlockSpec((rows, wl), lambda bi, j, t: (0, j))
    in_specs = ([cur(wl, c) for c in cols] + [cur(nl, lcol)] + [prev(wl, c) for c in cols] + [prev(nl, lcol)]
                + [mu_spec(wl, c) for c in cols] + [mu_spec(nl, lcol)]
                + [vec, mat(LANES), vec, mat(LANES), mat(nl - LANES)] + [vec] * 5)
    return pl.pallas_call(
        functools.partial(_rwkv_scan_kernel, npair=npair),
        out_shape=jax.ShapeDtypeStruct((b, tp, w), BF16),
        grid=(b, nj, tp // tb),
        in_specs=in_specs,
        out_specs=pl.BlockSpec((1, tb, wl), lambda bi, j, t: (bi, t, j)),
        scratch_shapes=[pltpu.VMEM((npair, LANES, LANES), F32)],
        compiler_params=_cparams("parallel", "parallel", "arbitrary"),
        name="rwkv_mix",
    )(pp, pp, pp, pp, pp, pp, pp, pp, mu_p, mu_p, mu_p, mu_p,
      w0.reshape(1, w), w2_p, a0.reshape(1, w), a2_p, g2_p,
      k_k.reshape(1, w), k_a.reshape(1, w), r_k.reshape(1, w), ln_g.reshape(1, w), ln_b.reshape(1, w))


def _merge_kernel(o1_ref, o2_ref, o3_ref, wb_ref, g1_ref, g2_ref, g3_ref, out_ref):
    acc = g1_ref[...].astype(F32) * _dot(o1_ref[...], wb_ref[0])
    acc = acc + g2_ref[...].astype(F32) * _dot(o2_ref[...], wb_ref[1])
    acc = acc + g3_ref[...].astype(F32) * _dot(o3_ref[...], wb_ref[2])
    out_ref[...] = acc.astype(out_ref.dtype)


def merge(o1, o2, o3, w_branch, gates, tm=ROW_TILE, tn=1024):
    m, w = o1.shape
    d = w_branch.shape[2]
    nj = d // tn
    ospec = pl.BlockSpec((tm, w), lambda i, j: (i, 0))
    gspec = lambda b: pl.BlockSpec((tm, tn), lambda i, j, b=b: (i, b * nj + j))
    return pl.pallas_call(
        _merge_kernel,
        out_shape=jax.ShapeDtypeStruct((m, d), BF16),
        grid=(m // tm, nj),
        in_specs=[ospec, ospec, ospec, pl.BlockSpec((3, w, tn), lambda i, j: (0, 0, j)),
                  gspec(0), gspec(1), gspec(2)],
        out_specs=pl.BlockSpec((tm, tn), lambda i, j: (i, j)),
        compiler_params=_cparams("parallel", "arbitrary"),
        name="merge",
    )(o1, o2, o3, w_branch, gates, gates, gates)


def _pack_bf16_pair(hi, lo):
    uh = lax.bitcast_convert_type(hi.astype(BF16).astype(F32), jnp.uint32)
    ul = lax.bitcast_convert_type(lo.astype(BF16).astype(F32), jnp.uint32)
    return uh | lax.shift_right_logical(ul, jnp.uint32(16))


def _unpack_bf16_pair(w):
    hi = lax.bitcast_convert_type(w & jnp.uint32(0xFFFF0000), F32)
    lo = lax.bitcast_convert_type(lax.shift_left(w, jnp.uint32(16)), F32)
    return hi, lo


def _store_token_rows(ref, val):
    tm, n = val.shape
    seg = n // LANES
    for s in range(seg):
        ref[pl.ds(s, tm, stride=seg), :] = val[:, s * LANES:(s + 1) * LANES]


def _load_token_rows(ref, seg):
    tm = ref.shape[0] // seg
    return jnp.concatenate([ref[pl.ds(s, tm, stride=seg), :] for s in range(seg)], axis=-1)


def _start_token_gather(idx_ref, idx_base, n, src_hbm, dst_vmem, dst_base, sem, seg):
    assert n % DMA_PRIORITIES == 0

    def body(r2, carry):
        for prio in range(DMA_PRIORITIES):
            r = r2 * DMA_PRIORITIES + prio
            t = pl.multiple_of(idx_ref[idx_base + r] * seg, seg)
            o = pl.multiple_of((dst_base + r) * seg, seg)
            pltpu.make_async_copy(src_hbm.at[pl.ds(t, seg), :], dst_vmem.at[pl.ds(o, seg), :],
                                  sem).start(priority=prio)
        return carry

    lax.fori_loop(0, n // DMA_PRIORITIES, body, 0, unroll=4)


def _wait_token_gather(n, src_hbm, dst_vmem, sem, seg):
    pltpu.make_async_copy(src_hbm.at[pl.ds(0, n * seg), :], dst_vmem.at[pl.ds(0, n * seg), :], sem).wait()


def _router_kernel(h_ref, g_ref, wr_ref, br_ref, m_ref, eid_ref, wt_ref):
    x = h_ref[...]
    m = x * lax.rsqrt(jnp.mean(x * x, axis=-1, keepdims=True) + RMS_EPS) * g_ref[...]
    half = m.shape[1] // 2
    _store_token_rows(m_ref, _pack_bf16_pair(m[:, :half], m[:, half:]))
    logits = _mm3(m, wr_ref[...]) + br_ref[...]
    lane = _iota(logits.shape, 1)
    lanef = lane.astype(F32)
    neg = -jnp.inf
    big = float(LANES)
    glog = jnp.where(lane < N_GROUPS, logits, neg)
    gmax = jnp.max(glog, axis=-1, keepdims=True)
    gsel = jnp.min(jnp.where(glog == gmax, lanef, big), axis=-1, keepdims=True)
    pg = 1.0 / jnp.sum(jnp.exp(glog - gmax), axis=-1, keepdims=True)
    lo = N_GROUPS + gsel * EXPERTS_PER_GROUP
    el = jnp.where((lanef >= lo) & (lanef < lo + EXPERTS_PER_GROUP), logits, neg)
    v1 = jnp.max(el, axis=-1, keepdims=True)
    i1 = jnp.min(jnp.where(el == v1, lanef, big), axis=-1, keepdims=True)
    el2 = jnp.where(lanef == i1, neg, el)
    v2 = jnp.max(el2, axis=-1, keepdims=True)
    i2 = jnp.min(jnp.where(el2 == v2, lanef, big), axis=-1, keepdims=True)
    e2 = jnp.exp(v2 - v1)
    w1 = pg / (1.0 + e2)
    w2 = pg * e2 / (1.0 + e2)
    eid = jnp.where(lane == 0, i1 - N_GROUPS, jnp.where(lane == 1, i2 - N_GROUPS, 0.0))
    eid_ref[...] = eid.astype(jnp.int32)
    wt_ref[...] = jnp.where(lane == 0, w1, jnp.where(lane == 1, w2, 0.0))


def router(h, g, wg, bg, we, be, tm=POINT_TILE):
    m, d = h.shape
    nr = N_GROUPS + N_EXPERTS
    wr = jnp.zeros((d, LANES), F32).at[:, :N_GROUPS].set(wg).at[:, N_GROUPS:nr].set(we)
    br = jnp.zeros((1, LANES), F32).at[0, :N_GROUPS].set(bg).at[0, N_GROUPS:nr].set(be)
    row = pl.BlockSpec((tm, d), lambda i: (i, 0))
    slab = pl.BlockSpec((tm, LANES), lambda i: (i, 0))
    seg = d // 2 // LANES
    return pl.pallas_call(
        _router_kernel,
        out_shape=[jax.ShapeDtypeStruct((m * seg, LANES), jnp.uint32), jax.ShapeDtypeStruct((m, LANES), jnp.int32),
                   jax.ShapeDtypeStruct((m, LANES), F32)],
        grid=(m // tm,),
        in_specs=[row, pl.BlockSpec((1, d), lambda i: (0, 0)), pl.BlockSpec((d, LANES), lambda i: (0, 0)),
                  pl.BlockSpec((1, LANES), lambda i: (0, 0))],
        out_specs=[pl.BlockSpec((tm * seg, LANES), lambda i: (i, 0)), slab, slab],
        compiler_params=_cparams("parallel"),
        name="moe_router",
    )(h, g.reshape(1, d), wr, br)


def _expert_kernel(te_ref, first_ref, nu_ref, src_ref, x_hbm, wg_ref, wu_ref, wd_ref, y_ref,
                   wgu_s, wd_s, xbuf, sem, *, nk, dk, tm):
    i = pl.program_id(0)
    kc = pl.program_id(1)
    nu = nu_ref[0]
    used = i < nu
    de = nk * dk
    seg = wd_s.shape[1] // 2 // LANES
    slot = lax.rem(i, 2)

    def start_gather(tile, s):
        _start_token_gather(src_ref, tile * tm, tm, x_hbm, xbuf.at[s], 0, sem.at[s], seg)

    @pl.when((i == 0) & (kc == 0) & used)
    def _():
        start_gather(0, 0)

    @pl.when((kc == nk - 1) & (i + 1 < nu))
    def _():
        start_gather(i + 1, 1 - slot)

    @pl.when(used & (first_ref[i] == 1))
    def _():
        for c in range(nk):
            @pl.when(kc == c)
            def _():
                wgu_s[:, c * dk:(c + 1) * dk] = wg_ref[0, 0].astype(BF16)
                wgu_s[:, de + c * dk:de + (c + 1) * dk] = wu_ref[0, 0].astype(BF16)
                wd_s[c * dk:(c + 1) * dk, :] = wd_ref[0, 0].astype(BF16)

    @pl.when(used & (kc == nk - 1))
    def _():
        _wait_token_gather(tm, x_hbm, xbuf.at[slot], sem.at[slot], seg)
        xh, xl = _unpack_bf16_pair(_load_token_rows(xbuf.at[slot], seg))
        x = jnp.concatenate([xh.astype(BF16), xl.astype(BF16)], axis=-1)
        gu = _dot(x, wgu_s[...])
        gt = gu[:, :de]
        act = (gt * jax.nn.sigmoid(gt) * gu[:, de:]).astype(BF16)
        y = _dot(act, wd_s[...])
        half = y.shape[1] // 2
        _store_token_rows(y_ref, _pack_bf16_pair(y[:, :half], y[:, half:]))

    @pl.when(jnp.logical_not(used) & (kc == nk - 1))
    def _():
        y_ref[...] = jnp.zeros_like(y_ref)


def expert_ffn(xm, src, tile_expert, tile_first, n_used, w_gate, w_up, w_down, layer, tm=MOE_TILE, nk=4):
    d = w_down.shape[3]
    seg = d // 2 // LANES
    p = src.shape[0]
    de = w_down.shape[2]
    dk = de // nk
    assert dk * nk == de and dk % LANES == 0

    def chunk(i, kc, first):
        return jnp.where(first[i] == 1, kc, nk - 1)

    grid_spec = pltpu.PrefetchScalarGridSpec(
        num_scalar_prefetch=4,
        grid=(p // tm, nk),
        in_specs=[pl.BlockSpec(memory_space=pl.ANY),
                  pl.BlockSpec((1, 1, d, dk), lambda i, kc, te, first, *_: (layer, te[i], 0, chunk(i, kc, first))),
                  pl.BlockSpec((1, 1, d, dk), lambda i, kc, te, first, *_: (layer, te[i], 0, chunk(i, kc, first))),
                  pl.BlockSpec((1, 1, dk, d), lambda i, kc, te, first, *_: (layer, te[i], chunk(i, kc, first), 0))],
        out_specs=pl.BlockSpec((tm * seg, LANES), lambda i, kc, *_: (i, 0)),
        scratch_shapes=[pltpu.VMEM((d, 2 * de), BF16), pltpu.VMEM((de, d), BF16),
                        pltpu.VMEM((2, tm * seg, LANES), jnp.uint32), pltpu.SemaphoreType.DMA((2,))],
    )
    return pl.pallas_call(
        functools.partial(_expert_kernel, nk=nk, dk=dk, tm=tm),
        out_shape=jax.ShapeDtypeStruct((p * seg, LANES), jnp.uint32),
        grid_spec=grid_spec,
        compiler_params=_cparams("arbitrary", "arbitrary"),
        name="moe_experts",
    )(tile_expert, tile_first, n_used, src, xm, w_gate, w_up, w_down)


def _combine_kernel(dest_ref, h_ref, ys_hbm, wt_ref, g_ref, *rest, emit_h, m):
    out_refs, (ybuf, sem) = rest[:-2], rest[-2:]
    n_ref = out_refs[-1]
    i = pl.program_id(0)
    slot = lax.rem(i, 2)
    tm, d = h_ref.shape
    half = d // 2
    seg = half // LANES

    def start_gather(step, s):
        for k in range(TOP_K):
            _start_token_gather(dest_ref, k * m + step * tm, tm, ys_hbm, ybuf.at[s], k * tm, sem.at[s], seg)

    @pl.when(i == 0)
    def _():
        start_gather(0, 0)

    @pl.when(i + 1 < pl.num_programs(0))
    def _():
        start_gather(i + 1, 1 - slot)

    _wait_token_gather(TOP_K * tm, ys_hbm, ybuf.at[slot], sem.at[slot], seg)
    wt = wt_ref[...]
    w0, w1 = wt[:, 0:1], wt[:, 1:2]
    yb = ybuf.at[slot]
    a0, b0 = _unpack_bf16_pair(_load_token_rows(yb.at[pl.ds(0, tm * seg), :], seg))
    a1, b1 = _unpack_bf16_pair(_load_token_rows(yb.at[pl.ds(tm * seg, tm * seg), :], seg))
    lo = h_ref[:, :half] + w0 * a0 + w1 * a1
    hi = h_ref[:, half:] + w0 * b0 + w1 * b1
    if emit_h:
        out_refs[0][:, :half] = lo
        out_refs[0][:, half:] = hi
    ms = (jnp.sum(lo * lo, axis=-1, keepdims=True) + jnp.sum(hi * hi, axis=-1, keepdims=True)) * (1.0 / d)
    scale = lax.rsqrt(ms + RMS_EPS) * wt[:, TOP_K:TOP_K + 1]
    n_ref[:, :half] = (lo * scale * g_ref[:, :half]).astype(n_ref.dtype)
    n_ref[:, half:] = (hi * scale * g_ref[:, half:]).astype(n_ref.dtype)


def combine(h, ys, dest, wt, g_next, norm_dtype, emit_h, blocks_per_seq, tm=POINT_TILE):
    m, d = h.shape
    seg = d // 2 // LANES
    if emit_h:
        row = pl.BlockSpec((tm, d), lambda i, dest: (i, 0))
        outs = [jax.ShapeDtypeStruct((m, d), F32), jax.ShapeDtypeStruct((m, d), norm_dtype)]
        out_specs = [row, row]
    else:
        tm = CHUNK
        row = pl.BlockSpec((tm, d), lambda i, dest: (i, 0))
        nb = blocks_per_seq
        outs = [jax.ShapeDtypeStruct((m // nb * (nb - 1), d), norm_dtype)]
        out_specs = [pl.BlockSpec((tm, d), lambda i, dest: ((i // nb) * (nb - 1) + jnp.maximum(i % nb - 1, 0), 0))]
    grid_spec = pltpu.PrefetchScalarGridSpec(
        num_scalar_prefetch=1,
        grid=(m // tm,),
        in_specs=[row, pl.BlockSpec(memory_space=pl.ANY), pl.BlockSpec((tm, LANES), lambda i, dest: (i, 0)),
                  pl.BlockSpec((1, d), lambda i, dest: (0, 0))],
        out_specs=out_specs,
        scratch_shapes=[pltpu.VMEM((2, TOP_K * tm * seg, LANES), jnp.uint32), pltpu.SemaphoreType.DMA((2,))],
    )
    return pl.pallas_call(
        functools.partial(_combine_kernel, emit_h=emit_h, m=m),
        out_shape=outs,
        grid_spec=grid_spec,
        compiler_params=_cparams("arbitrary"),
        name="moe_combine",
    )(dest, h, ys, wt, g_next.reshape(1, d))


def hier_moe(h, norm_g, wg, bg, we, be, w_gate, w_up, w_down, layer, g_next, norm_dtype, emit_h, row_valid,
             blocks_per_seq, tm=MOE_TILE):
    m, d = h.shape
    xm, eid_slab, wt = router(h, norm_g, wg, bg, we, be)
    wt = wt.at[:, TOP_K].set(row_valid)
    npair = m * TOP_K
    eid = eid_slab[:, :TOP_K].T.reshape(npair)
    onehot = (eid[:, None] == jnp.arange(N_EXPERTS, dtype=jnp.int32)[None, :]).astype(jnp.int32)
    counts = jnp.sum(onehot, axis=0)
    rank = jnp.sum((jnp.cumsum(onehot, axis=0) - 1) * onehot, axis=1)
    padded = ((counts + tm - 1) // tm) * tm
    ends = jnp.cumsum(padded)
    starts = ends - padded
    dest = jnp.sum(onehot * starts[None, :], axis=1) + rank
    p_rows = ((npair + tm - 1) // tm + N_EXPERTS) * tm
    pair_token = jnp.arange(npair, dtype=jnp.int32) % m
    src = jnp.zeros((p_rows,), jnp.int32).at[dest].set(pair_token, mode="promise_in_bounds",
                                                         unique_indices=True)
    n_tiles = p_rows // tm
    tile_start = jnp.arange(n_tiles, dtype=jnp.int32) * tm
    n_used = (ends[-1] // tm).astype(jnp.int32).reshape(1)
    last_used = jnp.maximum(n_used[0] - 1, 0) * tm
    tile_expert = jnp.sum((ends[None, :] <= jnp.minimum(tile_start, last_used)[:, None]).astype(jnp.int32), axis=1)
    tile_expert = jnp.minimum(tile_expert, N_EXPERTS - 1)
    tile_first = jnp.concatenate([jnp.ones((1,), jnp.int32),
                                  (tile_expert[1:] != tile_expert[:-1]).astype(jnp.int32)])
    ys = expert_ffn(xm, src, tile_expert, tile_first, n_used, w_gate, w_up, w_down, layer, tm)
    return combine(h, ys, dest.astype(jnp.int32), wt, g_next, norm_dtype, emit_h, blocks_per_seq)


def kernel(x, meta_tokens, final_norm_g, norm1_g, w_in, ret_gn_g, lru_conv_w, lru_conv_b, lru_wa, lru_ba,
           lru_wx, lru_bx, lru_lambda, rwkv_mu, rwkv_w0, rwkv_w2, rwkv_a0, rwkv_a2, rwkv_g2, rwkv_k_k,
           rwkv_k_a, rwkv_r_k, rwkv_ln_g, rwkv_ln_b, w_branch, w_out, norm2_g, moe_wg, moe_bg, moe_we,
           moe_be, moe_w_gate, moe_w_up, moe_w_down):
    b, seq, d = x.shape
    depth = w_in.shape[0]
    w = ret_gn_g.shape[1]
    t = N_META + seq
    tp = ((t + CHUNK - 1) // CHUNK) * CHUNK
    m = b * tp
    rwkv_cols = rwkv_mu.shape[1]
    rwkv_pad = 3 * w + LANES + 2 * LANES
    c_ret, c_lru, c_rwkv = 4 * w, 6 * w, 6 * w + rwkv_cols
    pad = tp - t
    row_valid = (jnp.arange(m, dtype=jnp.int32) % tp >= pad).astype(F32)

    h, a = embed(x, meta_tokens, norm1_g[0], pad)
    for l in range(depth):
        wl = w_in[l]
        w_ret = wl[:, :c_ret].astype(BF16)
        w_lru = wl[:, c_ret:c_lru].astype(BF16)
        w_rwkv = jnp.pad(wl[:, c_lru:c_rwkv], ((0, 0), (0, rwkv_pad - rwkv_cols))).astype(BF16)

        p_ret = matmul(a, w_ret, BF16, tm=ROW_TILE, tn=1024, name="proj_ret")
        p_lru = matmul(a, w_lru, F32, tm=ROW_TILE, tn=1024, name="proj_lru")
        p_rwkv = matmul(a, w_rwkv, F32, tm=ROW_TILE, tn=rwkv_pad // 3, name="proj_rwkv")
        gates = gate_matmul(a, wl[:, c_rwkv:], tm=ROW_TILE, tn=2048)

        o_ret = retention(p_ret.reshape(b, tp, 4 * w), ret_gn_g[l], pad)
        o_lru = rg_lru(p_lru.reshape(b, tp, 2 * w), lru_conv_w[l], lru_conv_b[l], lru_wa[l], lru_ba[l],
                       lru_wx[l], lru_bx[l], lru_lambda[l], pad)
        o_rwkv = rwkv_mix(p_rwkv.reshape(b, tp, rwkv_pad), rwkv_mu[l], rwkv_w0[l], rwkv_w2[l], rwkv_a0[l],
                          rwkv_a2[l], rwkv_g2[l], rwkv_k_k[l], rwkv_k_a[l], rwkv_r_k[l], rwkv_ln_g[l],
                          rwkv_ln_b[l], w)

        merged = merge(o_ret.reshape(m, w), o_lru.reshape(m, w), o_rwkv.reshape(m, w),
                       w_branch[l].astype(BF16), gates)
        h = matmul(merged, w_out[l].astype(BF16), F32, tm=ROW_TILE, tn=1024, residual=h, name="proj_out")

        moe_args = (h, norm2_g[l], moe_wg[l], moe_bg[l], moe_we[l], moe_be[l], moe_w_gate, moe_w_up, moe_w_down, l)
        if l + 1 < depth:
            h, a = hier_moe(*moe_args, norm1_g[l + 1], BF16, True, row_valid, tp // CHUNK)
        else:
            (out,) = hier_moe(*moe_args, final_norm_g, F32, False, row_valid, tp // CHUNK)

    return out.reshape(b, seq, d)
```
